```python
import math
import jax, jax.numpy as jnp
from jax import lax
import numpy as np

D_MODEL = 2048
BATCH = 2
SEQ = 4096
DEPTH = 1

D_PLE = 256
D_MIX = D_MODEL
POOL_WIDTH = D_MIX // 2
POOL_GROUPS = 4
POOL_GC = POOL_WIDTH // POOL_GROUPS
POOL_WINDOWS = (2, 4, 8, 16)
ATTN_HEADS = 8
HEAD_DIM = 128
ATTN_WIDTH = ATTN_HEADS * HEAD_DIM
IDX_HEADS = 16
IDX_DIM = 64
TOPK_MAX = 256
N_BUCKETS = 32
MAX_DISTANCE = 128
D_FF = 5632
CONV_WIDTH = 3
Q_BLOCK = 128
EPS = 1e-6
IN_SIZES = (POOL_WIDTH, ATTN_WIDTH, ATTN_WIDTH, ATTN_WIDTH,
            IDX_HEADS * IDX_DIM, IDX_DIM, IDX_HEADS)
D_IN = sum(IN_SIZES)

kernel_name = "hymba_pool_dsa_convffn_ple"


def rmsnorm(x, g):
    xf = x.astype(jnp.float32)
    y = xf * lax.rsqrt(jnp.mean(xf * xf, axis=-1, keepdims=True) + EPS)
    return (y * g.astype(jnp.float32)).astype(x.dtype)


def t5_bucket(dist):
    n = jnp.maximum(dist, 0)
    max_exact = N_BUCKETS // 2
    nf = jnp.maximum(n, 1).astype(jnp.float32)
    large = max_exact + (jnp.log(nf / max_exact) / math.log(MAX_DISTANCE / max_exact)
                         * (N_BUCKETS - max_exact)).astype(jnp.int32)
    large = jnp.minimum(large, N_BUCKETS - 1)
    return jnp.where(n < max_exact, n, large)


def pool_mixer(u, w_pool, pool_scale):
    B, S, _ = u.shape
    uf = u.astype(jnp.float32).reshape(B, S, POOL_GROUPS, POOL_GC)
    cs0 = jnp.concatenate([jnp.zeros((B, 1, POOL_GROUPS, POOL_GC), jnp.float32),
                           jnp.cumsum(uf, axis=1)], axis=1)
    t = jnp.arange(S)
    means = []
    for g, w in enumerate(POOL_WINDOWS):
        lo = jnp.maximum(t + 1 - w, 0)
        wsum = cs0[:, t + 1, g] - cs0[:, lo, g]
        count = (t + 1 - lo).astype(jnp.float32)
        means.append(wsum / count[None, :, None])
    pooled = (jnp.stack(means, axis=2) - uf).astype(u.dtype)
    y = jnp.einsum('bsgc,gcd->bsgd', pooled, w_pool).reshape(B, S, POOL_WIDTH)
    return y * pool_scale


def dsa_attention(q, k, v, q_idx, k_idx, w_idx, rel_bias):
    B, S = q.shape[0], q.shape[1]
    topk = min(TOPK_MAX, S // 4)
    nb = S // Q_BLOCK
    s_pos = jnp.arange(S)
    idx_scale = (IDX_HEADS ** -0.5) * (IDX_DIM ** -0.5)
    k_idx_f = k_idx.astype(jnp.float32)

    def to_blocks(a):
        return a.reshape(B, nb, Q_BLOCK, *a.shape[2:]).swapaxes(0, 1)

    gather = jax.vmap(lambda kb, ib: kb[ib])

    def one_block(args):
        blk, qb, qib, wb = args
        t_pos = blk * Q_BLOCK + jnp.arange(Q_BLOCK)
        dots = jnp.einsum('bqhd,bsd->bhqs', qib.astype(jnp.float32), k_idx_f)
        score = jnp.einsum('bqh,bhqs->bqs', wb.astype(jnp.float32) * idx_scale,
                           jax.nn.relu(dots))
        causal = s_pos[None, :] <= t_pos[:, None]
        score = jnp.where(causal[None], score, -jnp.inf)
        _, sel = lax.top_k(score, topk)
        k_sel = gather(k, sel)
        v_sel = gather(v, sel)
        dist = t_pos[None, :, None] - sel
        valid = dist >= 0
        bias = rel_bias[t5_bucket(dist)].astype(jnp.float32)
        logits = (jnp.einsum('bqhd,bqkhd->bhqk', qb.astype(jnp.float32),
                             k_sel.astype(jnp.float32)) * (HEAD_DIM ** -0.5)
                  + bias.transpose(0, 3, 1, 2))
        logits = jnp.where(valid[:, None], logits, -jnp.inf)
        probs = jax.nn.softmax(logits, axis=-1)
        out = jnp.einsum('bhqk,bqkhd->bqhd', probs, v_sel.astype(jnp.float32))
        return out.astype(q.dtype)

    outs = lax.map(one_block, (jnp.arange(nb), to_blocks(q), to_blocks(q_idx), to_blocks(w_idx)))
    return outs.swapaxes(0, 1).reshape(B, S, ATTN_HEADS * HEAD_DIM)


def conv_ffn(h, w_up, conv_w, conv_b, w_down):
    S = h.shape[1]
    u = h @ w_up
    upad = jnp.pad(u, ((0, 0), (CONV_WIDTH - 1, 0), (0, 0)))
    uc = sum(upad[:, j:j + S] * conv_w[j] for j in range(CONV_WIDTH)) + conv_b
    gate, val = jnp.split(uc, 2, axis=-1)
    return (jax.nn.silu(gate) * val) @ w_down


def setup_inputs(seed: int = 0) -> dict:
    key = jax.random.key(seed)
    ks = jax.random.split(key, 20)
    f32 = jnp.float32
    nrm = lambda k, shape, scale: jax.random.normal(k, shape, f32) * scale
    return {
        "x": nrm(ks[0], (BATCH, SEQ, D_MODEL), 1.0),
        "p": nrm(ks[1], (DEPTH, BATCH, SEQ, D_PLE), 1.0),
        "g_mix": 1.0 + nrm(ks[2], (DEPTH, D_MODEL), 0.02),
        "w_in": nrm(ks[3], (DEPTH, D_MODEL, D_IN), D_MODEL ** -0.5),
        "w_pool": nrm(ks[4], (DEPTH, POOL_GROUPS, POOL_GC, POOL_GC), POOL_GC ** -0.5),
        "pool_scale": 1.0 + nrm(ks[5], (DEPTH, POOL_WIDTH), 0.02),
        "rel_bias": nrm(ks[6], (N_BUCKETS, ATTN_HEADS), 0.5),
        "w_out": nrm(ks[7], (DEPTH, D_MIX, D_MODEL), D_MIX ** -0.5),
        "g_ffn": 1.0 + nrm(ks[8], (DEPTH, D_MODEL), 0.02),
        "w_up": nrm(ks[9], (DEPTH, D_MODEL, 2 * D_FF), D_MODEL ** -0.5),
        "conv_w": 1.0 / CONV_WIDTH + nrm(ks[10], (DEPTH, CONV_WIDTH, 2 * D_FF), 0.2),
        "conv_b": nrm(ks[11], (DEPTH, 2 * D_FF), 0.01),
        "w_down": nrm(ks[12], (DEPTH, D_FF, D_MODEL), D_FF ** -0.5),
        "g_ple": 1.0 + nrm(ks[13], (DEPTH, D_MODEL), 0.02),
        "w_ple_gate": nrm(ks[14], (DEPTH, D_MODEL, D_MODEL), D_MODEL ** -0.5),
        "w_ple_proj": nrm(ks[15], (DEPTH, D_PLE, D_MODEL), D_PLE ** -0.5),
        "g_final": 1.0 + nrm(ks[16], (D_MODEL,), 0.02),
    }


def reference(x, p, g_mix, w_in, w_pool, pool_scale, rel_bias, w_out, g_ffn, w_up,
              conv_w, conv_b, w_down, g_ple, w_ple_gate, w_ple_proj, g_final):
    B, S, _ = x.shape
    offs = np.cumsum((0,) + IN_SIZES).tolist()
    for i in range(DEPTH):
        h = rmsnorm(x, g_mix[i])
        proj = h @ w_in[i]
        u_pool, q, k, v, q_idx, k_idx, w_idx = [proj[..., offs[j]:offs[j + 1]]
                                                 for j in range(len(IN_SIZES))]
        pool_out = pool_mixer(u_pool, w_pool[i], pool_scale[i])
        attn_out = dsa_attention(q.reshape(B, S, ATTN_HEADS, HEAD_DIM),
                                 k.reshape(B, S, ATTN_HEADS, HEAD_DIM),
                                 v.reshape(B, S, ATTN_HEADS, HEAD_DIM),
                                 q_idx.reshape(B, S, IDX_HEADS, IDX_DIM),
                                 k_idx, w_idx, rel_bias)
        x = x + jnp.concatenate([pool_out, attn_out], axis=-1) @ w_out[i]
        x = x + conv_ffn(rmsnorm(x, g_ffn[i]), w_up[i], conv_w[i], conv_b[i], w_down[i])
        gate = jax.nn.sigmoid(rmsnorm(x, g_ple[i]) @ w_ple_gate[i])
        x = x + (p[i] @ w_ple_proj[i]) * gate
    return rmsnorm(x, g_final)
```

```python
import functools
import math

import numpy as np
import jax
import jax.numpy as jnp
from jax import lax
from jax.experimental import pallas as pl
from jax.experimental.pallas import tpu as pltpu

F32 = jnp.float32
BF16 = jnp.bfloat16
I32 = jnp.int32

D_MODEL = 2048
D_PLE = 256
POOL_WIDTH = 1024
POOL_GROUPS = 4
POOL_GC = 256
POOL_WINDOWS = (2, 4, 8, 16)
ATTN_HEADS = 8
HEAD_DIM = 128
ATTN_WIDTH = 1024
IDX_HEADS = 16
IDX_DIM = 64
TOPK_MAX = 256
N_BUCKETS = 32
MAX_DISTANCE = 128
D_FF = 5632
CONV_WIDTH = 3
EPS = 1e-6

V7X_VMEM_BYTES = 64 * 1024 * 1024
SUBLANES = 8
LANES = 128
VMEM_LIMIT = V7X_VMEM_BYTES - 8 * 1024 * 1024

TM_IN = 1024
TN_IN = 512
SMALL_W = LANES
QB = 256
KC = 256
TM_MIX = 512
POOL_HALO = 16
TM_FFN = 512
TF_FFN = 512
TM_PLE = 512

INT_MIN = -2 ** 31
NEG_BIG = -1e30


def _bucket_lower_bounds():
    n = np.arange(0, 4 * MAX_DISTANCE, dtype=np.int64)
    max_exact = N_BUCKETS // 2
    nf = np.maximum(n, 1).astype(np.float32)
    large = max_exact + (np.log(nf / np.float32(max_exact)) / np.float32(math.log(MAX_DISTANCE / max_exact))
                         * np.float32(N_BUCKETS - max_exact)).astype(np.int32)
    large = np.minimum(large, N_BUCKETS - 1)
    bucket = np.where(n < max_exact, n, large)
    assert np.all(np.diff(bucket) >= 0)
    lows = [int(np.argmax(bucket == b)) for b in range(N_BUCKETS)]
    assert all(bucket[lo] == b for b, lo in enumerate(lows))
    return lows


BUCKET_LO = _bucket_lower_bounds()


def _rms(x, g):
    ms = jnp.mean(x * x, axis=-1, keepdims=True)
    return (x * lax.rsqrt(ms + EPS)) * g


def _inproj_kernel(x_ref, g_ref, w_ref, ws_ref, pool_ref, big_ref, small_ref, h_s):
    n = pl.program_id(1)

    @pl.when(n == 0)
    def _():
        hb = _rms(x_ref[...], g_ref[...]).astype(BF16)
        h_s[...] = hb
        small_ref[...] = jnp.dot(hb, ws_ref[...], preferred_element_type=F32)

    acc = jnp.dot(h_s[...], w_ref[...], preferred_element_type=F32)
    n_pool = POOL_WIDTH // TN_IN

    @pl.when(n < n_pool)
    def _():
        pool_ref[...] = acc

    @pl.when(n >= n_pool)
    def _():
        big_ref[...] = acc.astype(BF16)


def _inproj(x2d, g, w_main, w_small):
    m = x2d.shape[0]
    n_main = w_main.shape[1]
    n_pool = POOL_WIDTH // TN_IN
    grid = (m // TM_IN, n_main // TN_IN)
    return pl.pallas_call(
        _inproj_kernel,
        grid=grid,
        in_specs=[
            pl.BlockSpec((TM_IN, D_MODEL), lambda i, j: (i, 0)),
            pl.BlockSpec((1, D_MODEL), lambda i, j: (0, 0)),
            pl.BlockSpec((D_MODEL, TN_IN), lambda i, j: (0, j)),
            pl.BlockSpec((D_MODEL, SMALL_W), lambda i, j: (0, 0)),
        ],
        out_specs=[
            pl.BlockSpec((TM_IN, TN_IN), lambda i, j: (i, jnp.minimum(j, n_pool - 1))),
            pl.BlockSpec((TM_IN, TN_IN), lambda i, j: (i, jnp.maximum(j - n_pool, 0))),
            pl.BlockSpec((TM_IN, SMALL_W), lambda i, j: (i, 0)),
        ],
        out_shape=[
            jax.ShapeDtypeStruct((m, POOL_WIDTH), F32),
            jax.ShapeDtypeStruct((m, n_main - POOL_WIDTH), BF16),
            jax.ShapeDtypeStruct((m, SMALL_W), F32),
        ],
        scratch_shapes=[pltpu.VMEM((TM_IN, D_MODEL), BF16)],
        compiler_params=pltpu.CompilerParams(
            dimension_semantics=("arbitrary", "arbitrary"), vmem_limit_bytes=VMEM_LIMIT),
        name="inproj",
    )(x2d, g, w_main, w_small)


def _dsa_kernel(rel_ref, qT_ref, qiT_ref, wT_ref, kidx_ref, k_ref, vT_ref, o_ref, keys_s, bias_s,
                *, topk):
    b = pl.program_id(0)
    i = pl.program_id(1)
    n_ch = i + 1

    @pl.when((b == 0) & (i == 0))
    def _():
        r = lax.broadcasted_iota(I32, (KC, QB), 0)
        c = lax.broadcasted_iota(I32, (KC, QB), 1)
        for off in range(2):
            dist = jnp.maximum(c - r + off * QB, 0)

            def head_body(h, carry, dist=dist, off=off):
                t = jnp.full((KC, QB), rel_ref[0, h], F32)
                for bk in range(1, N_BUCKETS):
                    t = jnp.where(dist >= BUCKET_LO[bk], rel_ref[bk, h], t)
                bias_s[off, h] = t
                return carry

            lax.fori_loop(0, ATTN_HEADS, head_body, 0)

    w_scaled = wT_ref[...] * ((IDX_HEADS ** -0.5) * (IDX_DIM ** -0.5))
    row = lax.broadcasted_iota(I32, (KC, QB), 0)
    col = lax.broadcasted_iota(I32, (KC, QB), 1)

    def idx_body(m, carry):
        kc = kidx_ref[pl.ds(pl.multiple_of(m * KC, KC), KC), :]
        acc = jnp.zeros((KC, QB), F32)
        for h in range(IDX_HEADS):
            d = jnp.dot(kc, qiT_ref[h * IDX_DIM:(h + 1) * IDX_DIM, :],
                        preferred_element_type=F32)
            acc = acc + w_scaled[h:h + 1, :] * jnp.maximum(d, 0.0)
        bits = pltpu.bitcast(acc, I32)
        key = jnp.where(bits >= 0, bits, bits ^ 0x7FFFFFFF)
        causal = (m * KC + row) <= (i * QB + col)
        keys_s[m] = jnp.where(causal, key, INT_MIN)
        return carry

    lax.fori_loop(0, n_ch, idx_body, 0)

    def bit_body(it, prefix):
        cand = prefix + jnp.left_shift(jnp.int32(1), 31 - it)

        def cnt_body(m, cnt):
            ge = (keys_s[m] >= cand).astype(I32)
            return cnt + jnp.sum(ge.reshape(KC // SUBLANES, SUBLANES, QB), axis=0)

        cnt8 = lax.fori_loop(0, n_ch, cnt_body, jnp.zeros((SUBLANES, QB), I32))
        cnt = jnp.sum(cnt8, axis=0, keepdims=True)
        return jnp.where(cnt >= topk, cand, prefix)

    prefix = lax.fori_loop(0, 32, bit_body, jnp.full((1, QB), INT_MIN, I32))
    thr = jnp.maximum(prefix, INT_MIN + 1)

    def mask_body(m, carry):
        madd = jnp.where(keys_s[m] >= thr, 0.0, NEG_BIG).astype(F32)
        keys_s[m] = pltpu.bitcast(madd, I32)
        return carry

    lax.fori_loop(0, n_ch, mask_body, 0)

    scale = HEAD_DIM ** -0.5
    n_far = jnp.maximum(i - 1, 0)
    for h in range(ATTN_HEADS):
        qh = qT_ref[h * HEAD_DIM:(h + 1) * HEAD_DIM, :]
        far_bias = rel_ref[N_BUCKETS - 1, h]

        def att_body(m, carry, near, h=h, qh=qh, far_bias=far_bias):
            mx, l, acc = carry
            kh = k_ref[pl.ds(pl.multiple_of(m * KC, KC), KC), h * HEAD_DIM:(h + 1) * HEAD_DIM]
            s = jnp.dot(kh, qh, preferred_element_type=F32)
            madd = pltpu.bitcast(keys_s[m], F32)
            if near:
                s = s * scale + (madd + bias_s[i - m, h])
            else:
                s = s * scale + (madd + far_bias)
            m_new = jnp.maximum(mx, jnp.max(s, axis=0, keepdims=True))
            alpha = jnp.exp(mx - m_new)
            p = jnp.exp(s - m_new)
            l = alpha * l + jnp.sum(p, axis=0, keepdims=True)
            vh = vT_ref[m, h * HEAD_DIM:(h + 1) * HEAD_DIM, :]
            acc = alpha * acc + jnp.dot(vh, p.astype(BF16), preferred_element_type=F32)
            return m_new, l, acc

        init = (jnp.full((1, QB), -jnp.inf, F32), jnp.zeros((1, QB), F32),
                jnp.zeros((HEAD_DIM, QB), F32))
        carry = lax.fori_loop(0, n_far, functools.partial(att_body, near=False), init)
        mx, l, acc = lax.fori_loop(n_far, n_ch, functools.partial(att_body, near=True), carry)
        out_t = acc / l
        o_ref[:, h * HEAD_DIM:(h + 1) * HEAD_DIM] = out_t.T.astype(o_ref.dtype)


def _dsa(rel_bias, qT, qiT, wT, kidx, big, vT4, seq):
    bsz = qT.shape[0]
    topk = min(TOPK_MAX, seq // 4)
    n_q = seq // QB
    kernel = functools.partial(_dsa_kernel, topk=topk)
    k_col_block = 1
    return pl.pallas_call(
        kernel,
        grid=(bsz, n_q),
        in_specs=[
            pl.BlockSpec(memory_space=pltpu.SMEM),
            pl.BlockSpec((None, ATTN_WIDTH, QB), lambda b, i: (b, 0, i)),
            pl.BlockSpec((None, IDX_HEADS * IDX_DIM, QB), lambda b, i: (b, 0, i)),
            pl.BlockSpec((None, IDX_HEADS, QB), lambda b, i: (b, 0, i)),
            pl.BlockSpec((None, seq, IDX_DIM), lambda b, i: (b, 0, 0)),
            pl.BlockSpec((None, seq, ATTN_WIDTH), lambda b, i: (b, 0, k_col_block)),
            pl.BlockSpec((None, seq // KC, ATTN_WIDTH, KC), lambda b, i: (b, 0, 0, 0)),
        ],
        out_specs=pl.BlockSpec((None, QB, ATTN_WIDTH), lambda b, i: (b, i, 0)),
        out_shape=jax.ShapeDtypeStruct((bsz, seq, ATTN_WIDTH), BF16),
        scratch_shapes=[
            pltpu.VMEM((seq // KC, KC, QB), I32),
            pltpu.VMEM((2, ATTN_HEADS, KC, QB), F32),
        ],
        compiler_params=pltpu.CompilerParams(
            dimension_semantics=("arbitrary", "arbitrary"), vmem_limit_bytes=VMEM_LIMIT),
        name="dsa",
    )(rel_bias, qT, qiT, wT, kidx, big, vT4)


def _mix_kernel(x_ref, u_ref, halo_ref, attn_ref, wpool_ref, pscale_ref, wout_ref, g_ref,
                x1_ref, h2_ref, ext_s, *, seq):
    m = pl.program_id(0)
    tiles_per_seq = seq // TM_MIX
    first = (m % tiles_per_seq) == 0
    ext_s[0:POOL_HALO, :] = jnp.where(first, 0.0, halo_ref[...])
    ext_s[POOL_HALO:, :] = u_ref[...]

    t_pos = (m % tiles_per_seq) * TM_MIX + lax.broadcasted_iota(I32, (TM_MIX, POOL_GC), 0)
    acc = x_ref[...] + jnp.dot(attn_ref[...], wout_ref[POOL_WIDTH:, :], preferred_element_type=F32)
    for g, w in enumerate(POOL_WINDOWS):
        cols = slice(g * POOL_GC, (g + 1) * POOL_GC)
        u = ext_s[POOL_HALO:, cols]
        wsum = u
        for j in range(1, w):
            wsum = wsum + ext_s[pl.ds(POOL_HALO - j, TM_MIX), cols]
        count = jnp.minimum(t_pos + 1, w).astype(F32)
        pooled = (wsum / count - u).astype(BF16)
        y = jnp.dot(pooled, wpool_ref[g], preferred_element_type=F32) * pscale_ref[:, cols]
        acc = acc + jnp.dot(y.astype(BF16), wout_ref[cols, :], preferred_element_type=F32)
    x1_ref[...] = acc
    h2_ref[...] = _rms(acc, g_ref[...]).astype(BF16)


def _mix(x2d, u_pool, attn2d, w_pool, pool_scale, w_out, g_ffn, seq):
    m = x2d.shape[0]
    halo_per_tile = TM_MIX // POOL_HALO
    kernel = functools.partial(_mix_kernel, seq=seq)
    return pl.pallas_call(
        kernel,
        grid=(m // TM_MIX,),
        in_specs=[
            pl.BlockSpec((TM_MIX, D_MODEL), lambda i: (i, 0)),
            pl.BlockSpec((TM_MIX, POOL_WIDTH), lambda i: (i, 0)),
            pl.BlockSpec((POOL_HALO, POOL_WIDTH), lambda i: (jnp.maximum(i * halo_per_tile - 1, 0), 0)),
            pl.BlockSpec((TM_MIX, ATTN_WIDTH), lambda i: (i, 0)),
            pl.BlockSpec((POOL_GROUPS, POOL_GC, POOL_GC), lambda i: (0, 0, 0)),
            pl.BlockSpec((1, POOL_WIDTH), lambda i: (0, 0)),
            pl.BlockSpec((D_MODEL, D_MODEL), lambda i: (0, 0)),
            pl.BlockSpec((1, D_MODEL), lambda i: (0, 0)),
        ],
        out_specs=[
            pl.BlockSpec((TM_MIX, D_MODEL), lambda i: (i, 0)),
            pl.BlockSpec((TM_MIX, D_MODEL), lambda i: (i, 0)),
        ],
        out_shape=[
            jax.ShapeDtypeStruct((m, D_MODEL), F32),
            jax.ShapeDtypeStruct((m, D_MODEL), BF16),
        ],
        scratch_shapes=[pltpu.VMEM((TM_MIX + POOL_HALO, POOL_WIDTH), F32)],
        compiler_params=pltpu.CompilerParams(
            dimension_semantics=("arbitrary",), vmem_limit_bytes=VMEM_LIMIT),
        name="mix",
    )(x2d, u_pool, u_pool, attn2d, w_pool, pool_scale, w_out, g_ffn)


def _ffn_kernel(h_ref, x1_ref, wg_ref, wv_ref, cwg_ref, cwv_ref, cbg_ref, cbv_ref, wd_ref,
                o_ref, acc_s, u_s, carry_s, *, seq):
    m = pl.program_id(0)
    f = pl.program_id(1)
    n_f = pl.num_programs(1)
    tiles_per_seq = seq // TM_FFN
    first = (m % tiles_per_seq) == 0
    h = h_ref[...]

    u_s[0:SUBLANES, :] = jnp.where(first, 0.0, carry_s[f])
    u_s[SUBLANES:, 0:TF_FFN] = jnp.dot(h, wg_ref[...], preferred_element_type=F32)
    u_s[SUBLANES:, TF_FFN:] = jnp.dot(h, wv_ref[...], preferred_element_type=F32)
    carry_s[f] = u_s[TM_FFN:, :]

    def conv(cols, cw_ref, cb_ref):
        out = cb_ref[...]
        for j in range(CONV_WIDTH):
            shift = CONV_WIDTH - 1 - j
            out = out + u_s[pl.ds(SUBLANES - shift, TM_FFN), cols] * cw_ref[j:j + 1, :]
        return out

    gate = conv(slice(0, TF_FFN), cwg_ref, cbg_ref)
    val = conv(slice(TF_FFN, 2 * TF_FFN), cwv_ref, cbv_ref)
    act = (gate * jax.nn.sigmoid(gate) * val).astype(BF16)
    contrib = jnp.dot(act, wd_ref[...], preferred_element_type=F32)

    @pl.when(f == 0)
    def _():
        acc_s[...] = x1_ref[...] + contrib

    @pl.when(f > 0)
    def _():
        acc_s[...] += contrib

    @pl.when(f == n_f - 1)
    def _():
        o_ref[...] = acc_s[...]


def _ffn(h2, x1, w_up, conv_w, conv_b, w_down, seq):
    m = h2.shape[0]
    n_f = D_FF // TF_FFN
    kernel = functools.partial(_ffn_kernel, seq=seq)
    return pl.pallas_call(
        kernel,
        grid=(m // TM_FFN, n_f),
        in_specs=[
            pl.BlockSpec((TM_FFN, D_MODEL), lambda i, f: (i, 0)),
            pl.BlockSpec((TM_FFN, D_MODEL), lambda i, f: (i, 0)),
            pl.BlockSpec((D_MODEL, TF_FFN), lambda i, f: (0, f)),
            pl.BlockSpec((D_MODEL, TF_FFN), lambda i, f: (0, f + n_f)),
            pl.BlockSpec((CONV_WIDTH, TF_FFN), lambda i, f: (0, f)),
            pl.BlockSpec((CONV_WIDTH, TF_FFN), lambda i, f: (0, f + n_f)),
            pl.BlockSpec((1, TF_FFN), lambda i, f: (0, f)),
            pl.BlockSpec((1, TF_FFN), lambda i, f: (0, f + n_f)),
            pl.BlockSpec((TF_FFN, D_MODEL), lambda i, f: (f, 0)),
        ],
        out_specs=pl.BlockSpec((TM_FFN, D_MODEL), lambda i, f: (i, 0)),
        out_shape=jax.ShapeDtypeStruct((m, D_MODEL), F32),
        scratch_shapes=[
            pltpu.VMEM((TM_FFN, D_MODEL), F32),
            pltpu.VMEM((TM_FFN + SUBLANES, 2 * TF_FFN), F32),
            pltpu.VMEM((n_f, SUBLANES, 2 * TF_FFN), F32),
        ],
        compiler_params=pltpu.CompilerParams(
            dimension_semantics=("arbitrary", "arbitrary"), vmem_limit_bytes=VMEM_LIMIT),
        name="ffn",
    )(h2, x1, w_up, w_up, conv_w, conv_w, conv_b, conv_b, w_down)


def _ple_kernel(x_ref, p_ref, gp_ref, wgate_ref, wproj_ref, gf_ref, o_ref, *, last_layer):
    x = x_ref[...]
    hg = _rms(x, gp_ref[...]).astype(BF16)
    gate = jax.nn.sigmoid(jnp.dot(hg, wgate_ref[...], preferred_element_type=F32))
    emb = jnp.dot(p_ref[...].astype(BF16), wproj_ref[...], preferred_element_type=F32)
    x3 = x + emb * gate
    o_ref[...] = _rms(x3, gf_ref[...]) if last_layer else x3


def _ple(x2, p2d, g_ple, w_gate, w_proj, g_final, last_layer):
    m = x2.shape[0]
    return pl.pallas_call(
        functools.partial(_ple_kernel, last_layer=last_layer),
        grid=(m // TM_PLE,),
        in_specs=[
            pl.BlockSpec((TM_PLE, D_MODEL), lambda i: (i, 0)),
            pl.BlockSpec((TM_PLE, D_PLE), lambda i: (i, 0)),
            pl.BlockSpec((1, D_MODEL), lambda i: (0, 0)),
            pl.BlockSpec((D_MODEL, D_MODEL), lambda i: (0, 0)),
            pl.BlockSpec((D_PLE, D_MODEL), lambda i: (0, 0)),
            pl.BlockSpec((1, D_MODEL), lambda i: (0, 0)),
        ],
        out_specs=pl.BlockSpec((TM_PLE, D_MODEL), lambda i: (i, 0)),
        out_shape=jax.ShapeDtypeStruct((m, D_MODEL), F32),
        compiler_params=pltpu.CompilerParams(
            dimension_semantics=("arbitrary",), vmem_limit_bytes=VMEM_LIMIT),
        name="ple",
    )(x2, p2d, g_ple, w_gate, w_proj, g_final)


def kernel(x, p, g_mix, w_in, w_pool, pool_scale, rel_bias, w_out, g_ffn, w_up, conv_w, conv_b,
           w_down, g_ple, w_ple_gate, w_ple_proj, g_final):
    bsz, seq, _ = x.shape
    depth = w_in.shape[0]
    assert seq % QB == 0 and seq % TM_MIX == 0 and seq % TM_FFN == 0 and (bsz * seq) % TM_IN == 0
    n_main = POOL_WIDTH + 3 * ATTN_WIDTH + IDX_HEADS * IDX_DIM
    x2d = x.reshape(bsz * seq, D_MODEL)
    for i in range(depth):
        wi = w_in[i].astype(BF16)
        w_main = wi[:, :n_main]
        w_small = jnp.pad(wi[:, n_main:], ((0, 0), (0, SMALL_W - (IDX_DIM + IDX_HEADS))))
        u_pool, big, small = _inproj(x2d, g_mix[i].reshape(1, D_MODEL), w_main, w_small)

        big3 = big.reshape(bsz, seq, 4 * ATTN_WIDTH)
        qT = jnp.swapaxes(big3[:, :, 0:ATTN_WIDTH], 1, 2)
        vT4 = jnp.swapaxes(big3[:, :, 2 * ATTN_WIDTH:3 * ATTN_WIDTH].reshape(bsz, seq // KC, KC, ATTN_WIDTH), 2, 3)
        qiT = jnp.swapaxes(big3[:, :, 3 * ATTN_WIDTH:], 1, 2)
        small3 = small.reshape(bsz, seq, SMALL_W)
        kidx = small3[:, :, :IDX_DIM].astype(BF16)
        wT = jnp.swapaxes(small3[:, :, IDX_DIM:IDX_DIM + IDX_HEADS], 1, 2)
        attn = _dsa(rel_bias, qT, qiT, wT, kidx, big3, vT4, seq)

        x1, h2 = _mix(x2d, u_pool, attn.reshape(bsz * seq, ATTN_WIDTH), w_pool[i].astype(BF16),
                      pool_scale[i].reshape(1, POOL_WIDTH), w_out[i].astype(BF16),
                      g_ffn[i].reshape(1, D_MODEL), seq)
        x2 = _ffn(h2, x1, w_up[i].astype(BF16), conv_w[i], conv_b[i].reshape(1, 2 * D_FF),
                  w_down[i].astype(BF16), seq)
        x2d = _ple(x2, p[i].reshape(bsz * seq, D_PLE), g_ple[i].reshape(1, D_MODEL),
                   w_ple_gate[i].astype(BF16), w_ple_proj[i].astype(BF16),
                   g_final.reshape(1, D_MODEL), last_layer=(i == depth - 1))
    return x2d.reshape(bsz, seq, D_MODEL)
```

```python
import functools
import math

import numpy as np
import jax
import jax.numpy as jnp
from jax import lax
from jax.experimental import pallas as pl
from jax.experimental.pallas import tpu as pltpu

F32 = jnp.float32
BF16 = jnp.bfloat16
I32 = jnp.int32

D_MODEL = 2048
D_PLE = 256
POOL_WIDTH = 1024
POOL_GROUPS = 4
POOL_GC = 256
POOL_WINDOWS = (2, 4, 8, 16)
ATTN_HEADS = 8
HEAD_DIM = 128
ATTN_WIDTH = 1024
IDX_HEADS = 16
IDX_DIM = 64
TOPK_MAX = 256
N_BUCKETS = 32
MAX_DISTANCE = 128
D_FF = 5632
CONV_WIDTH = 3
EPS = 1e-6

V7X_VMEM_BYTES = 64 * 1024 * 1024
SUBLANES = 8
LANES = 128
VMEM_LIMIT = V7X_VMEM_BYTES - 8 * 1024 * 1024

TM_IN = 1024
TN_IN = 512
SMALL_W = LANES
QB = 256
KC = 256
TM_MIX = 512
POOL_HALO = 16
TM_FFN = 512
TF_FFN = 512
TM_PLE = 512

INT_MIN = -2 ** 31
NEG_BIG = -1e30


def _bucket_lower_bounds():
    n = np.arange(0, 4 * MAX_DISTANCE, dtype=np.int64)
    max_exact = N_BUCKETS // 2
    nf = np.maximum(n, 1).astype(np.float32)
    large = max_exact + (np.log(nf / np.float32(max_exact)) / np.float32(math.log(MAX_DISTANCE / max_exact))
                         * np.float32(N_BUCKETS - max_exact)).astype(np.int32)
    large = np.minimum(large, N_BUCKETS - 1)
    bucket = np.where(n < max_exact, n, large)
    assert np.all(np.diff(bucket) >= 0)
    lows = [int(np.argmax(bucket == b)) for b in range(N_BUCKETS)]
    assert all(bucket[lo] == b for b, lo in enumerate(lows))
    return lows


BUCKET_LO = _bucket_lower_bounds()


def _rms(x, g):
    ms = jnp.mean(x * x, axis=-1, keepdims=True)
    return (x * lax.rsqrt(ms + EPS)) * g


def _inproj_kernel(x_ref, g_ref, w_ref, ws_ref, pool_ref, big_ref, small_ref, h_s):
    n = pl.program_id(1)

    @pl.when(n == 0)
    def _():
        hb = _rms(x_ref[...], g_ref[...]).astype(BF16)
        h_s[...] = hb
        small_ref[...] = jnp.dot(hb, ws_ref[...], preferred_element_type=F32)

    acc = jnp.dot(h_s[...], w_ref[...], preferred_element_type=F32)
    n_pool = POOL_WIDTH // TN_IN

    @pl.when(n < n_pool)
    def _():
        pool_ref[...] = acc

    @pl.when(n >= n_pool)
    def _():
        big_ref[...] = acc.astype(BF16)


def _inproj(x2d, g, w_main, w_small):
    m = x2d.shape[0]
    n_main = w_main.shape[1]
    n_pool = POOL_WIDTH // TN_IN
    grid = (m // TM_IN, n_main // TN_IN)
    return pl.pallas_call(
        _inproj_kernel,
        grid=grid,
        in_specs=[
            pl.BlockSpec((TM_IN, D_MODEL), lambda i, j: (i, 0)),
            pl.BlockSpec((1, D_MODEL), lambda i, j: (0, 0)),
            pl.BlockSpec((D_MODEL, TN_IN), lambda i, j: (0, j)),
            pl.BlockSpec((D_MODEL, SMALL_W), lambda i, j: (0, 0)),
        ],
        out_specs=[
            pl.BlockSpec((TM_IN, TN_IN), lambda i, j: (i, jnp.minimum(j, n_pool - 1))),
            pl.BlockSpec((TM_IN, TN_IN), lambda i, j: (i, jnp.maximum(j - n_pool, 0))),
            pl.BlockSpec((TM_IN, SMALL_W), lambda i, j: (i, 0)),
        ],
        out_shape=[
            jax.ShapeDtypeStruct((m, POOL_WIDTH), F32),
            jax.ShapeDtypeStruct((m, n_main - POOL_WIDTH), BF16),
            jax.ShapeDtypeStruct((m, SMALL_W), F32),
        ],
        scratch_shapes=[pltpu.VMEM((TM_IN, D_MODEL), BF16)],
        compiler_params=pltpu.CompilerParams(
            dimension_semantics=("arbitrary", "arbitrary"), vmem_limit_bytes=VMEM_LIMIT),
        name="inproj",
    )(x2d, g, w_main, w_small)


def _dsa_kernel(rel_ref, qT_ref, qiT_ref, wT_ref, kidx_ref, k_ref, vT_ref, o_ref, keys_s, bias_s,
                mx_s, l_s, alpha_s, acc_s, s_s, *, topk):
    b = pl.program_id(0)
    i = pl.program_id(1)
    n_ch = i + 1

    @pl.when((b == 0) & (i == 0))
    def _():
        r = lax.broadcasted_iota(I32, (KC, QB), 0)
        c = lax.broadcasted_iota(I32, (KC, QB), 1)
        for off in range(2):
            dist = jnp.maximum(c - r + off * QB, 0)

            def head_body(h, carry, dist=dist, off=off):
                t = jnp.full((KC, QB), rel_ref[0, h], F32)
                for bk in range(1, N_BUCKETS):
                    t = jnp.where(dist >= BUCKET_LO[bk], rel_ref[bk, h], t)
                bias_s[off, h] = t
                return carry

            lax.fori_loop(0, ATTN_HEADS, head_body, 0)

    w_scaled = wT_ref[...] * ((IDX_HEADS ** -0.5) * (IDX_DIM ** -0.5))
    row = lax.broadcasted_iota(I32, (KC, QB), 0)
    col = lax.broadcasted_iota(I32, (KC, QB), 1)

    def idx_body(m, carry):
        kc = kidx_ref[pl.ds(pl.multiple_of(m * KC, KC), KC), :]
        acc = jnp.zeros((KC, QB), F32)
        for h in range(IDX_HEADS):
            d = jnp.dot(kc, qiT_ref[h * IDX_DIM:(h + 1) * IDX_DIM, :],
                        preferred_element_type=F32)
            acc = acc + w_scaled[h:h + 1, :] * jnp.maximum(d, 0.0)
        bits = pltpu.bitcast(acc, I32)
        key = jnp.where(bits >= 0, bits, bits ^ 0x7FFFFFFF)
        causal = (m * KC + row) <= (i * QB + col)
        keys_s[m] = jnp.where(causal, key, INT_MIN)
        return carry

    lax.fori_loop(0, n_ch, idx_body, 0)

    def bit_body(it, prefix):
        cand = prefix + jnp.left_shift(jnp.int32(1), 31 - it)

        def cnt_body(m, cnt):
            ge = (keys_s[m] >= cand).astype(I32)
            return cnt + jnp.sum(ge.reshape(KC // SUBLANES, SUBLANES, QB), axis=0)

        cnt8 = lax.fori_loop(0, n_ch, cnt_body, jnp.zeros((SUBLANES, QB), I32))
        cnt = jnp.sum(cnt8, axis=0, keepdims=True)
        return jnp.where(cnt >= topk, cand, prefix)

    prefix = lax.fori_loop(0, 32, bit_body, jnp.full((1, QB), INT_MIN, I32))
    thr = jnp.maximum(prefix, INT_MIN + 1)

    def mask_body(m, carry):
        madd = jnp.where(keys_s[m] >= thr, 0.0, NEG_BIG).astype(F32)
        keys_s[m] = pltpu.bitcast(madd, I32)
        return carry

    lax.fori_loop(0, n_ch, mask_body, 0)

    scale = HEAD_DIM ** -0.5
    n_far = jnp.maximum(i - 1, 0)
    mx_s[...] = jnp.full(mx_s.shape, -jnp.inf, F32)
    l_s[...] = jnp.zeros(l_s.shape, F32)
    acc_s[...] = jnp.zeros(acc_s.shape, F32)

    def att_body(m, carry, near):
        madd = pltpu.bitcast(keys_s[m], F32)
        rows = pl.ds(pl.multiple_of(m * KC, KC), KC)
        for h in range(ATTN_HEADS):
            hd = slice(h * HEAD_DIM, (h + 1) * HEAD_DIM)
            s = jnp.dot(k_ref[rows, hd], qT_ref[hd, :], preferred_element_type=F32)
            if near:
                s = s * scale + (madd + bias_s[i - m, h])
            else:
                s = s * scale + (madd + rel_ref[N_BUCKETS - 1, h])
            s_s[h] = s
            mx = mx_s[h]
            m_new = jnp.maximum(mx, jnp.max(s, axis=0, keepdims=True))
            alpha_s[h] = jnp.exp(mx - m_new)
            mx_s[h] = m_new
        for h in range(ATTN_HEADS):
            hd = slice(h * HEAD_DIM, (h + 1) * HEAD_DIM)
            p = jnp.exp(s_s[h] - mx_s[h])
            alpha = alpha_s[h]
            l_s[h] = alpha * l_s[h] + jnp.sum(p, axis=0, keepdims=True)
            pv = jnp.dot(vT_ref[m, hd, :], p.astype(BF16), preferred_element_type=F32)
            acc_s[h] = alpha * acc_s[h] + pv
        return carry

    lax.fori_loop(0, n_far, functools.partial(att_body, near=False), 0)
    lax.fori_loop(n_far, n_ch, functools.partial(att_body, near=True), 0)
    for h in range(ATTN_HEADS):
        out_t = acc_s[h] / l_s[h]
        o_ref[:, h * HEAD_DIM:(h + 1) * HEAD_DIM] = out_t.T.astype(o_ref.dtype)


def _dsa(rel_bias, qT, qiT, wT, kidx, big, vT4, seq):
    bsz = qT.shape[0]
    topk = min(TOPK_MAX, seq // 4)
    n_q = seq // QB
    kernel = functools.partial(_dsa_kernel, topk=topk)
    k_col_block = 1
    return pl.pallas_call(
        kernel,
        grid=(bsz, n_q),
        in_specs=[
            pl.BlockSpec(memory_space=pltpu.SMEM),
            pl.BlockSpec((None, ATTN_WIDTH, QB), lambda b, i: (b, 0, i)),
            pl.BlockSpec((None, IDX_HEADS * IDX_DIM, QB), lambda b, i: (b, 0, i)),
            pl.BlockSpec((None, IDX_HEADS, QB), lambda b, i: (b, 0, i)),
            pl.BlockSpec((None, seq, IDX_DIM), lambda b, i: (b, 0, 0)),
            pl.BlockSpec((None, seq, ATTN_WIDTH), lambda b, i: (b, 0, k_col_block)),
            pl.BlockSpec((None, seq // KC, ATTN_WIDTH, KC), lambda b, i: (b, 0, 0, 0)),
        ],
        out_specs=pl.BlockSpec((None, QB, ATTN_WIDTH), lambda b, i: (b, i, 0)),
        out_shape=jax.ShapeDtypeStruct((bsz, seq, ATTN_WIDTH), BF16),
        scratch_shapes=[
            pltpu.VMEM((seq // KC, KC, QB), I32),
            pltpu.VMEM((2, ATTN_HEADS, KC, QB), F32),
            pltpu.VMEM((ATTN_HEADS, 1, QB), F32),
            pltpu.VMEM((ATTN_HEADS, 1, QB), F32),
            pltpu.VMEM((ATTN_HEADS, 1, QB), F32),
            pltpu.VMEM((ATTN_HEADS, HEAD_DIM, QB), F32),
            pltpu.VMEM((ATTN_HEADS, KC, QB), F32),
        ],
        compiler_params=pltpu.CompilerParams(
            dimension_semantics=("arbitrary", "arbitrary"), vmem_limit_bytes=VMEM_LIMIT),
        name="dsa",
    )(rel_bias, qT, qiT, wT, kidx, big, vT4)


def _mix_kernel(x_ref, u_ref, halo_ref, attn_ref, wpool_ref, pscale_ref, wout_ref, g_ref,
                x1_ref, h2_ref, ext_s, *, seq):
    m = pl.program_id(0)
    tiles_per_seq = seq // TM_MIX
    first = (m % tiles_per_seq) == 0
    ext_s[0:POOL_HALO, :] = jnp.where(first, 0.0, halo_ref[...])
    ext_s[POOL_HALO:, :] = u_ref[...]

    t_pos = (m % tiles_per_seq) * TM_MIX + lax.broadcasted_iota(I32, (TM_MIX, POOL_GC), 0)
    acc = x_ref[...] + jnp.dot(attn_ref[...], wout_ref[POOL_WIDTH:, :], preferred_element_type=F32)
    for g, w in enumerate(POOL_WINDOWS):
        cols = slice(g * POOL_GC, (g + 1) * POOL_GC)
        u = ext_s[POOL_HALO:, cols]
        wsum = u
        for j in range(1, w):
            wsum = wsum + ext_s[pl.ds(POOL_HALO - j, TM_MIX), cols]
        count = jnp.minimum(t_pos + 1, w).astype(F32)
        pooled = (wsum / count - u).astype(BF16)
        y = jnp.dot(pooled, wpool_ref[g], preferred_element_type=F32) * pscale_ref[:, cols]
        acc = acc + jnp.dot(y.astype(BF16), wout_ref[cols, :], preferred_element_type=F32)
    x1_ref[...] = acc
    h2_ref[...] = _rms(acc, g_ref[...]).astype(BF16)


def _mix(x2d, u_pool, attn2d, w_pool, pool_scale, w_out, g_ffn, seq):
    m = x2d.shape[0]
    halo_per_tile = TM_MIX // POOL_HALO
    kernel = functools.partial(_mix_kernel, seq=seq)
    return pl.pallas_call(
        kernel,
        grid=(m // TM_MIX,),
        in_specs=[
            pl.BlockSpec((TM_MIX, D_MODEL), lambda i: (i, 0)),
            pl.BlockSpec((TM_MIX, POOL_WIDTH), lambda i: (i, 0)),
            pl.BlockSpec((POOL_HALO, POOL_WIDTH), lambda i: (jnp.maximum(i * halo_per_tile - 1, 0), 0)),
            pl.BlockSpec((TM_MIX, ATTN_WIDTH), lambda i: (i, 0)),
            pl.BlockSpec((POOL_GROUPS, POOL_GC, POOL_GC), lambda i: (0, 0, 0)),
            pl.BlockSpec((1, POOL_WIDTH), lambda i: (0, 0)),
            pl.BlockSpec((D_MODEL, D_MODEL), lambda i: (0, 0)),
            pl.BlockSpec((1, D_MODEL), lambda i: (0, 0)),
        ],
        out_specs=[
            pl.BlockSpec((TM_MIX, D_MODEL), lambda i: (i, 0)),
            pl.BlockSpec((TM_MIX, D_MODEL), lambda i: (i, 0)),
        ],
        out_shape=[
            jax.ShapeDtypeStruct((m, D_MODEL), F32),
            jax.ShapeDtypeStruct((m, D_MODEL), BF16),
        ],
        scratch_shapes=[pltpu.VMEM((TM_MIX + POOL_HALO, POOL_WIDTH), F32)],
        compiler_params=pltpu.CompilerParams(
            dimension_semantics=("arbitrary",), vmem_limit_bytes=VMEM_LIMIT),
        name="mix",
    )(x2d, u_pool, u_pool, attn2d, w_pool, pool_scale, w_out, g_ffn)


def _ffn_kernel(h_ref, x1_ref, wg_ref, wv_ref, cwg_ref, cwv_ref, cbg_ref, cbv_ref, wd_ref,
                o_ref, acc_s, u_s, carry_s, *, seq):
    m = pl.program_id(0)
    f = pl.program_id(1)
    n_f = pl.num_programs(1)
    tiles_per_seq = seq // TM_FFN
    first = (m % tiles_per_seq) == 0
    h = h_ref[...]

    u_s[0:SUBLANES, :] = jnp.where(first, 0.0, carry_s[f])
    u_s[SUBLANES:, 0:TF_FFN] = jnp.dot(h, wg_ref[...], preferred_element_type=F32)
    u_s[SUBLANES:, TF_FFN:] = jnp.dot(h, wv_ref[...], preferred_element_type=F32)
    carry_s[f] = u_s[TM_FFN:, :]

    def conv(cols, cw_ref, cb_ref):
        out = cb_ref[...]
        for j in range(CONV_WIDTH):
            shift = CONV_WIDTH - 1 - j
            out = out + u_s[pl.ds(SUBLANES - shift, TM_FFN), cols] * cw_ref[j:j + 1, :]
        return out

    gate = conv(slice(0, TF_FFN), cwg_ref, cbg_ref)
    val = conv(slice(TF_FFN, 2 * TF_FFN), cwv_ref, cbv_ref)
    act = (gate * jax.nn.sigmoid(gate) * val).astype(BF16)
    contrib = jnp.dot(act, wd_ref[...], preferred_element_type=F32)

    @pl.when(f == 0)
    def _():
        acc_s[...] = x1_ref[...] + contrib

    @pl.when(f > 0)
    def _():
        acc_s[...] += contrib

    @pl.when(f == n_f - 1)
    def _():
        o_ref[...] = acc_s[...]


def _ffn(h2, x1, w_up, conv_w, conv_b, w_down, seq):
    m = h2.shape[0]
    n_f = D_FF // TF_FFN
    kernel = functools.partial(_ffn_kernel, seq=seq)
    return pl.pallas_call(
        kernel,
        grid=(m // TM_FFN, n_f),
        in_specs=[
            pl.BlockSpec((TM_FFN, D_MODEL), lambda i, f: (i, 0)),
            pl.BlockSpec((TM_FFN, D_MODEL), lambda i, f: (i, 0)),
            pl.BlockSpec((D_MODEL, TF_FFN), lambda i, f: (0, f)),
            pl.BlockSpec((D_MODEL, TF_FFN), lambda i, f: (0, f + n_f)),
            pl.BlockSpec((CONV_WIDTH, TF_FFN), lambda i, f: (0, f)),
            pl.BlockSpec((CONV_WIDTH, TF_FFN), lambda i, f: (0, f + n_f)),
            pl.BlockSpec((1, TF_FFN), lambda i, f: (0, f)),
            pl.BlockSpec((1, TF_FFN), lambda i, f: (0, f + n_f)),
            pl.BlockSpec((TF_FFN, D_MODEL), lambda i, f: (f, 0)),
        ],
        out_specs=pl.BlockSpec((TM_FFN, D_MODEL), lambda i, f: (i, 0)),
        out_shape=jax.ShapeDtypeStruct((m, D_MODEL), F32),
        scratch_shapes=[
            pltpu.VMEM((TM_FFN, D_MODEL), F32),
            pltpu.VMEM((TM_FFN + SUBLANES, 2 * TF_FFN), F32),
            pltpu.VMEM((n_f, SUBLANES, 2 * TF_FFN), F32),
        ],
        compiler_params=pltpu.CompilerParams(
            dimension_semantics=("arbitrary", "arbitrary"), vmem_limit_bytes=VMEM_LIMIT),
        name="ffn",
    )(h2, x1, w_up, w_up, conv_w, conv_w, conv_b, conv_b, w_down)


def _ple_kernel(x_ref, p_ref, gp_ref, wgate_ref, wproj_ref, gf_ref, o_ref, *, last_layer):
    x = x_ref[...]
    hg = _rms(x, gp_ref[...]).astype(BF16)
    gate = jax.nn.sigmoid(jnp.dot(hg, wgate_ref[...], preferred_element_type=F32))
    emb = jnp.dot(p_ref[...].astype(BF16), wproj_ref[...], preferred_element_type=F32)
    x3 = x + emb * gate
    o_ref[...] = _rms(x3, gf_ref[...]) if last_layer else x3


def _ple(x2, p2d, g_ple, w_gate, w_proj, g_final, last_layer):
    m = x2.shape[0]
    return pl.pallas_call(
        functools.partial(_ple_kernel, last_layer=last_layer),
        grid=(m // TM_PLE,),
        in_specs=[
            pl.BlockSpec((TM_PLE, D_MODEL), lambda i: (i, 0)),
            pl.BlockSpec((TM_PLE, D_PLE), lambda i: (i, 0)),
            pl.BlockSpec((1, D_MODEL), lambda i: (0, 0)),
            pl.BlockSpec((D_MODEL, D_MODEL), lambda i: (0, 0)),
            pl.BlockSpec((D_PLE, D_MODEL), lambda i: (0, 0)),
            pl.BlockSpec((1, D_MODEL), lambda i: (0, 0)),
        ],
        out_specs=pl.BlockSpec((TM_PLE, D_MODEL), lambda i: (i, 0)),
        out_shape=jax.ShapeDtypeStruct((m, D_MODEL), F32),
        compiler_params=pltpu.CompilerParams(
            dimension_semantics=("arbitrary",), vmem_limit_bytes=VMEM_LIMIT),
        name="ple",
    )(x2, p2d, g_ple, w_gate, w_proj, g_final)


def kernel(x, p, g_mix, w_in, w_pool, pool_scale, rel_bias, w_out, g_ffn, w_up, conv_w, conv_b,
           w_down, g_ple, w_ple_gate, w_ple_proj, g_final):
    bsz, seq, _ = x.shape
    depth = w_in.shape[0]
    assert seq % QB == 0 and seq % TM_MIX == 0 and seq % TM_FFN == 0 and (bsz * seq) % TM_IN == 0
    n_main = POOL_WIDTH + 3 * ATTN_WIDTH + IDX_HEADS * IDX_DIM
    x2d = x.reshape(bsz * seq, D_MODEL)
    for i in range(depth):
        wi = w_in[i].astype(BF16)
        w_main = wi[:, :n_main]
        w_small = jnp.pad(wi[:, n_main:], ((0, 0), (0, SMALL_W - (IDX_DIM + IDX_HEADS))))
        u_pool, big, small = _inproj(x2d, g_mix[i].reshape(1, D_MODEL), w_main, w_small)

        big3 = big.reshape(bsz, seq, 4 * ATTN_WIDTH)
        qT = jnp.swapaxes(big3[:, :, 0:ATTN_WIDTH], 1, 2)
        vT4 = jnp.swapaxes(big3[:, :, 2 * ATTN_WIDTH:3 * ATTN_WIDTH].reshape(bsz, seq // KC, KC, ATTN_WIDTH), 2, 3)
        qiT = jnp.swapaxes(big3[:, :, 3 * ATTN_WIDTH:], 1, 2)
        small3 = small.reshape(bsz, seq, SMALL_W)
        kidx = small3[:, :, :IDX_DIM].astype(BF16)
        wT = jnp.swapaxes(small3[:, :, IDX_DIM:IDX_DIM + IDX_HEADS], 1, 2)
        attn = _dsa(rel_bias, qT, qiT, wT, kidx, big3, vT4, seq)

        x1, h2 = _mix(x2d, u_pool, attn.reshape(bsz * seq, ATTN_WIDTH), w_pool[i].astype(BF16),
                      pool_scale[i].reshape(1, POOL_WIDTH), w_out[i].astype(BF16),
                      g_ffn[i].reshape(1, D_MODEL), seq)
        x2 = _ffn(h2, x1, w_up[i].astype(BF16), conv_w[i], conv_b[i].reshape(1, 2 * D_FF),
                  w_down[i].astype(BF16), seq)
        x2d = _ple(x2, p[i].reshape(bsz * seq, D_PLE), g_ple[i].reshape(1, D_MODEL),
                   w_ple_gate[i].astype(BF16), w_ple_proj[i].astype(BF16),
                   g_final.reshape(1, D_MODEL), last_layer=(i == depth - 1))
    return x2d.reshape(bsz, seq, D_MODEL)
```

```python
import functools
import math

import numpy as np
import jax
import jax.numpy as jnp
from jax import lax
from jax.experimental import pallas as pl
from jax.experimental.pallas import tpu as pltpu

F32 = jnp.float32
BF16 = jnp.bfloat16
I32 = jnp.int32
I16 = jnp.int16

D_MODEL = 2048
D_PLE = 256
POOL_WIDTH = 1024
POOL_GROUPS = 4
POOL_GC = 256
POOL_WINDOWS = (2, 4, 8, 16)
ATTN_HEADS = 8
HEAD_DIM = 128
ATTN_WIDTH = 1024
IDX_HEADS = 16
IDX_DIM = 64
TOPK_MAX = 256
N_BUCKETS = 32
MAX_DISTANCE = 128
D_FF = 5632
CONV_WIDTH = 3
EPS = 1e-6

V7X_VMEM_BYTES = 64 * 1024 * 1024
SUBLANES = 8
PACKED_SUBLANES = 16
LANES = 128
VMEM_LIMIT = V7X_VMEM_BYTES - 8 * 1024 * 1024

TM_IN = 1024
TN_IN = 512
SMALL_W = LANES
QB = 256
KC = 256
TM_MIX = 512
POOL_HALO = 16
TM_FFN = 512
TF_FFN = 512
TF_HALF = 256
TM_PLE = 512

INT_MIN = -2 ** 31
I16_MIN = -2 ** 15
I16_MAX = 2 ** 15 - 1
NEG_BIG = -1e30


def _bucket_lower_bounds():
    n = np.arange(0, 4 * MAX_DISTANCE, dtype=np.int64)
    max_exact = N_BUCKETS // 2
    nf = np.maximum(n, 1).astype(np.float32)
    large = max_exact + (np.log(nf / np.float32(max_exact)) / np.float32(math.log(MAX_DISTANCE / max_exact))
                         * np.float32(N_BUCKETS - max_exact)).astype(np.int32)
    large = np.minimum(large, N_BUCKETS - 1)
    bucket = np.where(n < max_exact, n, large)
    assert np.all(np.diff(bucket) >= 0)
    lows = [int(np.argmax(bucket == b)) for b in range(N_BUCKETS)]
    assert all(bucket[lo] == b for b, lo in enumerate(lows))
    return lows


BUCKET_LO = _bucket_lower_bounds()


def _rms(x, g):
    ms = jnp.mean(x * x, axis=-1, keepdims=True)
    return (x * lax.rsqrt(ms + EPS)) * g


def _inproj_kernel(x_ref, g_ref, w_ref, ws_ref, pool_ref, big_ref, small_ref, h_s):
    n = pl.program_id(1)

    @pl.when(n == 0)
    def _():
        hb = _rms(x_ref[...], g_ref[...]).astype(BF16)
        h_s[...] = hb
        small_ref[...] = jnp.dot(hb, ws_ref[...], preferred_element_type=F32)

    acc = jnp.dot(h_s[...], w_ref[...], preferred_element_type=F32)
    n_pool = POOL_WIDTH // TN_IN

    @pl.when(n < n_pool)
    def _():
        pool_ref[...] = acc

    @pl.when(n >= n_pool)
    def _():
        big_ref[...] = acc.astype(BF16)


def _inproj(x2d, g, w_main, w_small):
    m = x2d.shape[0]
    n_main = w_main.shape[1]
    n_pool = POOL_WIDTH // TN_IN
    grid = (m // TM_IN, n_main // TN_IN)
    return pl.pallas_call(
        _inproj_kernel,
        grid=grid,
        in_specs=[
            pl.BlockSpec((TM_IN, D_MODEL), lambda i, j: (i, 0)),
            pl.BlockSpec((1, D_MODEL), lambda i, j: (0, 0)),
            pl.BlockSpec((D_MODEL, TN_IN), lambda i, j: (0, j)),
            pl.BlockSpec((D_MODEL, SMALL_W), lambda i, j: (0, 0)),
        ],
        out_specs=[
            pl.BlockSpec((TM_IN, TN_IN), lambda i, j: (i, jnp.minimum(j, n_pool - 1))),
            pl.BlockSpec((TM_IN, TN_IN), lambda i, j: (i, jnp.maximum(j - n_pool, 0))),
            pl.BlockSpec((TM_IN, SMALL_W), lambda i, j: (i, 0)),
        ],
        out_shape=[
            jax.ShapeDtypeStruct((m, POOL_WIDTH), F32),
            jax.ShapeDtypeStruct((m, n_main - POOL_WIDTH), BF16),
            jax.ShapeDtypeStruct((m, SMALL_W), F32),
        ],
        scratch_shapes=[pltpu.VMEM((TM_IN, D_MODEL), BF16)],
        compiler_params=pltpu.CompilerParams(
            dimension_semantics=("arbitrary", "arbitrary"), vmem_limit_bytes=VMEM_LIMIT),
        name="inproj",
    )(x2d, g, w_main, w_small)


def _dsa_kernel(rel_ref, qT_ref, qiT_ref, wT_ref, kidx_ref, k_ref, vT_ref, o_ref, hi_s, lo_s, madd_s,
                bias_s, mx_s, l_s, alpha_s, acc_s, s_s, *, topk):
    b = pl.program_id(0)
    i = pl.program_id(1)
    n_ch = i + 1

    @pl.when((b == 0) & (i == 0))
    def _():
        r = lax.broadcasted_iota(I32, (KC, QB), 0)
        c = lax.broadcasted_iota(I32, (KC, QB), 1)
        for off in range(2):
            dist = jnp.maximum(c - r + off * QB, 0)

            def head_body(h, carry, dist=dist, off=off):
                t = jnp.full((KC, QB), rel_ref[0, h], F32)
                for bk in range(1, N_BUCKETS):
                    t = jnp.where(dist >= BUCKET_LO[bk], rel_ref[bk, h], t)
                bias_s[off, h] = t - rel_ref[N_BUCKETS - 1, h]
                return carry

            lax.fori_loop(0, ATTN_HEADS, head_body, 0)

    w_scaled = wT_ref[...] * ((IDX_HEADS ** -0.5) * (IDX_DIM ** -0.5))
    row = lax.broadcasted_iota(I32, (KC, QB), 0)
    col = lax.broadcasted_iota(I32, (KC, QB), 1)

    def idx_body(m, carry):
        kc = kidx_ref[pl.ds(pl.multiple_of(m * KC, KC), KC), :]
        acc = jnp.zeros((KC, QB), F32)
        for h in range(IDX_HEADS):
            d = jnp.dot(kc, qiT_ref[h * IDX_DIM:(h + 1) * IDX_DIM, :],
                        preferred_element_type=F32)
            acc = acc + w_scaled[h:h + 1, :] * jnp.maximum(d, 0.0)
        bits = pltpu.bitcast(acc, I32)
        key = jnp.where(bits >= 0, bits, bits ^ 0x7FFFFFFF)
        causal = (m * KC + row) <= (i * QB + col)
        key = jnp.where(causal, key, INT_MIN)
        hi_s[m] = (key >> 16).astype(I16)
        lo_s[m] = ((key & 0xFFFF) + I16_MIN).astype(I16)
        return carry

    lax.fori_loop(0, n_ch, idx_body, 0)

    def count_ge(ref, cand):
        cand16 = cand.astype(I16)

        def cnt_body(m, accs):
            ge = (ref[m] >= cand16).astype(I16)
            accs = list(accs)
            for j in range(KC // PACKED_SUBLANES):
                a = j % len(accs)
                accs[a] = accs[a] + ge[j * PACKED_SUBLANES:(j + 1) * PACKED_SUBLANES, :]
            return tuple(accs)

        zero = jnp.zeros((PACKED_SUBLANES, QB), I16)
        accs = lax.fori_loop(0, n_ch, cnt_body, (zero,) * 4)
        tot = (accs[0] + accs[1]) + (accs[2] + accs[3])
        return jnp.sum(tot.astype(I32), axis=0, keepdims=True)

    def descend(ref, want):
        def bit_body(it, prefix):
            cand = prefix + jnp.left_shift(jnp.int32(1), 15 - it)
            return jnp.where(count_ge(ref, cand) >= want, cand, prefix)

        return lax.fori_loop(0, 16, bit_body, jnp.full((1, QB), I16_MIN, I32))

    p_hi = descend(hi_s, topk)
    n_above = jnp.where(p_hi >= I16_MAX, 0, count_ge(hi_s, jnp.minimum(p_hi + 1, I16_MAX)))
    hi_eq = jnp.where(p_hi > I16_MIN, p_hi, I16_MAX).astype(I16)

    def member_body(m, carry):
        lo_s[m] = jnp.where(hi_s[m] == hi_eq, lo_s[m], I16_MIN)
        return carry

    lax.fori_loop(0, n_ch, member_body, 0)
    p_lo = descend(lo_s, topk - n_above)
    p_hi16 = p_hi.astype(I16)
    p_lo16 = p_lo.astype(I16)

    def mask_body(m, carry):
        hi = hi_s[m]
        keep = (hi > p_hi16) | ((hi == hi_eq) & (lo_s[m] >= p_lo16))
        madd_s[m] = jnp.where(keep, jnp.bfloat16(0.0), jnp.bfloat16(NEG_BIG)).astype(F32)
        return carry

    lax.fori_loop(0, n_ch, mask_body, 0)

    scale = HEAD_DIM ** -0.5
    n_far = jnp.maximum(i - 1, 0)
    mx_s[...] = jnp.full(mx_s.shape, -jnp.inf, F32)
    l_s[...] = jnp.zeros(l_s.shape, F32)
    acc_s[...] = jnp.zeros(acc_s.shape, F32)

    def att_body(m, carry, near):
        madd = madd_s[m]
        rows = pl.ds(pl.multiple_of(m * KC, KC), KC)
        for h in range(ATTN_HEADS):
            hd = slice(h * HEAD_DIM, (h + 1) * HEAD_DIM)
            s = jnp.dot(k_ref[rows, hd], qT_ref[hd, :], preferred_element_type=F32)
            if near:
                s = s * scale + (madd + bias_s[i - m, h])
            else:
                s = s * scale + madd
            s_s[h] = s
            mx = mx_s[h]
            m_new = jnp.maximum(mx, jnp.max(s, axis=0, keepdims=True))
            alpha_s[h] = jnp.exp(mx - m_new)
            mx_s[h] = m_new
        for h in range(ATTN_HEADS):
            hd = slice(h * HEAD_DIM, (h + 1) * HEAD_DIM)
            p = jnp.exp(s_s[h] - mx_s[h])
            alpha = alpha_s[h]
            l_s[h] = alpha * l_s[h] + jnp.sum(p, axis=0, keepdims=True)
            pv = jnp.dot(vT_ref[m, hd, :], p.astype(BF16), preferred_element_type=F32)
            acc_s[h] = alpha * acc_s[h] + pv
        return carry

    lax.fori_loop(0, n_far, functools.partial(att_body, near=False), 0)
    lax.fori_loop(n_far, n_ch, functools.partial(att_body, near=True), 0)
    for h in range(ATTN_HEADS):
        out_t = acc_s[h] / l_s[h]
        o_ref[:, h * HEAD_DIM:(h + 1) * HEAD_DIM] = out_t.T.astype(o_ref.dtype)


def _dsa(rel_bias, qT, qiT, wT, kidx, big, vT4, seq):
    bsz = qT.shape[0]
    topk = min(TOPK_MAX, seq // 4)
    n_q = seq // QB
    kernel = functools.partial(_dsa_kernel, topk=topk)
    k_col_block = 1
    return pl.pallas_call(
        kernel,
        grid=(bsz, n_q),
        in_specs=[
            pl.BlockSpec(memory_space=pltpu.SMEM),
            pl.BlockSpec((None, ATTN_WIDTH, QB), lambda b, i: (b, 0, i)),
            pl.BlockSpec((None, IDX_HEADS * IDX_DIM, QB), lambda b, i: (b, 0, i)),
            pl.BlockSpec((None, IDX_HEADS, QB), lambda b, i: (b, 0, i)),
            pl.BlockSpec((None, seq, IDX_DIM), lambda b, i: (b, 0, 0)),
            pl.BlockSpec((None, seq, ATTN_WIDTH), lambda b, i: (b, 0, k_col_block)),
            pl.BlockSpec((None, seq // KC, ATTN_WIDTH, KC), lambda b, i: (b, 0, 0, 0)),
        ],
        out_specs=pl.BlockSpec((None, QB, ATTN_WIDTH), lambda b, i: (b, i, 0)),
        out_shape=jax.ShapeDtypeStruct((bsz, seq, ATTN_WIDTH), BF16),
        scratch_shapes=[
            pltpu.VMEM((seq // KC, KC, QB), I16),
            pltpu.VMEM((seq // KC, KC, QB), I16),
            pltpu.VMEM((seq // KC, KC, QB), F32),
            pltpu.VMEM((2, ATTN_HEADS, KC, QB), F32),
            pltpu.VMEM((ATTN_HEADS, 1, QB), F32),
            pltpu.VMEM((ATTN_HEADS, 1, QB), F32),
            pltpu.VMEM((ATTN_HEADS, 1, QB), F32),
            pltpu.VMEM((ATTN_HEADS, HEAD_DIM, QB), F32),
            pltpu.VMEM((ATTN_HEADS, KC, QB), F32),
        ],
        compiler_params=pltpu.CompilerParams(
            dimension_semantics=("arbitrary", "arbitrary"), vmem_limit_bytes=VMEM_LIMIT),
        name="dsa",
    )(rel_bias, qT, qiT, wT, kidx, big, vT4)


def _mix_kernel(x_ref, u_ref, halo_ref, attn_ref, wpool_ref, pscale_ref, wout_ref, g_ref,
                x1_ref, h2_ref, ext_s, *, seq):
    m = pl.program_id(0)
    tiles_per_seq = seq // TM_MIX
    first = (m % tiles_per_seq) == 0
    ext_s[0:POOL_HALO, :] = jnp.where(first, 0.0, halo_ref[...])
    ext_s[POOL_HALO:, :] = u_ref[...]

    t_pos = (m % tiles_per_seq) * TM_MIX + lax.broadcasted_iota(I32, (TM_MIX, POOL_GC), 0)
    acc = x_ref[...] + jnp.dot(attn_ref[...], wout_ref[POOL_WIDTH:, :], preferred_element_type=F32)
    for g, w in enumerate(POOL_WINDOWS):
        cols = slice(g * POOL_GC, (g + 1) * POOL_GC)
        u = ext_s[POOL_HALO:, cols]
        wsum = u
        for j in range(1, w):
            wsum = wsum + ext_s[pl.ds(POOL_HALO - j, TM_MIX), cols]
        count = jnp.minimum(t_pos + 1, w).astype(F32)
        pooled = (wsum / count - u).astype(BF16)
        y = jnp.dot(pooled, wpool_ref[g], preferred_element_type=F32) * pscale_ref[:, cols]
        acc = acc + jnp.dot(y.astype(BF16), wout_ref[cols, :], preferred_element_type=F32)
    x1_ref[...] = acc
    h2_ref[...] = _rms(acc, g_ref[...]).astype(BF16)


def _mix(x2d, u_pool, attn2d, w_pool, pool_scale, w_out, g_ffn, seq):
    m = x2d.shape[0]
    halo_per_tile = TM_MIX // POOL_HALO
    kernel = functools.partial(_mix_kernel, seq=seq)
    return pl.pallas_call(
        kernel,
        grid=(m // TM_MIX,),
        in_specs=[
            pl.BlockSpec((TM_MIX, D_MODEL), lambda i: (i, 0)),
            pl.BlockSpec((TM_MIX, POOL_WIDTH), lambda i: (i, 0)),
            pl.BlockSpec((POOL_HALO, POOL_WIDTH), lambda i: (jnp.maximum(i * halo_per_tile - 1, 0), 0)),
            pl.BlockSpec((TM_MIX, ATTN_WIDTH), lambda i: (i, 0)),
            pl.BlockSpec((POOL_GROUPS, POOL_GC, POOL_GC), lambda i: (0, 0, 0)),
            pl.BlockSpec((1, POOL_WIDTH), lambda i: (0, 0)),
            pl.BlockSpec((D_MODEL, D_MODEL), lambda i: (0, 0)),
            pl.BlockSpec((1, D_MODEL), lambda i: (0, 0)),
        ],
        out_specs=[
            pl.BlockSpec((TM_MIX, D_MODEL), lambda i: (i, 0)),
            pl.BlockSpec((TM_MIX, D_MODEL), lambda i: (i, 0)),
        ],
        out_shape=[
            jax.ShapeDtypeStruct((m, D_MODEL), F32),
            jax.ShapeDtypeStruct((m, D_MODEL), BF16),
        ],
        scratch_shapes=[pltpu.VMEM((TM_MIX + POOL_HALO, POOL_WIDTH), F32)],
        compiler_params=pltpu.CompilerParams(
            dimension_semantics=("arbitrary",), vmem_limit_bytes=VMEM_LIMIT),
        name="mix",
    )(x2d, u_pool, u_pool, attn2d, w_pool, pool_scale, w_out, g_ffn)


def _ffn_kernel(h_ref, x1_ref, wg_ref, wv_ref, cwg_ref, cwv_ref, cbg_ref, cbv_ref, wd_ref,
                o_ref, carry_s, *u_bufs, seq):
    m = pl.program_id(0)
    f = pl.program_id(1)
    tiles_per_seq = seq // TM_FFN
    first = (m % tiles_per_seq) == 0

    @pl.when(f == 0)
    def _():
        o_ref[...] = x1_ref[...]

    h = h_ref[...]
    n_half = TF_FFN // TF_HALF

    def up(s):
        cols = slice(s * TF_HALF, (s + 1) * TF_HALF)
        for k, w_ref in enumerate((wg_ref, wv_ref)):
            u = u_bufs[2 * s + k]
            u[0:SUBLANES, :] = jnp.where(first, 0.0, carry_s[f, 2 * s + k])
            u[SUBLANES:, :] = jnp.dot(h, w_ref[:, cols], preferred_element_type=F32)
            carry_s[f, 2 * s + k] = u[TM_FFN:, :]

    def conv(u, cols, cw_ref, cb_ref):
        out = cb_ref[:, cols]
        for j in range(CONV_WIDTH):
            shift = CONV_WIDTH - 1 - j
            out = out + u[pl.ds(SUBLANES - shift, TM_FFN), :] * cw_ref[j:j + 1, cols]
        return out

    def gated(s):
        cols = slice(s * TF_HALF, (s + 1) * TF_HALF)
        gate = conv(u_bufs[2 * s], cols, cwg_ref, cbg_ref)
        val = conv(u_bufs[2 * s + 1], cols, cwv_ref, cbv_ref)
        return (gate * jax.nn.sigmoid(gate) * val).astype(BF16)

    def down(s, act):
        return jnp.dot(act, wd_ref[s * TF_HALF:(s + 1) * TF_HALF, :], preferred_element_type=F32)

    up(0)
    act0 = gated(0)
    up(1)
    contrib = down(0, act0)
    act1 = gated(1)
    contrib = contrib + down(1, act1)
    o_ref[...] += contrib


def _ffn(h2, x1, w_up, conv_w, conv_b, w_down, seq):
    m = h2.shape[0]
    n_f = D_FF // TF_FFN
    kernel = functools.partial(_ffn_kernel, seq=seq)
    return pl.pallas_call(
        kernel,
        grid=(m // TM_FFN, n_f),
        in_specs=[
            pl.BlockSpec((TM_FFN, D_MODEL), lambda i, f: (i, 0)),
            pl.BlockSpec((TM_FFN, D_MODEL), lambda i, f: (i, 0)),
            pl.BlockSpec((D_MODEL, TF_FFN), lambda i, f: (0, f)),
            pl.BlockSpec((D_MODEL, TF_FFN), lambda i, f: (0, f + n_f)),
            pl.BlockSpec((CONV_WIDTH, TF_FFN), lambda i, f: (0, f)),
            pl.BlockSpec((CONV_WIDTH, TF_FFN), lambda i, f: (0, f + n_f)),
            pl.BlockSpec((1, TF_FFN), lambda i, f: (0, f)),
            pl.BlockSpec((1, TF_FFN), lambda i, f: (0, f + n_f)),
            pl.BlockSpec((TF_FFN, D_MODEL), lambda i, f: (f, 0)),
        ],
        out_specs=pl.BlockSpec((TM_FFN, D_MODEL), lambda i, f: (i, 0)),
        out_shape=jax.ShapeDtypeStruct((m, D_MODEL), F32),
        scratch_shapes=[pltpu.VMEM((n_f, 2 * (TF_FFN // TF_HALF), SUBLANES, TF_HALF), F32)]
        + [pltpu.VMEM((TM_FFN + SUBLANES, TF_HALF), F32)] * (2 * (TF_FFN // TF_HALF)),
        compiler_params=pltpu.CompilerParams(
            dimension_semantics=("arbitrary", "arbitrary"), vmem_limit_bytes=VMEM_LIMIT),
        name="ffn",
    )(h2, x1, w_up, w_up, conv_w, conv_w, conv_b, conv_b, w_down)


def _ple_kernel(x_ref, p_ref, gp_ref, wgate_ref, wproj_ref, gf_ref, o_ref, *, last_layer):
    x = x_ref[...]
    hg = _rms(x, gp_ref[...]).astype(BF16)
    gate = jax.nn.sigmoid(jnp.dot(hg, wgate_ref[...], preferred_element_type=F32))
    emb = jnp.dot(p_ref[...].astype(BF16), wproj_ref[...], preferred_element_type=F32)
    x3 = x + emb * gate
    o_ref[...] = _rms(x3, gf_ref[...]) if last_layer else x3


def _ple(x2, p2d, g_ple, w_gate, w_proj, g_final, last_layer):
    m = x2.shape[0]
    return pl.pallas_call(
        functools.partial(_ple_kernel, last_layer=last_layer),
        grid=(m // TM_PLE,),
        in_specs=[
            pl.BlockSpec((TM_PLE, D_MODEL), lambda i: (i, 0)),
            pl.BlockSpec((TM_PLE, D_PLE), lambda i: (i, 0)),
            pl.BlockSpec((1, D_MODEL), lambda i: (0, 0)),
            pl.BlockSpec((D_MODEL, D_MODEL), lambda i: (0, 0)),
            pl.BlockSpec((D_PLE, D_MODEL), lambda i: (0, 0)),
            pl.BlockSpec((1, D_MODEL), lambda i: (0, 0)),
        ],
        out_specs=pl.BlockSpec((TM_PLE, D_MODEL), lambda i: (i, 0)),
        out_shape=jax.ShapeDtypeStruct((m, D_MODEL), F32),
        compiler_params=pltpu.CompilerParams(
            dimension_semantics=("arbitrary",), vmem_limit_bytes=VMEM_LIMIT),
        name="ple",
    )(x2, p2d, g_ple, w_gate, w_proj, g_final)


def kernel(x, p, g_mix, w_in, w_pool, pool_scale, rel_bias, w_out, g_ffn, w_up, conv_w, conv_b,
           w_down, g_ple, w_ple_gate, w_ple_proj, g_final):
    bsz, seq, _ = x.shape
    depth = w_in.shape[0]
    assert seq % QB == 0 and seq % TM_MIX == 0 and seq % TM_FFN == 0 and (bsz * seq) % TM_IN == 0
    n_main = POOL_WIDTH + 3 * ATTN_WIDTH + IDX_HEADS * IDX_DIM
    x2d = x.reshape(bsz * seq, D_MODEL)
    for i in range(depth):
        wi = w_in[i].astype(BF16)
        w_main = wi[:, :n_main]
        w_small = jnp.pad(wi[:, n_main:], ((0, 0), (0, SMALL_W - (IDX_DIM + IDX_HEADS))))
        u_pool, big, small = _inproj(x2d, g_mix[i].reshape(1, D_MODEL), w_main, w_small)

        big3 = big.reshape(bsz, seq, 4 * ATTN_WIDTH)
        qT = jnp.swapaxes(big3[:, :, 0:ATTN_WIDTH], 1, 2)
        vT4 = jnp.swapaxes(big3[:, :, 2 * ATTN_WIDTH:3 * ATTN_WIDTH].reshape(bsz, seq // KC, KC, ATTN_WIDTH), 2, 3)
        qiT = jnp.swapaxes(big3[:, :, 3 * ATTN_WIDTH:], 1, 2)
        small3 = small.reshape(bsz, seq, SMALL_W)
        kidx = small3[:, :, :IDX_DIM].astype(BF16)
        wT = jnp.swapaxes(small3[:, :, IDX_DIM:IDX_DIM + IDX_HEADS], 1, 2)
        attn = _dsa(rel_bias, qT, qiT, wT, kidx, big3, vT4, seq)

        x1, h2 = _mix(x2d, u_pool, attn.reshape(bsz * seq, ATTN_WIDTH), w_pool[i].astype(BF16),
                      pool_scale[i].reshape(1, POOL_WIDTH), w_out[i].astype(BF16),
                      g_ffn[i].reshape(1, D_MODEL), seq)
        x2 = _ffn(h2, x1, w_up[i].astype(BF16), conv_w[i], conv_b[i].reshape(1, 2 * D_FF),
                  w_down[i].astype(BF16), seq)
        x2d = _ple(x2, p[i].reshape(bsz * seq, D_PLE), g_ple[i].reshape(1, D_MODEL),
                   w_ple_gate[i].astype(BF16), w_ple_proj[i].astype(BF16),
                   g_final.reshape(1, D_MODEL), last_layer=(i == depth - 1))
    return x2d.reshape(bsz, seq, D_MODEL)
```

```python
import functools
import math

import numpy as np
import jax
import jax.numpy as jnp
from jax import lax
from jax.experimental import pallas as pl
from jax.experimental.pallas import tpu as pltpu

F32 = jnp.float32
BF16 = jnp.bfloat16
I32 = jnp.int32
I16 = jnp.int16

D_MODEL = 2048
D_PLE = 256
POOL_WIDTH = 1024
POOL_GROUPS = 4
POOL_GC = 256
POOL_WINDOWS = (2, 4, 8, 16)
ATTN_HEADS = 8
HEAD_DIM = 128
ATTN_WIDTH = 1024
IDX_HEADS = 16
IDX_DIM = 64
TOPK_MAX = 256
N_BUCKETS = 32
MAX_DISTANCE = 128
D_FF = 5632
CONV_WIDTH = 3
EPS = 1e-6
IN_SIZES = (POOL_WIDTH, ATTN_WIDTH, ATTN_WIDTH, ATTN_WIDTH, IDX_HEADS * IDX_DIM, IDX_DIM, IDX_HEADS)

V7X_VMEM_BYTES = 64 * 1024 * 1024
SUBLANES = 8
PACKED_SUBLANES = 16
LANES = 128
VMEM_LIMIT = V7X_VMEM_BYTES - 8 * 1024 * 1024

TM_IN = 1024
TN_IN = 512
IN_TILES_PER_GROUP = 1024 // TN_IN
SMALL_W = LANES
QB = 256
KC = 256
TM_MIX = 512
POOL_HALO = 16
TM_FFN = 512
TF_FFN = 512
TF_HALF = 256
TM_PLE = 512

INT_MIN = -2 ** 31
I16_MIN = -2 ** 15
I16_MAX = 2 ** 15 - 1
NEG_BIG = -1e30
LOG2E = math.log2(math.e)


def _bucket_lower_bounds():
    n = np.arange(0, 4 * MAX_DISTANCE, dtype=np.int64)
    max_exact = N_BUCKETS // 2
    nf = np.maximum(n, 1).astype(np.float32)
    large = max_exact + (np.log(nf / np.float32(max_exact)) / np.float32(math.log(MAX_DISTANCE / max_exact))
                         * np.float32(N_BUCKETS - max_exact)).astype(np.int32)
    large = np.minimum(large, N_BUCKETS - 1)
    bucket = np.where(n < max_exact, n, large)
    assert np.all(np.diff(bucket) >= 0)
    lows = [int(np.argmax(bucket == b)) for b in range(N_BUCKETS)]
    assert all(bucket[lo] == b for b, lo in enumerate(lows))
    return lows


BUCKET_LO = _bucket_lower_bounds()


def _rms(x, g):
    ms = jnp.mean(x * x, axis=-1, keepdims=True)
    return (x * lax.rsqrt(ms + EPS)) * g


def _nt_dot(a, b):
    return lax.dot_general(a, b, (((1,), (1,)), ((), ())), preferred_element_type=F32)


def _inproj_kernel(x_ref, g_ref, wt_ref, wst_ref, pool_ref, k_ref, qT_ref, qiT_ref, vT_ref, small_ref, h_s):
    n = pl.program_id(1)
    group = n // IN_TILES_PER_GROUP

    @pl.when(n == 0)
    def _():
        hb = _rms(x_ref[...], g_ref[...]).astype(BF16)
        h_s[...] = hb
        small_ref[...] = _nt_dot(hb, wst_ref[...])

    @pl.when(group == 0)
    def _():
        pool_ref[...] = _nt_dot(h_s[...], wt_ref[...])

    @pl.when(group == 1)
    def _():
        k_ref[...] = _nt_dot(h_s[...], wt_ref[...]).astype(BF16)

    @pl.when(group == 2)
    def _():
        qT_ref[...] = _nt_dot(wt_ref[...], h_s[...]).astype(BF16)

    @pl.when(group == 3)
    def _():
        qiT_ref[...] = _nt_dot(wt_ref[...], h_s[...]).astype(BF16)

    @pl.when(group == 4)
    def _():
        out_t = _nt_dot(wt_ref[...], h_s[...]).astype(BF16)
        for j in range(TM_IN // KC):
            vT_ref[j] = out_t[:, j * KC:(j + 1) * KC]


def _inproj(x2d, g, wt_main, wt_small, bsz, seq):
    m = x2d.shape[0]
    tiles_per_seq = seq // TM_IN
    n_steps = wt_main.shape[0] // TN_IN

    def half(j, group):
        return jnp.clip(j - group * IN_TILES_PER_GROUP, 0, IN_TILES_PER_GROUP - 1)

    return pl.pallas_call(
        _inproj_kernel,
        grid=(m // TM_IN, n_steps),
        in_specs=[
            pl.BlockSpec((TM_IN, D_MODEL), lambda i, j: (i, 0)),
            pl.BlockSpec((1, D_MODEL), lambda i, j: (0, 0)),
            pl.BlockSpec((TN_IN, D_MODEL), lambda i, j: (j, 0)),
            pl.BlockSpec((SMALL_W, D_MODEL), lambda i, j: (0, 0)),
        ],
        out_specs=[
            pl.BlockSpec((TM_IN, TN_IN), lambda i, j: (i, half(j, 0))),
            pl.BlockSpec((TM_IN, TN_IN), lambda i, j: (i, half(j, 1))),
            pl.BlockSpec((None, TN_IN, TM_IN),
                         lambda i, j: (i // tiles_per_seq, half(j, 2), i % tiles_per_seq)),
            pl.BlockSpec((None, TN_IN, TM_IN),
                         lambda i, j: (i // tiles_per_seq, half(j, 3), i % tiles_per_seq)),
            pl.BlockSpec((None, TM_IN // KC, TN_IN, KC),
                         lambda i, j: (i // tiles_per_seq, i % tiles_per_seq, half(j, 4), 0)),
            pl.BlockSpec((TM_IN, SMALL_W), lambda i, j: (i, 0)),
        ],
        out_shape=[
            jax.ShapeDtypeStruct((m, POOL_WIDTH), F32),
            jax.ShapeDtypeStruct((m, ATTN_WIDTH), BF16),
            jax.ShapeDtypeStruct((bsz, ATTN_WIDTH, seq), BF16),
            jax.ShapeDtypeStruct((bsz, IDX_HEADS * IDX_DIM, seq), BF16),
            jax.ShapeDtypeStruct((bsz, seq // KC, ATTN_WIDTH, KC), BF16),
            jax.ShapeDtypeStruct((m, SMALL_W), F32),
        ],
        scratch_shapes=[pltpu.VMEM((TM_IN, D_MODEL), BF16)],
        compiler_params=pltpu.CompilerParams(
            dimension_semantics=("arbitrary", "arbitrary"), vmem_limit_bytes=VMEM_LIMIT),
        name="inproj",
    )(x2d, g, wt_main, wt_small)


def _dsa_kernel(rel_ref, qT_ref, qiT_ref, wT_ref, kidx_ref, k_ref, vT_ref, o_ref, hi_s, lo_s, madd_s,
                bias_s, mx_s, alpha_s, acc_s, s_s, *, topk):
    b = pl.program_id(0)
    i = pl.program_id(1)
    n_ch = i + 1

    @pl.when((b == 0) & (i == 0))
    def _():
        r = lax.broadcasted_iota(I32, (KC, QB), 0)
        c = lax.broadcasted_iota(I32, (KC, QB), 1)
        for off in range(2):
            dist = jnp.maximum(c - r + off * QB, 0)

            def head_body(h, carry, dist=dist, off=off):
                t = jnp.full((KC, QB), rel_ref[0, h], F32)
                for bk in range(1, N_BUCKETS):
                    t = jnp.where(dist >= BUCKET_LO[bk], rel_ref[bk, h], t)
                bias_s[off, h] = (t - rel_ref[N_BUCKETS - 1, h]) * LOG2E
                return carry

            lax.fori_loop(0, ATTN_HEADS, head_body, 0)

    w_scaled = wT_ref[...] * ((IDX_HEADS ** -0.5) * (IDX_DIM ** -0.5))
    row = lax.broadcasted_iota(I32, (KC, QB), 0)
    col = lax.broadcasted_iota(I32, (KC, QB), 1)

    def idx_body(m, carry):
        kc = kidx_ref[pl.ds(pl.multiple_of(m * KC, KC), KC), :]
        acc = jnp.zeros((KC, QB), F32)
        for h in range(IDX_HEADS):
            d = jnp.dot(kc, qiT_ref[h * IDX_DIM:(h + 1) * IDX_DIM, :],
                        preferred_element_type=F32)
            acc = acc + w_scaled[h:h + 1, :] * jnp.maximum(d, 0.0)
        bits = pltpu.bitcast(acc, I32)
        key = jnp.where(bits >= 0, bits, bits ^ 0x7FFFFFFF)
        causal = (m * KC + row) <= (i * QB + col)
        key = jnp.where(causal, key, INT_MIN)
        hi_s[m] = (key >> 16).astype(I16)
        lo_s[m] = ((key & 0xFFFF) + I16_MIN).astype(I16)
        return carry

    lax.fori_loop(0, n_ch, idx_body, 0)

    def count_ge(ref, cand):
        cand16 = cand.astype(I16)

        def cnt_body(m, accs):
            ge = (ref[m] >= cand16).astype(I16)
            accs = list(accs)
            for j in range(KC // PACKED_SUBLANES):
                a = j % len(accs)
                accs[a] = accs[a] + ge[j * PACKED_SUBLANES:(j + 1) * PACKED_SUBLANES, :]
            return tuple(accs)

        zero = jnp.zeros((PACKED_SUBLANES, QB), I16)
        accs = lax.fori_loop(0, n_ch, cnt_body, (zero,) * 4)
        tot = (accs[0] + accs[1]) + (accs[2] + accs[3])
        return jnp.sum(tot.astype(I32), axis=0, keepdims=True)

    def descend(ref, want):
        def bit_body(it, prefix):
            cand = prefix + jnp.left_shift(jnp.int32(1), 15 - it)
            return jnp.where(count_ge(ref, cand) >= want, cand, prefix)

        return lax.fori_loop(0, 16, bit_body, jnp.full((1, QB), I16_MIN, I32))

    p_hi = descend(hi_s, topk)
    n_above = jnp.where(p_hi >= I16_MAX, 0, count_ge(hi_s, jnp.minimum(p_hi + 1, I16_MAX)))
    hi_eq = jnp.where(p_hi > I16_MIN, p_hi, I16_MAX).astype(I16)

    def member_body(m, carry):
        lo_s[m] = jnp.where(hi_s[m] == hi_eq, lo_s[m], I16_MIN)
        return carry

    lax.fori_loop(0, n_ch, member_body, 0)
    p_lo = descend(lo_s, topk - n_above)
    p_hi16 = p_hi.astype(I16)
    p_lo16 = p_lo.astype(I16)

    def mask_body(m, carry):
        hi = hi_s[m]
        keep = (hi > p_hi16) | ((hi == hi_eq) & (lo_s[m] >= p_lo16))
        madd_s[m] = jnp.where(keep, jnp.bfloat16(0.0), jnp.bfloat16(NEG_BIG)).astype(F32)
        return carry

    lax.fori_loop(0, n_ch, mask_body, 0)

    scale2 = (HEAD_DIM ** -0.5) * LOG2E
    n_far = jnp.maximum(i - 1, 0)
    mx_s[...] = jnp.full(mx_s.shape, -jnp.inf, F32)
    acc_s[...] = jnp.zeros(acc_s.shape, F32)

    def att_body(m, carry, near):
        madd = madd_s[m]
        rows = pl.ds(pl.multiple_of(m * KC, KC), KC)
        for h in range(ATTN_HEADS):
            hd = slice(h * HEAD_DIM, (h + 1) * HEAD_DIM)
            s = jnp.dot(k_ref[rows, hd], qT_ref[hd, :], preferred_element_type=F32)
            if near:
                s = s * scale2 + (madd + bias_s[i - m, h])
            else:
                s = s * scale2 + madd
            s_s[h] = s
            mx = mx_s[h]
            m_new = jnp.maximum(mx, jnp.max(s, axis=0, keepdims=True))
            alpha_s[h] = jnp.exp2(mx - m_new)
            mx_s[h] = m_new
        ones_rows = jnp.ones((PACKED_SUBLANES, KC), BF16)
        for h in range(ATTN_HEADS):
            hd = slice(h * HEAD_DIM, (h + 1) * HEAD_DIM)
            p = jnp.exp2(s_s[h] - mx_s[h])
            v_aug = jnp.concatenate([vT_ref[m, hd, :], ones_rows], axis=0)
            pv = jnp.dot(v_aug, p.astype(BF16), preferred_element_type=F32)
            acc_s[h] = alpha_s[h] * acc_s[h] + pv
        return carry

    lax.fori_loop(0, n_far, functools.partial(att_body, near=False), 0)
    lax.fori_loop(n_far, n_ch, functools.partial(att_body, near=True), 0)
    for h in range(ATTN_HEADS):
        acc = acc_s[h]
        out_t = acc[:HEAD_DIM] / acc[HEAD_DIM:HEAD_DIM + 1]
        o_ref[:, h * HEAD_DIM:(h + 1) * HEAD_DIM] = out_t.T.astype(o_ref.dtype)


def _dsa(rel_bias, qT, qiT, wT, kidx, k, vT4, seq):
    bsz = qT.shape[0]
    topk = min(TOPK_MAX, seq // 4)
    n_q = seq // QB
    kernel = functools.partial(_dsa_kernel, topk=topk)
    return pl.pallas_call(
        kernel,
        grid=(bsz, n_q),
        in_specs=[
            pl.BlockSpec(memory_space=pltpu.SMEM),
            pl.BlockSpec((None, ATTN_WIDTH, QB), lambda b, i: (b, 0, i)),
            pl.BlockSpec((None, IDX_HEADS * IDX_DIM, QB), lambda b, i: (b, 0, i)),
            pl.BlockSpec((None, IDX_HEADS, QB), lambda b, i: (b, 0, i)),
            pl.BlockSpec((None, seq, IDX_DIM), lambda b, i: (b, 0, 0)),
            pl.BlockSpec((None, seq, ATTN_WIDTH), lambda b, i: (b, 0, 0)),
            pl.BlockSpec((None, seq // KC, ATTN_WIDTH, KC), lambda b, i: (b, 0, 0, 0)),
        ],
        out_specs=pl.BlockSpec((None, QB, ATTN_WIDTH), lambda b, i: (b, i, 0)),
        out_shape=jax.ShapeDtypeStruct((bsz, seq, ATTN_WIDTH), BF16),
        scratch_shapes=[
            pltpu.VMEM((seq // KC, KC, QB), I16),
            pltpu.VMEM((seq // KC, KC, QB), I16),
            pltpu.VMEM((seq // KC, KC, QB), F32),
            pltpu.VMEM((2, ATTN_HEADS, KC, QB), F32),
            pltpu.VMEM((ATTN_HEADS, 1, QB), F32),
            pltpu.VMEM((ATTN_HEADS, 1, QB), F32),
            pltpu.VMEM((ATTN_HEADS, HEAD_DIM + PACKED_SUBLANES, QB), F32),
            pltpu.VMEM((ATTN_HEADS, KC, QB), F32),
        ],
        compiler_params=pltpu.CompilerParams(
            dimension_semantics=("arbitrary", "arbitrary"), vmem_limit_bytes=VMEM_LIMIT),
        name="dsa",
    )(rel_bias, qT, qiT, wT, kidx, k, vT4)


def _mix_kernel(x_ref, u_ref, halo_ref, attn_ref, wpool_ref, pscale_ref, wout_ref, g_ref,
                x1_ref, h2_ref, ext_s, *, seq):
    m = pl.program_id(0)
    tiles_per_seq = seq // TM_MIX
    first = (m % tiles_per_seq) == 0
    ext_s[0:POOL_HALO, :] = jnp.where(first, 0.0, halo_ref[...])
    ext_s[POOL_HALO:, :] = u_ref[...]

    t_pos = (m % tiles_per_seq) * TM_MIX + lax.broadcasted_iota(I32, (TM_MIX, POOL_GC), 0)
    acc = x_ref[...] + jnp.dot(attn_ref[...], wout_ref[POOL_WIDTH:, :], preferred_element_type=F32)
    for g, w in enumerate(POOL_WINDOWS):
        cols = slice(g * POOL_GC, (g + 1) * POOL_GC)
        u = ext_s[POOL_HALO:, cols]
        wsum = u
        for j in range(1, w):
            wsum = wsum + ext_s[pl.ds(POOL_HALO - j, TM_MIX), cols]
        count = jnp.minimum(t_pos + 1, w).astype(F32)
        pooled = (wsum / count - u).astype(BF16)
        y = jnp.dot(pooled, wpool_ref[g], preferred_element_type=F32) * pscale_ref[:, cols]
        acc = acc + jnp.dot(y.astype(BF16), wout_ref[cols, :], preferred_element_type=F32)
    x1_ref[...] = acc
    h2_ref[...] = _rms(acc, g_ref[...]).astype(BF16)


def _mix(x2d, u_pool, attn2d, w_pool, pool_scale, w_out, g_ffn, seq):
    m = x2d.shape[0]
    halo_per_tile = TM_MIX // POOL_HALO
    kernel = functools.partial(_mix_kernel, seq=seq)
    return pl.pallas_call(
        kernel,
        grid=(m // TM_MIX,),
        in_specs=[
            pl.BlockSpec((TM_MIX, D_MODEL), lambda i: (i, 0)),
            pl.BlockSpec((TM_MIX, POOL_WIDTH), lambda i: (i, 0)),
            pl.BlockSpec((POOL_HALO, POOL_WIDTH), lambda i: (jnp.maximum(i * halo_per_tile - 1, 0), 0)),
            pl.BlockSpec((TM_MIX, ATTN_WIDTH), lambda i: (i, 0)),
            pl.BlockSpec((POOL_GROUPS, POOL_GC, POOL_GC), lambda i: (0, 0, 0)),
            pl.BlockSpec((1, POOL_WIDTH), lambda i: (0, 0)),
            pl.BlockSpec((D_MODEL, D_MODEL), lambda i: (0, 0)),
            pl.BlockSpec((1, D_MODEL), lambda i: (0, 0)),
        ],
        out_specs=[
            pl.BlockSpec((TM_MIX, D_MODEL), lambda i: (i, 0)),
            pl.BlockSpec((TM_MIX, D_MODEL), lambda i: (i, 0)),
        ],
        out_shape=[
            jax.ShapeDtypeStruct((m, D_MODEL), F32),
            jax.ShapeDtypeStruct((m, D_MODEL), BF16),
        ],
        scratch_shapes=[pltpu.VMEM((TM_MIX + POOL_HALO, POOL_WIDTH), F32)],
        compiler_params=pltpu.CompilerParams(
            dimension_semantics=("arbitrary",), vmem_limit_bytes=VMEM_LIMIT),
        name="mix",
    )(x2d, u_pool, u_pool, attn2d, w_pool, pool_scale, w_out, g_ffn)


def _ffn_kernel(h_ref, x1_ref, wg_ref, wv_ref, cwg_ref, cwv_ref, cbg_ref, cbv_ref, wd_ref,
                o_ref, carry_s, *u_bufs, seq):
    m = pl.program_id(0)
    f = pl.program_id(1)
    tiles_per_seq = seq // TM_FFN
    first = (m % tiles_per_seq) == 0

    @pl.when(f == 0)
    def _():
        o_ref[...] = x1_ref[...]

    h = h_ref[...]
    n_half = TF_FFN // TF_HALF

    def up(s):
        cols = slice(s * TF_HALF, (s + 1) * TF_HALF)
        for k, w_ref in enumerate((wg_ref, wv_ref)):
            u = u_bufs[2 * s + k]
            u[0:SUBLANES, :] = jnp.where(first, 0.0, carry_s[f, 2 * s + k])
            u[SUBLANES:, :] = jnp.dot(h, w_ref[:, cols], preferred_element_type=F32)
            carry_s[f, 2 * s + k] = u[TM_FFN:, :]

    def conv(u, cols, cw_ref, cb_ref):
        out = cb_ref[:, cols]
        for j in range(CONV_WIDTH):
            shift = CONV_WIDTH - 1 - j
            out = out + u[pl.ds(SUBLANES - shift, TM_FFN), :] * cw_ref[j:j + 1, cols]
        return out

    def gated(s):
        cols = slice(s * TF_HALF, (s + 1) * TF_HALF)
        gate = conv(u_bufs[2 * s], cols, cwg_ref, cbg_ref)
        val = conv(u_bufs[2 * s + 1], cols, cwv_ref, cbv_ref)
        return (gate * jax.nn.sigmoid(gate) * val).astype(BF16)

    def down(s, act):
        return jnp.dot(act, wd_ref[s * TF_HALF:(s + 1) * TF_HALF, :], preferred_element_type=F32)

    up(0)
    act0 = gated(0)
    up(1)
    contrib = down(0, act0)
    act1 = gated(1)
    contrib = contrib + down(1, act1)
    o_ref[...] += contrib


def _ffn(h2, x1, w_up, conv_w, conv_b, w_down, seq):
    m = h2.shape[0]
    n_f = D_FF // TF_FFN
    kernel = functools.partial(_ffn_kernel, seq=seq)
    return pl.pallas_call(
        kernel,
        grid=(m // TM_FFN, n_f),
        in_specs=[
            pl.BlockSpec((TM_FFN, D_MODEL), lambda i, f: (i, 0)),
            pl.BlockSpec((TM_FFN, D_MODEL), lambda i, f: (i, 0)),
            pl.BlockSpec((D_MODEL, TF_FFN), lambda i, f: (0, f)),
            pl.BlockSpec((D_MODEL, TF_FFN), lambda i, f: (0, f + n_f)),
            pl.BlockSpec((CONV_WIDTH, TF_FFN), lambda i, f: (0, f)),
            pl.BlockSpec((CONV_WIDTH, TF_FFN), lambda i, f: (0, f + n_f)),
            pl.BlockSpec((1, TF_FFN), lambda i, f: (0, f)),
            pl.BlockSpec((1, TF_FFN), lambda i, f: (0, f + n_f)),
            pl.BlockSpec((TF_FFN, D_MODEL), lambda i, f: (f, 0)),
        ],
        out_specs=pl.BlockSpec((TM_FFN, D_MODEL), lambda i, f: (i, 0)),
        out_shape=jax.ShapeDtypeStruct((m, D_MODEL), F32),
        scratch_shapes=[pltpu.VMEM((n_f, 2 * (TF_FFN // TF_HALF), SUBLANES, TF_HALF), F32)]
        + [pltpu.VMEM((TM_FFN + SUBLANES, TF_HALF), F32)] * (2 * (TF_FFN // TF_HALF)),
        compiler_params=pltpu.CompilerParams(
            dimension_semantics=("arbitrary", "arbitrary"), vmem_limit_bytes=VMEM_LIMIT),
        name="ffn",
    )(h2, x1, w_up, w_up, conv_w, conv_w, conv_b, conv_b, w_down)


def _ple_kernel(x_ref, p_ref, gp_ref, wgate_ref, wproj_ref, gf_ref, o_ref, *, last_layer):
    x = x_ref[...]
    hg = _rms(x, gp_ref[...]).astype(BF16)
    gate = jax.nn.sigmoid(jnp.dot(hg, wgate_ref[...], preferred_element_type=F32))
    emb = jnp.dot(p_ref[...].astype(BF16), wproj_ref[...], preferred_element_type=F32)
    x3 = x + emb * gate
    o_ref[...] = _rms(x3, gf_ref[...]) if last_layer else x3


def _ple(x2, p2d, g_ple, w_gate, w_proj, g_final, last_layer):
    m = x2.shape[0]
    return pl.pallas_call(
        functools.partial(_ple_kernel, last_layer=last_layer),
        grid=(m // TM_PLE,),
        in_specs=[
            pl.BlockSpec((TM_PLE, D_MODEL), lambda i: (i, 0)),
            pl.BlockSpec((TM_PLE, D_PLE), lambda i: (i, 0)),
            pl.BlockSpec((1, D_MODEL), lambda i: (0, 0)),
            pl.BlockSpec((D_MODEL, D_MODEL), lambda i: (0, 0)),
            pl.BlockSpec((D_PLE, D_MODEL), lambda i: (0, 0)),
            pl.BlockSpec((1, D_MODEL), lambda i: (0, 0)),
        ],
        out_specs=pl.BlockSpec((TM_PLE, D_MODEL), lambda i: (i, 0)),
        out_shape=jax.ShapeDtypeStruct((m, D_MODEL), F32),
        compiler_params=pltpu.CompilerParams(
            dimension_semantics=("arbitrary",), vmem_limit_bytes=VMEM_LIMIT),
        name="ple",
    )(x2, p2d, g_ple, w_gate, w_proj, g_final)


def kernel(x, p, g_mix, w_in, w_pool, pool_scale, rel_bias, w_out, g_ffn, w_up, conv_w, conv_b,
           w_down, g_ple, w_ple_gate, w_ple_proj, g_final):
    bsz, seq, _ = x.shape
    depth = w_in.shape[0]
    assert seq % QB == 0 and seq % TM_MIX == 0 and seq % TM_FFN == 0 and (bsz * seq) % TM_IN == 0
    x2d = x.reshape(bsz * seq, D_MODEL)
    for i in range(depth):
        wi = w_in[i]
        cols = {}
        start = 0
        for name, width in zip(("pool", "q", "k", "v", "q_idx", "k_idx", "w_idx"), IN_SIZES):
            cols[name] = wi[:, start:start + width]
            start += width
        wt_main = jnp.concatenate([cols[nm] for nm in ("pool", "k", "q", "q_idx", "v")], axis=1).T.astype(BF16)
        wt_small = jnp.concatenate([cols["k_idx"], cols["w_idx"]], axis=1).T.astype(BF16)
        wt_small = jnp.pad(wt_small, ((0, SMALL_W - wt_small.shape[0]), (0, 0)))
        u_pool, k2d, qT, qiT, vT4, small = _inproj(x2d, g_mix[i].reshape(1, D_MODEL), wt_main, wt_small,
                                                   bsz, seq)
        small3 = small.reshape(bsz, seq, SMALL_W)
        kidx = small3[:, :, :IDX_DIM].astype(BF16)
        wT = jnp.swapaxes(small3[:, :, IDX_DIM:IDX_DIM + IDX_HEADS], 1, 2)
        attn = _dsa(rel_bias, qT, qiT, wT, kidx, k2d.reshape(bsz, seq, ATTN_WIDTH), vT4, seq)

        x1, h2 = _mix(x2d, u_pool, attn.reshape(bsz * seq, ATTN_WIDTH), w_pool[i].astype(BF16),
                      pool_scale[i].reshape(1, POOL_WIDTH), w_out[i].astype(BF16),
                      g_ffn[i].reshape(1, D_MODEL), seq)
        x2 = _ffn(h2, x1, w_up[i].astype(BF16), conv_w[i], conv_b[i].reshape(1, 2 * D_FF),
                  w_down[i].astype(BF16), seq)
        x2d = _ple(x2, p[i].reshape(bsz * seq, D_PLE), g_ple[i].reshape(1, D_MODEL),
                   w_ple_gate[i].astype(BF16), w_ple_proj[i].astype(BF16),
                   g_final.reshape(1, D_MODEL), last_layer=(i == depth - 1))
    return x2d.reshape(bsz, seq, D_MODEL)
```

```python
import functools
import math

import numpy as np
import jax
import jax.numpy as jnp
from jax import lax
from jax.experimental import pallas as pl
from jax.experimental.pallas import tpu as pltpu

F32 = jnp.float32
BF16 = jnp.bfloat16
I32 = jnp.int32
I16 = jnp.int16

D_MODEL = 2048
D_PLE = 256
POOL_WIDTH = 1024
POOL_GROUPS = 4
POOL_GC = 256
POOL_WINDOWS = (2, 4, 8, 16)
ATTN_HEADS = 8
HEAD_DIM = 128
ATTN_WIDTH = 1024
IDX_HEADS = 16
IDX_DIM = 64
TOPK_MAX = 256
N_BUCKETS = 32
MAX_DISTANCE = 128
D_FF = 5632
CONV_WIDTH = 3
EPS = 1e-6
IN_SIZES = (POOL_WIDTH, ATTN_WIDTH, ATTN_WIDTH, ATTN_WIDTH, IDX_HEADS * IDX_DIM, IDX_DIM, IDX_HEADS)

V7X_VMEM_BYTES = 64 * 1024 * 1024
SUBLANES = 8
PACKED_SUBLANES = 16
LANES = 128
VMEM_LIMIT = V7X_VMEM_BYTES - 8 * 1024 * 1024

TM_IN = 1024
TN_IN = 512
IN_TILES_PER_GROUP = 1024 // TN_IN
SMALL_W = LANES
QB = 256
KC = 256
TM_MIX = 512
POOL_HALO = 16
TM_FFN = 512
TF_FFN = 512
TF_HALF = 256
TM_PLE = 512

INT_MIN = -2 ** 31
I16_MIN = -2 ** 15
I16_MAX = 2 ** 15 - 1
NEG_BIG = -1e30
LOG2E = math.log2(math.e)


def _bucket_lower_bounds():
    n = np.arange(0, 4 * MAX_DISTANCE, dtype=np.int64)
    max_exact = N_BUCKETS // 2
    nf = np.maximum(n, 1).astype(np.float32)
    large = max_exact + (np.log(nf / np.float32(max_exact)) / np.float32(math.log(MAX_DISTANCE / max_exact))
                         * np.float32(N_BUCKETS - max_exact)).astype(np.int32)
    large = np.minimum(large, N_BUCKETS - 1)
    bucket = np.where(n < max_exact, n, large)
    assert np.all(np.diff(bucket) >= 0)
    lows = [int(np.argmax(bucket == b)) for b in range(N_BUCKETS)]
    assert all(bucket[lo] == b for b, lo in enumerate(lows))
    return lows


BUCKET_LO = _bucket_lower_bounds()


def _rms(x, g):
    ms = jnp.mean(x * x, axis=-1, keepdims=True)
    return (x * lax.rsqrt(ms + EPS)) * g


def _nt_dot(a, b):
    return lax.dot_general(a, b, (((1,), (1,)), ((), ())), preferred_element_type=F32)


def _inproj_kernel(x_ref, g_ref, wt_ref, wst_ref, pool_ref, k_ref, qT_ref, qiT_ref, vT_ref, small_ref, h_s):
    n = pl.program_id(1)
    group = n // IN_TILES_PER_GROUP

    @pl.when(n == 0)
    def _():
        hb = _rms(x_ref[...], g_ref[...]).astype(BF16)
        h_s[...] = hb
        small_ref[...] = _nt_dot(hb, wst_ref[...])

    @pl.when(group == 0)
    def _():
        pool_ref[...] = _nt_dot(h_s[...], wt_ref[...])

    @pl.when(group == 1)
    def _():
        k_ref[...] = _nt_dot(h_s[...], wt_ref[...]).astype(BF16)

    @pl.when(group == 2)
    def _():
        qT_ref[...] = _nt_dot(wt_ref[...], h_s[...]).astype(BF16)

    @pl.when(group == 3)
    def _():
        qiT_ref[...] = _nt_dot(wt_ref[...], h_s[...]).astype(BF16)

    @pl.when(group == 4)
    def _():
        out_t = _nt_dot(wt_ref[...], h_s[...]).astype(BF16)
        for j in range(TM_IN // KC):
            vT_ref[j] = out_t[:, j * KC:(j + 1) * KC]


def _inproj(x2d, g, wt_main, wt_small, bsz, seq):
    m = x2d.shape[0]
    tiles_per_seq = seq // TM_IN
    n_steps = wt_main.shape[0] // TN_IN

    def half(j, group):
        return jnp.clip(j - group * IN_TILES_PER_GROUP, 0, IN_TILES_PER_GROUP - 1)

    return pl.pallas_call(
        _inproj_kernel,
        grid=(m // TM_IN, n_steps),
        in_specs=[
            pl.BlockSpec((TM_IN, D_MODEL), lambda i, j: (i, 0)),
            pl.BlockSpec((1, D_MODEL), lambda i, j: (0, 0)),
            pl.BlockSpec((TN_IN, D_MODEL), lambda i, j: (j, 0)),
            pl.BlockSpec((SMALL_W, D_MODEL), lambda i, j: (0, 0)),
        ],
        out_specs=[
            pl.BlockSpec((TM_IN, TN_IN), lambda i, j: (i, half(j, 0))),
            pl.BlockSpec((TM_IN, TN_IN), lambda i, j: (i, half(j, 1))),
            pl.BlockSpec((None, TN_IN, TM_IN),
                         lambda i, j: (i // tiles_per_seq, half(j, 2), i % tiles_per_seq)),
            pl.BlockSpec((None, TN_IN, TM_IN),
                         lambda i, j: (i // tiles_per_seq, half(j, 3), i % tiles_per_seq)),
            pl.BlockSpec((None, TM_IN // KC, TN_IN, KC),
                         lambda i, j: (i // tiles_per_seq, i % tiles_per_seq, half(j, 4), 0)),
            pl.BlockSpec((TM_IN, SMALL_W), lambda i, j: (i, 0)),
        ],
        out_shape=[
            jax.ShapeDtypeStruct((m, POOL_WIDTH), F32),
            jax.ShapeDtypeStruct((m, ATTN_WIDTH), BF16),
            jax.ShapeDtypeStruct((bsz, ATTN_WIDTH, seq), BF16),
            jax.ShapeDtypeStruct((bsz, IDX_HEADS * IDX_DIM, seq), BF16),
            jax.ShapeDtypeStruct((bsz, seq // KC, ATTN_WIDTH, KC), BF16),
            jax.ShapeDtypeStruct((m, SMALL_W), F32),
        ],
        scratch_shapes=[pltpu.VMEM((TM_IN, D_MODEL), BF16)],
        compiler_params=pltpu.CompilerParams(
            dimension_semantics=("arbitrary", "arbitrary"), vmem_limit_bytes=VMEM_LIMIT),
        name="inproj",
    )(x2d, g, wt_main, wt_small)


def _dsa_kernel(rel_ref, qT_ref, qiT_ref, wT_ref, qiTn_ref, wTn_ref, kidx_ref, k_ref, vT_ref, o_ref,
                hi_s, lo_s, madd_s, bias_s, mx_s, alpha_s, acc_s, s_s, *, topk, n_q):
    b = pl.program_id(0)
    i = pl.program_id(1)
    n_ch = i + 1

    @pl.when((b == 0) & (i == 0))
    def _():
        r = lax.broadcasted_iota(I32, (KC, QB), 0)
        c = lax.broadcasted_iota(I32, (KC, QB), 1)
        for off in range(2):
            dist = jnp.maximum(c - r + off * QB, 0)

            def head_body(h, carry, dist=dist, off=off):
                t = jnp.full((KC, QB), rel_ref[0, h], F32)
                for bk in range(1, N_BUCKETS):
                    t = jnp.where(dist >= BUCKET_LO[bk], rel_ref[bk, h], t)
                bias_s[off, h] = (t - rel_ref[N_BUCKETS - 1, h]) * LOG2E
                return carry

            lax.fori_loop(0, ATTN_HEADS, head_body, 0)

    idx_scale = (IDX_HEADS ** -0.5) * (IDX_DIM ** -0.5)
    row = lax.broadcasted_iota(I32, (KC, QB), 0)
    col = lax.broadcasted_iota(I32, (KC, QB), 1)

    def index_chunk(m, qi_ref, w_scaled, q_block):
        kc = kidx_ref[pl.ds(pl.multiple_of(m * KC, KC), KC), :]
        acc = jnp.zeros((KC, QB), F32)
        for h in range(IDX_HEADS):
            d = jnp.dot(kc, qi_ref[h * IDX_DIM:(h + 1) * IDX_DIM, :],
                        preferred_element_type=F32)
            acc = acc + w_scaled[h:h + 1, :] * jnp.maximum(d, 0.0)
        bits = pltpu.bitcast(acc, I32)
        key = jnp.where(bits >= 0, bits, bits ^ 0x7FFFFFFF)
        causal = (m * KC + row) <= (q_block * QB + col)
        key = jnp.where(causal, key, INT_MIN)
        hi_s[m] = (key >> 16).astype(I16)
        lo_s[m] = ((key & 0xFFFF) + I16_MIN).astype(I16)

    @pl.when(i == 0)
    def _():
        index_chunk(0, qiT_ref, wT_ref[...] * idx_scale, 0)

    def count_ge(ref, cand):
        cand16 = cand.astype(I16)

        def cnt_body(m, accs):
            ge = (ref[m] >= cand16).astype(I16)
            accs = list(accs)
            for j in range(KC // PACKED_SUBLANES):
                a = j % len(accs)
                accs[a] = accs[a] + ge[j * PACKED_SUBLANES:(j + 1) * PACKED_SUBLANES, :]
            return tuple(accs)

        zero = jnp.zeros((PACKED_SUBLANES, QB), I16)
        accs = lax.fori_loop(0, n_ch, cnt_body, (zero,) * 4)
        tot = (accs[0] + accs[1]) + (accs[2] + accs[3])
        return jnp.sum(tot.astype(I32), axis=0, keepdims=True)

    def descend(ref, want):
        def bit_body(it, prefix):
            cand = prefix + jnp.left_shift(jnp.int32(1), 15 - it)
            return jnp.where(count_ge(ref, cand) >= want, cand, prefix)

        return lax.fori_loop(0, 16, bit_body, jnp.full((1, QB), I16_MIN, I32))

    p_hi = descend(hi_s, topk)
    n_above = jnp.where(p_hi >= I16_MAX, 0, count_ge(hi_s, jnp.minimum(p_hi + 1, I16_MAX)))
    hi_eq = jnp.where(p_hi > I16_MIN, p_hi, I16_MAX).astype(I16)

    def member_body(m, carry):
        lo_s[m] = jnp.where(hi_s[m] == hi_eq, lo_s[m], I16_MIN)
        return carry

    lax.fori_loop(0, n_ch, member_body, 0)
    p_lo = descend(lo_s, topk - n_above)
    p_hi16 = p_hi.astype(I16)
    p_lo16 = p_lo.astype(I16)

    def mask_body(m, carry):
        hi = hi_s[m]
        keep = (hi > p_hi16) | ((hi == hi_eq) & (lo_s[m] >= p_lo16))
        madd_s[m] = jnp.where(keep, jnp.bfloat16(0.0), jnp.bfloat16(NEG_BIG)).astype(F32)
        return carry

    lax.fori_loop(0, n_ch, mask_body, 0)

    scale2 = (HEAD_DIM ** -0.5) * LOG2E
    n_far = jnp.maximum(i - 1, 0)
    mx_s[...] = jnp.full(mx_s.shape, -jnp.inf, F32)
    acc_s[...] = jnp.zeros(acc_s.shape, F32)

    w_next = wTn_ref[...] * idx_scale

    def att_body(m, carry, near):
        index_chunk(m, qiTn_ref, w_next, i + 1)
        madd = madd_s[m]
        rows = pl.ds(pl.multiple_of(m * KC, KC), KC)
        for h in range(ATTN_HEADS):
            hd = slice(h * HEAD_DIM, (h + 1) * HEAD_DIM)
            s = jnp.dot(k_ref[rows, hd], qT_ref[hd, :], preferred_element_type=F32)
            if near:
                s = s * scale2 + (madd + bias_s[i - m, h])
            else:
                s = s * scale2 + madd
            s_s[h] = s
            mx = mx_s[h]
            m_new = jnp.maximum(mx, jnp.max(s, axis=0, keepdims=True))
            alpha_s[h] = jnp.exp2(mx - m_new)
            mx_s[h] = m_new
        ones_rows = jnp.ones((PACKED_SUBLANES, KC), BF16)
        for h in range(ATTN_HEADS):
            hd = slice(h * HEAD_DIM, (h + 1) * HEAD_DIM)
            p = jnp.exp2(s_s[h] - mx_s[h])
            v_aug = jnp.concatenate([vT_ref[m, hd, :], ones_rows], axis=0)
            pv = jnp.dot(v_aug, p.astype(BF16), preferred_element_type=F32)
            acc_s[h] = alpha_s[h] * acc_s[h] + pv
        return carry

    lax.fori_loop(0, n_far, functools.partial(att_body, near=False), 0)
    lax.fori_loop(n_far, n_ch, functools.partial(att_body, near=True), 0)

    @pl.when(i + 1 < n_q)
    def _():
        index_chunk(i + 1, qiTn_ref, w_next, i + 1)

    for h in range(ATTN_HEADS):
        acc = acc_s[h]
        out_t = acc[:HEAD_DIM] / acc[HEAD_DIM:HEAD_DIM + 1]
        o_ref[:, h * HEAD_DIM:(h + 1) * HEAD_DIM] = out_t.T.astype(o_ref.dtype)


def _dsa(rel_bias, qT, qiT, wT, kidx, k, vT4, seq):
    bsz = qT.shape[0]
    topk = min(TOPK_MAX, seq // 4)
    n_q = seq // QB
    kernel = functools.partial(_dsa_kernel, topk=topk, n_q=n_q)
    nxt = lambda i: jnp.minimum(i + 1, n_q - 1)
    return pl.pallas_call(
        kernel,
        grid=(bsz, n_q),
        in_specs=[
            pl.BlockSpec(memory_space=pltpu.SMEM),
            pl.BlockSpec((None, ATTN_WIDTH, QB), lambda b, i: (b, 0, i)),
            pl.BlockSpec((None, IDX_HEADS * IDX_DIM, QB), lambda b, i: (b, 0, i)),
            pl.BlockSpec((None, IDX_HEADS, QB), lambda b, i: (b, 0, i)),
            pl.BlockSpec((None, IDX_HEADS * IDX_DIM, QB), lambda b, i: (b, 0, nxt(i))),
            pl.BlockSpec((None, IDX_HEADS, QB), lambda b, i: (b, 0, nxt(i))),
            pl.BlockSpec((None, seq, IDX_DIM), lambda b, i: (b, 0, 0)),
            pl.BlockSpec((None, seq, ATTN_WIDTH), lambda b, i: (b, 0, 0)),
            pl.BlockSpec((None, seq // KC, ATTN_WIDTH, KC), lambda b, i: (b, 0, 0, 0)),
        ],
        out_specs=pl.BlockSpec((None, QB, ATTN_WIDTH), lambda b, i: (b, i, 0)),
        out_shape=jax.ShapeDtypeStruct((bsz, seq, ATTN_WIDTH), BF16),
        scratch_shapes=[
            pltpu.VMEM((seq // KC, KC, QB), I16),
            pltpu.VMEM((seq // KC, KC, QB), I16),
            pltpu.VMEM((seq // KC, KC, QB), F32),
            pltpu.VMEM((2, ATTN_HEADS, KC, QB), F32),
            pltpu.VMEM((ATTN_HEADS, 1, QB), F32),
            pltpu.VMEM((ATTN_HEADS, 1, QB), F32),
            pltpu.VMEM((ATTN_HEADS, HEAD_DIM + PACKED_SUBLANES, QB), F32),
            pltpu.VMEM((ATTN_HEADS, KC, QB), F32),
        ],
        compiler_params=pltpu.CompilerParams(
            dimension_semantics=("arbitrary", "arbitrary"), vmem_limit_bytes=VMEM_LIMIT),
        name="dsa",
    )(rel_bias, qT, qiT, wT, qiT, wT, kidx, k, vT4)


def _mix_kernel(x_ref, u_ref, halo_ref, attn_ref, wpool_ref, pscale_ref, wout_ref, g_ref,
                x1_ref, h2_ref, ext_s, *, seq):
    m = pl.program_id(0)
    tiles_per_seq = seq // TM_MIX
    first = (m % tiles_per_seq) == 0
    ext_s[0:POOL_HALO, :] = jnp.where(first, 0.0, halo_ref[...])
    ext_s[POOL_HALO:, :] = u_ref[...]

    t_pos = (m % tiles_per_seq) * TM_MIX + lax.broadcasted_iota(I32, (TM_MIX, POOL_GC), 0)
    acc = x_ref[...] + jnp.dot(attn_ref[...], wout_ref[POOL_WIDTH:, :], preferred_element_type=F32)
    for g, w in enumerate(POOL_WINDOWS):
        cols = slice(g * POOL_GC, (g + 1) * POOL_GC)
        u = ext_s[POOL_HALO:, cols]
        wsum = u
        for j in range(1, w):
            wsum = wsum + ext_s[pl.ds(POOL_HALO - j, TM_MIX), cols]
        count = jnp.minimum(t_pos + 1, w).astype(F32)
        pooled = (wsum / count - u).astype(BF16)
        y = jnp.dot(pooled, wpool_ref[g], preferred_element_type=F32) * pscale_ref[:, cols]
        acc = acc + jnp.dot(y.astype(BF16), wout_ref[cols, :], preferred_element_type=F32)
    x1_ref[...] = acc
    h2_ref[...] = _rms(acc, g_ref[...]).astype(BF16)


def _mix(x2d, u_pool, attn2d, w_pool, pool_scale, w_out, g_ffn, seq):
    m = x2d.shape[0]
    halo_per_tile = TM_MIX // POOL_HALO
    kernel = functools.partial(_mix_kernel, seq=seq)
    return pl.pallas_call(
        kernel,
        grid=(m // TM_MIX,),
        in_specs=[
            pl.BlockSpec((TM_MIX, D_MODEL), lambda i: (i, 0)),
            pl.BlockSpec((TM_MIX, POOL_WIDTH), lambda i: (i, 0)),
            pl.BlockSpec((POOL_HALO, POOL_WIDTH), lambda i: (jnp.maximum(i * halo_per_tile - 1, 0), 0)),
            pl.BlockSpec((TM_MIX, ATTN_WIDTH), lambda i: (i, 0)),
            pl.BlockSpec((POOL_GROUPS, POOL_GC, POOL_GC), lambda i: (0, 0, 0)),
            pl.BlockSpec((1, POOL_WIDTH), lambda i: (0, 0)),
            pl.BlockSpec((D_MODEL, D_MODEL), lambda i: (0, 0)),
            pl.BlockSpec((1, D_MODEL), lambda i: (0, 0)),
        ],
        out_specs=[
            pl.BlockSpec((TM_MIX, D_MODEL), lambda i: (i, 0)),
            pl.BlockSpec((TM_MIX, D_MODEL), lambda i: (i, 0)),
        ],
        out_shape=[
            jax.ShapeDtypeStruct((m, D_MODEL), F32),
            jax.ShapeDtypeStruct((m, D_MODEL), BF16),
        ],
        scratch_shapes=[pltpu.VMEM((TM_MIX + POOL_HALO, POOL_WIDTH), F32)],
        compiler_params=pltpu.CompilerParams(
            dimension_semantics=("arbitrary",), vmem_limit_bytes=VMEM_LIMIT),
        name="mix",
    )(x2d, u_pool, u_pool, attn2d, w_pool, pool_scale, w_out, g_ffn)


def _ffn_kernel(h_ref, x1_ref, wg_ref, wv_ref, cwg_ref, cwv_ref, cbg_ref, cbv_ref, wd_ref,
                o_ref, carry_s, *u_bufs, seq):
    m = pl.program_id(0)
    f = pl.program_id(1)
    tiles_per_seq = seq // TM_FFN
    first = (m % tiles_per_seq) == 0

    @pl.when(f == 0)
    def _():
        o_ref[...] = x1_ref[...]

    h = h_ref[...]
    n_half = TF_FFN // TF_HALF

    def up(s):
        cols = slice(s * TF_HALF, (s + 1) * TF_HALF)
        for k, w_ref in enumerate((wg_ref, wv_ref)):
            u = u_bufs[2 * s + k]
            u[0:SUBLANES, :] = jnp.where(first, 0.0, carry_s[f, 2 * s + k])
            u[SUBLANES:, :] = jnp.dot(h, w_ref[:, cols], preferred_element_type=F32)
            carry_s[f, 2 * s + k] = u[TM_FFN:, :]

    def conv(u, cols, cw_ref, cb_ref):
        out = cb_ref[:, cols]
        for j in range(CONV_WIDTH):
            shift = CONV_WIDTH - 1 - j
            out = out + u[pl.ds(SUBLANES - shift, TM_FFN), :] * cw_ref[j:j + 1, cols]
        return out

    def gated(s):
        cols = slice(s * TF_HALF, (s + 1) * TF_HALF)
        gate = conv(u_bufs[2 * s], cols, cwg_ref, cbg_ref)
        val = conv(u_bufs[2 * s + 1], cols, cwv_ref, cbv_ref)
        return (gate * jax.nn.sigmoid(gate) * val).astype(BF16)

    def down(s, act):
        return jnp.dot(act, wd_ref[s * TF_HALF:(s + 1) * TF_HALF, :], preferred_element_type=F32)

    up(0)
    act0 = gated(0)
    up(1)
    contrib = down(0, act0)
    act1 = gated(1)
    contrib = contrib + down(1, act1)
    o_ref[...] += contrib


def _ffn(h2, x1, w_up, conv_w, conv_b, w_down, seq):
    m = h2.shape[0]
    n_f = D_FF // TF_FFN
    kernel = functools.partial(_ffn_kernel, seq=seq)
    return pl.pallas_call(
        kernel,
        grid=(m // TM_FFN, n_f),
        in_specs=[
            pl.BlockSpec((TM_FFN, D_MODEL), lambda i, f: (i, 0)),
            pl.BlockSpec((TM_FFN, D_MODEL), lambda i, f: (i, 0)),
            pl.BlockSpec((D_MODEL, TF_FFN), lambda i, f: (0, f)),
            pl.BlockSpec((D_MODEL, TF_FFN), lambda i, f: (0, f + n_f)),
            pl.BlockSpec((CONV_WIDTH, TF_FFN), lambda i, f: (0, f)),
            pl.BlockSpec((CONV_WIDTH, TF_FFN), lambda i, f: (0, f + n_f)),
            pl.BlockSpec((1, TF_FFN), lambda i, f: (0, f)),
            pl.BlockSpec((1, TF_FFN), lambda i, f: (0, f + n_f)),
            pl.BlockSpec((TF_FFN, D_MODEL), lambda i, f: (f, 0)),
        ],
        out_specs=pl.BlockSpec((TM_FFN, D_MODEL), lambda i, f: (i, 0)),
        out_shape=jax.ShapeDtypeStruct((m, D_MODEL), F32),
        scratch_shapes=[pltpu.VMEM((n_f, 2 * (TF_FFN // TF_HALF), SUBLANES, TF_HALF), F32)]
        + [pltpu.VMEM((TM_FFN + SUBLANES, TF_HALF), F32)] * (2 * (TF_FFN // TF_HALF)),
        compiler_params=pltpu.CompilerParams(
            dimension_semantics=("arbitrary", "arbitrary"), vmem_limit_bytes=VMEM_LIMIT),
        name="ffn",
    )(h2, x1, w_up, w_up, conv_w, conv_w, conv_b, conv_b, w_down)


def _ple_kernel(x_ref, p_ref, gp_ref, wgate_ref, wproj_ref, gf_ref, o_ref, *, last_layer):
    x = x_ref[...]
    hg = _rms(x, gp_ref[...]).astype(BF16)
    gate = jax.nn.sigmoid(jnp.dot(hg, wgate_ref[...], preferred_element_type=F32))
    emb = jnp.dot(p_ref[...].astype(BF16), wproj_ref[...], preferred_element_type=F32)
    x3 = x + emb * gate
    o_ref[...] = _rms(x3, gf_ref[...]) if last_layer else x3


def _ple(x2, p2d, g_ple, w_gate, w_proj, g_final, last_layer):
    m = x2.shape[0]
    return pl.pallas_call(
        functools.partial(_ple_kernel, last_layer=last_layer),
        grid=(m // TM_PLE,),
        in_specs=[
            pl.BlockSpec((TM_PLE, D_MODEL), lambda i: (i, 0)),
            pl.BlockSpec((TM_PLE, D_PLE), lambda i: (i, 0)),
            pl.BlockSpec((1, D_MODEL), lambda i: (0, 0)),
            pl.BlockSpec((D_MODEL, D_MODEL), lambda i: (0, 0)),
            pl.BlockSpec((D_PLE, D_MODEL), lambda i: (0, 0)),
            pl.BlockSpec((1, D_MODEL), lambda i: (0, 0)),
        ],
        out_specs=pl.BlockSpec((TM_PLE, D_MODEL), lambda i: (i, 0)),
        out_shape=jax.ShapeDtypeStruct((m, D_MODEL), F32),
        compiler_params=pltpu.CompilerParams(
            dimension_semantics=("arbitrary",), vmem_limit_bytes=VMEM_LIMIT),
        name="ple",
    )(x2, p2d, g_ple, w_gate, w_proj, g_final)


def kernel(x, p, g_mix, w_in, w_pool, pool_scale, rel_bias, w_out, g_ffn, w_up, conv_w, conv_b,
           w_down, g_ple, w_ple_gate, w_ple_proj, g_final):
    bsz, seq, _ = x.shape
    depth = w_in.shape[0]
    assert seq % QB == 0 and seq % TM_MIX == 0 and seq % TM_FFN == 0 and (bsz * seq) % TM_IN == 0
    x2d = x.reshape(bsz * seq, D_MODEL)
    for i in range(depth):
        wi = w_in[i]
        cols = {}
        start = 0
        for name, width in zip(("pool", "q", "k", "v", "q_idx", "k_idx", "w_idx"), IN_SIZES):
            cols[name] = wi[:, start:start + width]
            start += width
        wt_main = jnp.concatenate([cols[nm] for nm in ("pool", "k", "q", "q_idx", "v")], axis=1).T.astype(BF16)
        wt_small = jnp.concatenate([cols["k_idx"], cols["w_idx"]], axis=1).T.astype(BF16)
        wt_small = jnp.pad(wt_small, ((0, SMALL_W - wt_small.shape[0]), (0, 0)))
        u_pool, k2d, qT, qiT, vT4, small = _inproj(x2d, g_mix[i].reshape(1, D_MODEL), wt_main, wt_small,
                                                   bsz, seq)
        small3 = small.reshape(bsz, seq, SMALL_W)
        kidx = small3[:, :, :IDX_DIM].astype(BF16)
        wT = jnp.swapaxes(small3[:, :, IDX_DIM:IDX_DIM + IDX_HEADS], 1, 2)
        attn = _dsa(rel_bias, qT, qiT, wT, kidx, k2d.reshape(bsz, seq, ATTN_WIDTH), vT4, seq)

        x1, h2 = _mix(x2d, u_pool, attn.reshape(bsz * seq, ATTN_WIDTH), w_pool[i].astype(BF16),
                      pool_scale[i].reshape(1, POOL_WIDTH), w_out[i].astype(BF16),
                      g_ffn[i].reshape(1, D_MODEL), seq)
        x2 = _ffn(h2, x1, w_up[i].astype(BF16), conv_w[i], conv_b[i].reshape(1, 2 * D_FF),
                  w_down[i].astype(BF16), seq)
        x2d = _ple(x2, p[i].reshape(bsz * seq, D_PLE), g_ple[i].reshape(1, D_MODEL),
                   w_ple_gate[i].astype(BF16), w_ple_proj[i].astype(BF16),
                   g_final.reshape(1, D_MODEL), last_layer=(i == depth - 1))
    return x2d.reshape(bsz, seq, D_MODEL)
```

```python
import functools
import math

import numpy as np
import jax
import jax.numpy as jnp
from jax import lax
from jax.experimental import pallas as pl
from jax.experimental.pallas import tpu as pltpu

F32 = jnp.float32
BF16 = jnp.bfloat16
I32 = jnp.int32
I16 = jnp.int16

D_MODEL = 2048
D_PLE = 256
POOL_WIDTH = 1024
POOL_GROUPS = 4
POOL_GC = 256
POOL_WINDOWS = (2, 4, 8, 16)
ATTN_HEADS = 8
HEAD_DIM = 128
ATTN_WIDTH = 1024
IDX_HEADS = 16
IDX_DIM = 64
TOPK_MAX = 256
N_BUCKETS = 32
MAX_DISTANCE = 128
D_FF = 5632
CONV_WIDTH = 3
EPS = 1e-6
IN_SIZES = (POOL_WIDTH, ATTN_WIDTH, ATTN_WIDTH, ATTN_WIDTH, IDX_HEADS * IDX_DIM, IDX_DIM, IDX_HEADS)
G_POOL, G_Q, G_K, G_V, G_QI = range(5)

V7X_VMEM_BYTES = 64 * 1024 * 1024
SUBLANES = 8
PACKED_SUBLANES = 16
LANES = 128
VMEM_LIMIT = V7X_VMEM_BYTES - 8 * 1024 * 1024

TM_IN = 1024
TN_IN = 512
IN_TILES_PER_GROUP = 1024 // TN_IN
SMALL_W = LANES
QB = 256
KC = 256
TM_MIX = 512
POOL_HALO = 16
TM_FFN = 512
TF_FFN = 512
TF_HALF = 256
TM_PLE = 512

INT_MIN = -2 ** 31
I16_MIN = -2 ** 15
I16_MAX = 2 ** 15 - 1
NEG_BIG = -1e30
LOG2E = math.log2(math.e)
Q_PRESCALE = (HEAD_DIM ** -0.5) * LOG2E


def _bucket_lower_bounds():
    n = np.arange(0, 4 * MAX_DISTANCE, dtype=np.int64)
    max_exact = N_BUCKETS // 2
    nf = np.maximum(n, 1).astype(np.float32)
    large = max_exact + (np.log(nf / np.float32(max_exact)) / np.float32(math.log(MAX_DISTANCE / max_exact))
                         * np.float32(N_BUCKETS - max_exact)).astype(np.int32)
    large = np.minimum(large, N_BUCKETS - 1)
    bucket = np.where(n < max_exact, n, large)
    assert np.all(np.diff(bucket) >= 0)
    lows = [int(np.argmax(bucket == b)) for b in range(N_BUCKETS)]
    assert all(bucket[lo] == b for b, lo in enumerate(lows))
    return lows


BUCKET_LO = _bucket_lower_bounds()


def _rms(x, g):
    ms = jnp.mean(x * x, axis=-1, keepdims=True)
    return (x * lax.rsqrt(ms + EPS)) * g


def _nt_dot(a, b):
    return lax.dot_general(a, b, (((1,), (1,)), ((), ())), preferred_element_type=F32)


def _inproj_kernel(x_ref, g_ref, wt_ref, wst_ref, pool_ref, k_ref, qT_ref, qiT_ref, vT_ref, small_ref, h_s):
    n = pl.program_id(1)
    group = n // IN_TILES_PER_GROUP

    @pl.when(n == 0)
    def _():
        hb = _rms(x_ref[...], g_ref[...]).astype(BF16)
        h_s[...] = hb
        small_ref[...] = _nt_dot(hb, wst_ref[...])

    @pl.when(group == G_POOL)
    def _():
        pool_ref[...] = _nt_dot(h_s[...], wt_ref[...])

    @pl.when(group == G_K)
    def _():
        k_ref[...] = _nt_dot(h_s[...], wt_ref[...]).astype(BF16)

    @pl.when(group == G_Q)
    def _():
        qT_ref[...] = (_nt_dot(wt_ref[...], h_s[...]) * Q_PRESCALE).astype(BF16)

    @pl.when(group == G_QI)
    def _():
        qiT_ref[...] = _nt_dot(wt_ref[...], h_s[...]).astype(BF16)

    @pl.when(group == G_V)
    def _():
        out_t = _nt_dot(wt_ref[...], h_s[...]).astype(BF16)
        for j in range(TM_IN // KC):
            vT_ref[j] = out_t[:, j * KC:(j + 1) * KC]


def _inproj(x2d, g, wt_main, wt_small, bsz, seq):
    m = x2d.shape[0]
    tiles_per_seq = seq // TM_IN
    n_steps = wt_main.shape[0] // TN_IN

    def half(j, group):
        return jnp.clip(j - group * IN_TILES_PER_GROUP, 0, IN_TILES_PER_GROUP - 1)

    return pl.pallas_call(
        _inproj_kernel,
        grid=(m // TM_IN, n_steps),
        in_specs=[
            pl.BlockSpec((TM_IN, D_MODEL), lambda i, j: (i, 0)),
            pl.BlockSpec((1, D_MODEL), lambda i, j: (0, 0)),
            pl.BlockSpec((TN_IN, D_MODEL), lambda i, j: (j, 0)),
            pl.BlockSpec((SMALL_W, D_MODEL), lambda i, j: (0, 0)),
        ],
        out_specs=[
            pl.BlockSpec((TM_IN, TN_IN), lambda i, j: (i, half(j, G_POOL))),
            pl.BlockSpec((TM_IN, TN_IN), lambda i, j: (i, half(j, G_K))),
            pl.BlockSpec((None, TN_IN, TM_IN),
                         lambda i, j: (i // tiles_per_seq, half(j, G_Q), i % tiles_per_seq)),
            pl.BlockSpec((None, TN_IN, TM_IN),
                         lambda i, j: (i // tiles_per_seq, half(j, G_QI), i % tiles_per_seq)),
            pl.BlockSpec((None, TM_IN // KC, TN_IN, KC),
                         lambda i, j: (i // tiles_per_seq, i % tiles_per_seq, half(j, G_V), 0)),
            pl.BlockSpec((TM_IN, SMALL_W), lambda i, j: (i, 0)),
        ],
        out_shape=[
            jax.ShapeDtypeStruct((m, POOL_WIDTH), F32),
            jax.ShapeDtypeStruct((m, ATTN_WIDTH), BF16),
            jax.ShapeDtypeStruct((bsz, ATTN_WIDTH, seq), BF16),
            jax.ShapeDtypeStruct((bsz, IDX_HEADS * IDX_DIM, seq), BF16),
            jax.ShapeDtypeStruct((bsz, seq // KC, ATTN_WIDTH, KC), BF16),
            jax.ShapeDtypeStruct((m, SMALL_W), F32),
        ],
        scratch_shapes=[pltpu.VMEM((TM_IN, D_MODEL), BF16)],
        compiler_params=pltpu.CompilerParams(
            dimension_semantics=("arbitrary", "arbitrary"), vmem_limit_bytes=VMEM_LIMIT),
        name="inproj",
    )(x2d, g, wt_main, wt_small)


def _dsa_kernel(rel_ref, qT_ref, qiT_ref, wT_ref, qiTn_ref, wTn_ref, kidx_ref, k_ref, vT_ref, o_ref,
                hi_s, lo_s, madd_s, bias_s, mx_s, alpha_s, acc_s, s_s, *, topk, n_q):
    b = pl.program_id(0)
    i = pl.program_id(1)
    n_ch = i + 1

    @pl.when((b == 0) & (i == 0))
    def _():
        r = lax.broadcasted_iota(I32, (KC, QB), 0)
        c = lax.broadcasted_iota(I32, (KC, QB), 1)
        for off in range(2):
            dist = jnp.maximum(c - r + off * QB, 0)

            def head_body(h, carry, dist=dist, off=off):
                t = jnp.full((KC, QB), rel_ref[0, h], F32)
                for bk in range(1, N_BUCKETS):
                    t = jnp.where(dist >= BUCKET_LO[bk], rel_ref[bk, h], t)
                bias_s[off, h] = (t - rel_ref[N_BUCKETS - 1, h]) * LOG2E
                return carry

            lax.fori_loop(0, ATTN_HEADS, head_body, 0)

    idx_scale = (IDX_HEADS ** -0.5) * (IDX_DIM ** -0.5)
    row = lax.broadcasted_iota(I32, (KC, QB), 0)
    col = lax.broadcasted_iota(I32, (KC, QB), 1)

    def index_chunk(m, qi_ref, w_scaled, q_block):
        kc = kidx_ref[pl.ds(pl.multiple_of(m * KC, KC), KC), :]
        acc = jnp.zeros((KC, QB), F32)
        for h in range(IDX_HEADS):
            d = jnp.dot(kc, qi_ref[h * IDX_DIM:(h + 1) * IDX_DIM, :],
                        preferred_element_type=F32)
            acc = acc + w_scaled[h:h + 1, :] * jnp.maximum(d, 0.0)
        bits = pltpu.bitcast(acc, I32)
        key = jnp.where(bits >= 0, bits, bits ^ 0x7FFFFFFF)
        causal = (m * KC + row) <= (q_block * QB + col)
        key = jnp.where(causal, key, INT_MIN)
        hi_s[m] = (key >> 16).astype(I16)
        lo_s[m] = ((key & 0xFFFF) + I16_MIN).astype(I16)

    @pl.when(i == 0)
    def _():
        index_chunk(0, qiT_ref, wT_ref[...] * idx_scale, 0)

    def count_ge(ref, cand):
        cand16 = cand.astype(I16)

        def cnt_body(m, accs):
            ge = (ref[m] >= cand16).astype(I16)
            accs = list(accs)
            for j in range(KC // PACKED_SUBLANES):
                a = j % len(accs)
                accs[a] = accs[a] + ge[j * PACKED_SUBLANES:(j + 1) * PACKED_SUBLANES, :]
            return tuple(accs)

        zero = jnp.zeros((PACKED_SUBLANES, QB), I16)
        accs = lax.fori_loop(0, n_ch, cnt_body, (zero,) * 4)
        tot = (accs[0] + accs[1]) + (accs[2] + accs[3])
        return jnp.sum(tot.astype(I32), axis=0, keepdims=True)

    def descend(ref, want):
        def bit_body(it, prefix):
            cand = prefix + jnp.left_shift(jnp.int32(1), 15 - it)
            return jnp.where(count_ge(ref, cand) >= want, cand, prefix)

        return lax.fori_loop(0, 16, bit_body, jnp.full((1, QB), I16_MIN, I32))

    p_hi = descend(hi_s, topk)
    n_above = jnp.where(p_hi >= I16_MAX, 0, count_ge(hi_s, jnp.minimum(p_hi + 1, I16_MAX)))
    hi_eq = jnp.where(p_hi > I16_MIN, p_hi, I16_MAX).astype(I16)

    def member_body(m, carry):
        lo_s[m] = jnp.where(hi_s[m] == hi_eq, lo_s[m], I16_MIN)
        return carry

    lax.fori_loop(0, n_ch, member_body, 0)
    p_lo = descend(lo_s, topk - n_above)
    p_hi16 = p_hi.astype(I16)
    p_lo16 = p_lo.astype(I16)

    def mask_body(m, carry):
        hi = hi_s[m]
        keep = (hi > p_hi16) | ((hi == hi_eq) & (lo_s[m] >= p_lo16))
        madd_s[m] = jnp.where(keep, jnp.bfloat16(0.0), jnp.bfloat16(NEG_BIG)).astype(F32)
        return carry

    lax.fori_loop(0, n_ch, mask_body, 0)

    n_far = jnp.maximum(i - 1, 0)
    mx_s[...] = jnp.full(mx_s.shape, -jnp.inf, F32)
    acc_s[...] = jnp.zeros(acc_s.shape, F32)

    w_next = wTn_ref[...] * idx_scale

    def att_body(m, carry, near):
        index_chunk(m, qiTn_ref, w_next, i + 1)
        madd = madd_s[m]
        rows = pl.ds(pl.multiple_of(m * KC, KC), KC)
        for h in range(ATTN_HEADS):
            hd = slice(h * HEAD_DIM, (h + 1) * HEAD_DIM)
            s = jnp.dot(k_ref[rows, hd], qT_ref[hd, :], preferred_element_type=F32)
            if near:
                s = s + (madd + bias_s[i - m, h])
            else:
                s = s + madd
            s_s[h] = s
            mx = mx_s[h]
            m_new = jnp.maximum(mx, jnp.max(s, axis=0, keepdims=True))
            alpha_s[h] = jnp.exp2(mx - m_new)
            mx_s[h] = m_new
        ones_rows = jnp.ones((PACKED_SUBLANES, KC), BF16)
        for h in range(ATTN_HEADS):
            hd = slice(h * HEAD_DIM, (h + 1) * HEAD_DIM)
            p = jnp.exp2(s_s[h] - mx_s[h])
            v_aug = jnp.concatenate([vT_ref[m, hd, :], ones_rows], axis=0)
            pv = jnp.dot(v_aug, p.astype(BF16), preferred_element_type=F32)
            acc_s[h] = alpha_s[h] * acc_s[h] + pv
        return carry

    lax.fori_loop(0, n_far, functools.partial(att_body, near=False), 0)
    lax.fori_loop(n_far, n_ch, functools.partial(att_body, near=True), 0)

    @pl.when(i + 1 < n_q)
    def _():
        index_chunk(i + 1, qiTn_ref, w_next, i + 1)

    for h in range(ATTN_HEADS):
        acc = acc_s[h]
        out_t = acc[:HEAD_DIM] / acc[HEAD_DIM:HEAD_DIM + 1]
        o_ref[:, h * HEAD_DIM:(h + 1) * HEAD_DIM] = out_t.T.astype(o_ref.dtype)


def _dsa(rel_bias, qT, qiT, wT, kidx, k, vT4, seq):
    bsz = qT.shape[0]
    topk = min(TOPK_MAX, seq // 4)
    n_q = seq // QB
    kernel = functools.partial(_dsa_kernel, topk=topk, n_q=n_q)
    nxt = lambda i: jnp.minimum(i + 1, n_q - 1)
    return pl.pallas_call(
        kernel,
        grid=(bsz, n_q),
        in_specs=[
            pl.BlockSpec(memory_space=pltpu.SMEM),
            pl.BlockSpec((None, ATTN_WIDTH, QB), lambda b, i: (b, 0, i)),
            pl.BlockSpec((None, IDX_HEADS * IDX_DIM, QB), lambda b, i: (b, 0, i)),
            pl.BlockSpec((None, IDX_HEADS, QB), lambda b, i: (b, 0, i)),
            pl.BlockSpec((None, IDX_HEADS * IDX_DIM, QB), lambda b, i: (b, 0, nxt(i))),
            pl.BlockSpec((None, IDX_HEADS, QB), lambda b, i: (b, 0, nxt(i))),
            pl.BlockSpec((None, seq, IDX_DIM), lambda b, i: (b, 0, 0)),
            pl.BlockSpec((None, seq, ATTN_WIDTH), lambda b, i: (b, 0, 0)),
            pl.BlockSpec((None, seq // KC, ATTN_WIDTH, KC), lambda b, i: (b, 0, 0, 0)),
        ],
        out_specs=pl.BlockSpec((None, QB, ATTN_WIDTH), lambda b, i: (b, i, 0)),
        out_shape=jax.ShapeDtypeStruct((bsz, seq, ATTN_WIDTH), BF16),
        scratch_shapes=[
            pltpu.VMEM((seq // KC, KC, QB), I16),
            pltpu.VMEM((seq // KC, KC, QB), I16),
            pltpu.VMEM((seq // KC, KC, QB), F32),
            pltpu.VMEM((2, ATTN_HEADS, KC, QB), F32),
            pltpu.VMEM((ATTN_HEADS, 1, QB), F32),
            pltpu.VMEM((ATTN_HEADS, 1, QB), F32),
            pltpu.VMEM((ATTN_HEADS, HEAD_DIM + PACKED_SUBLANES, QB), F32),
            pltpu.VMEM((ATTN_HEADS, KC, QB), F32),
        ],
        compiler_params=pltpu.CompilerParams(
            dimension_semantics=("arbitrary", "arbitrary"), vmem_limit_bytes=VMEM_LIMIT),
        name="dsa",
    )(rel_bias, qT, qiT, wT, qiT, wT, kidx, k, vT4)


def _mix_kernel(x_ref, u_ref, halo_ref, attn_ref, wpool_ref, pscale_ref, wout_ref, g_ref,
                x1_ref, h2_ref, ext_s, *, seq):
    m = pl.program_id(0)
    tiles_per_seq = seq // TM_MIX
    first = (m % tiles_per_seq) == 0
    ext_s[0:POOL_HALO, :] = jnp.where(first, 0.0, halo_ref[...])
    ext_s[POOL_HALO:, :] = u_ref[...]

    t_pos = (m % tiles_per_seq) * TM_MIX + lax.broadcasted_iota(I32, (TM_MIX, POOL_GC), 0)
    acc = x_ref[...] + jnp.dot(attn_ref[...], wout_ref[POOL_WIDTH:, :], preferred_element_type=F32)
    for g, w in enumerate(POOL_WINDOWS):
        cols = slice(g * POOL_GC, (g + 1) * POOL_GC)
        u = ext_s[POOL_HALO:, cols]
        wsum = u
        for j in range(1, w):
            wsum = wsum + ext_s[pl.ds(POOL_HALO - j, TM_MIX), cols]
        count = jnp.minimum(t_pos + 1, w).astype(F32)
        pooled = (wsum / count - u).astype(BF16)
        y = jnp.dot(pooled, wpool_ref[g], preferred_element_type=F32) * pscale_ref[:, cols]
        acc = acc + jnp.dot(y.astype(BF16), wout_ref[cols, :], preferred_element_type=F32)
    x1_ref[...] = acc
    h2_ref[...] = _rms(acc, g_ref[...]).astype(BF16)


def _mix(x2d, u_pool, attn2d, w_pool, pool_scale, w_out, g_ffn, seq):
    m = x2d.shape[0]
    halo_per_tile = TM_MIX // POOL_HALO
    kernel = functools.partial(_mix_kernel, seq=seq)
    return pl.pallas_call(
        kernel,
        grid=(m // TM_MIX,),
        in_specs=[
            pl.BlockSpec((TM_MIX, D_MODEL), lambda i: (i, 0)),
            pl.BlockSpec((TM_MIX, POOL_WIDTH), lambda i: (i, 0)),
            pl.BlockSpec((POOL_HALO, POOL_WIDTH), lambda i: (jnp.maximum(i * halo_per_tile - 1, 0), 0)),
            pl.BlockSpec((TM_MIX, ATTN_WIDTH), lambda i: (i, 0)),
            pl.BlockSpec((POOL_GROUPS, POOL_GC, POOL_GC), lambda i: (0, 0, 0)),
            pl.BlockSpec((1, POOL_WIDTH), lambda i: (0, 0)),
            pl.BlockSpec((D_MODEL, D_MODEL), lambda i: (0, 0)),
            pl.BlockSpec((1, D_MODEL), lambda i: (0, 0)),
        ],
        out_specs=[
            pl.BlockSpec((TM_MIX, D_MODEL), lambda i: (i, 0)),
            pl.BlockSpec((TM_MIX, D_MODEL), lambda i: (i, 0)),
        ],
        out_shape=[
            jax.ShapeDtypeStruct((m, D_MODEL), F32),
            jax.ShapeDtypeStruct((m, D_MODEL), BF16),
        ],
        scratch_shapes=[pltpu.VMEM((TM_MIX + POOL_HALO, POOL_WIDTH), F32)],
        compiler_params=pltpu.CompilerParams(
            dimension_semantics=("arbitrary",), vmem_limit_bytes=VMEM_LIMIT),
        name="mix",
    )(x2d, u_pool, u_pool, attn2d, w_pool, pool_scale, w_out, g_ffn)


def _ffn_kernel(h_ref, x1_ref, wg_ref, wv_ref, cwg_ref, cwv_ref, cbg_ref, cbv_ref, wd_ref,
                o_ref, carry_s, *u_bufs, seq):
    m = pl.program_id(0)
    f = pl.program_id(1)
    tiles_per_seq = seq // TM_FFN
    first = (m % tiles_per_seq) == 0

    @pl.when(f == 0)
    def _():
        o_ref[...] = x1_ref[...]

    h = h_ref[...]
    n_half = TF_FFN // TF_HALF

    def up(s):
        cols = slice(s * TF_HALF, (s + 1) * TF_HALF)
        for k, w_ref in enumerate((wg_ref, wv_ref)):
            u = u_bufs[2 * s + k]
            u[0:SUBLANES, :] = jnp.where(first, 0.0, carry_s[f, 2 * s + k])
            u[SUBLANES:, :] = jnp.dot(h, w_ref[:, cols], preferred_element_type=F32)
            carry_s[f, 2 * s + k] = u[TM_FFN:, :]

    def conv(u, cols, cw_ref, cb_ref):
        out = cb_ref[:, cols]
        for j in range(CONV_WIDTH):
            shift = CONV_WIDTH - 1 - j
            out = out + u[pl.ds(SUBLANES - shift, TM_FFN), :] * cw_ref[j:j + 1, cols]
        return out

    def gated(s):
        cols = slice(s * TF_HALF, (s + 1) * TF_HALF)
        gate = conv(u_bufs[2 * s], cols, cwg_ref, cbg_ref)
        val = conv(u_bufs[2 * s + 1], cols, cwv_ref, cbv_ref)
        return (gate * jax.nn.sigmoid(gate) * val).astype(BF16)

    def down(s, act):
        return jnp.dot(act, wd_ref[s * TF_HALF:(s + 1) * TF_HALF, :], preferred_element_type=F32)

    up(0)
    act0 = gated(0)
    up(1)
    contrib = down(0, act0)
    act1 = gated(1)
    contrib = contrib + down(1, act1)
    o_ref[...] += contrib


def _ffn(h2, x1, w_up, conv_w, conv_b, w_down, seq):
    m = h2.shape[0]
    n_f = D_FF // TF_FFN
    kernel = functools.partial(_ffn_kernel, seq=seq)
    return pl.pallas_call(
        kernel,
        grid=(m // TM_FFN, n_f),
        in_specs=[
            pl.BlockSpec((TM_FFN, D_MODEL), lambda i, f: (i, 0)),
            pl.BlockSpec((TM_FFN, D_MODEL), lambda i, f: (i, 0)),
            pl.BlockSpec((D_MODEL, TF_FFN), lambda i, f: (0, f)),
            pl.BlockSpec((D_MODEL, TF_FFN), lambda i, f: (0, f + n_f)),
            pl.BlockSpec((CONV_WIDTH, TF_FFN), lambda i, f: (0, f)),
            pl.BlockSpec((CONV_WIDTH, TF_FFN), lambda i, f: (0, f + n_f)),
            pl.BlockSpec((1, TF_FFN), lambda i, f: (0, f)),
            pl.BlockSpec((1, TF_FFN), lambda i, f: (0, f + n_f)),
            pl.BlockSpec((TF_FFN, D_MODEL), lambda i, f: (f, 0)),
        ],
        out_specs=pl.BlockSpec((TM_FFN, D_MODEL), lambda i, f: (i, 0)),
        out_shape=jax.ShapeDtypeStruct((m, D_MODEL), F32),
        scratch_shapes=[pltpu.VMEM((n_f, 2 * (TF_FFN // TF_HALF), SUBLANES, TF_HALF), F32)]
        + [pltpu.VMEM((TM_FFN + SUBLANES, TF_HALF), F32)] * (2 * (TF_FFN // TF_HALF)),
        compiler_params=pltpu.CompilerParams(
            dimension_semantics=("arbitrary", "arbitrary"), vmem_limit_bytes=VMEM_LIMIT),
        name="ffn",
    )(h2, x1, w_up, w_up, conv_w, conv_w, conv_b, conv_b, w_down)


def _ple_kernel(x_ref, p_ref, gp_ref, wgate_ref, wproj_ref, gf_ref, o_ref, *, last_layer):
    x = x_ref[...]
    hg = _rms(x, gp_ref[...]).astype(BF16)
    gate = jax.nn.sigmoid(jnp.dot(hg, wgate_ref[...], preferred_element_type=F32))
    emb = jnp.dot(p_ref[...].astype(BF16), wproj_ref[...], preferred_element_type=F32)
    x3 = x + emb * gate
    o_ref[...] = _rms(x3, gf_ref[...]) if last_layer else x3


def _ple(x2, p2d, g_ple, w_gate, w_proj, g_final, last_layer):
    m = x2.shape[0]
    return pl.pallas_call(
        functools.partial(_ple_kernel, last_layer=last_layer),
        grid=(m // TM_PLE,),
        in_specs=[
            pl.BlockSpec((TM_PLE, D_MODEL), lambda i: (i, 0)),
            pl.BlockSpec((TM_PLE, D_PLE), lambda i: (i, 0)),
            pl.BlockSpec((1, D_MODEL), lambda i: (0, 0)),
            pl.BlockSpec((D_MODEL, D_MODEL), lambda i: (0, 0)),
            pl.BlockSpec((D_PLE, D_MODEL), lambda i: (0, 0)),
            pl.BlockSpec((1, D_MODEL), lambda i: (0, 0)),
        ],
        out_specs=pl.BlockSpec((TM_PLE, D_MODEL), lambda i: (i, 0)),
        out_shape=jax.ShapeDtypeStruct((m, D_MODEL), F32),
        compiler_params=pltpu.CompilerParams(
            dimension_semantics=("arbitrary",), vmem_limit_bytes=VMEM_LIMIT),
        name="ple",
    )(x2, p2d, g_ple, w_gate, w_proj, g_final)


def kernel(x, p, g_mix, w_in, w_pool, pool_scale, rel_bias, w_out, g_ffn, w_up, conv_w, conv_b,
           w_down, g_ple, w_ple_gate, w_ple_proj, g_final):
    bsz, seq, _ = x.shape
    depth = w_in.shape[0]
    assert seq % QB == 0 and seq % TM_MIX == 0 and seq % TM_FFN == 0 and (bsz * seq) % TM_IN == 0
    x2d = x.reshape(bsz * seq, D_MODEL)
    for i in range(depth):
        n_main = sum(IN_SIZES[:5])
        wt_main = w_in[i][:, :n_main].T.astype(BF16)
        wt_small = w_in[i][:, n_main:].T.astype(BF16)
        wt_small = jnp.pad(wt_small, ((0, SMALL_W - wt_small.shape[0]), (0, 0)))
        u_pool, k2d, qT, qiT, vT4, small = _inproj(x2d, g_mix[i].reshape(1, D_MODEL), wt_main, wt_small,
                                                   bsz, seq)
        small3 = small.reshape(bsz, seq, SMALL_W)
        kidx = small3[:, :, :IDX_DIM].astype(BF16)
        wT = jnp.swapaxes(small3[:, :, IDX_DIM:IDX_DIM + IDX_HEADS], 1, 2)
        attn = _dsa(rel_bias, qT, qiT, wT, kidx, k2d.reshape(bsz, seq, ATTN_WIDTH), vT4, seq)

        x1, h2 = _mix(x2d, u_pool, attn.reshape(bsz * seq, ATTN_WIDTH), w_pool[i].astype(BF16),
                      pool_scale[i].reshape(1, POOL_WIDTH), w_out[i].astype(BF16),
                      g_ffn[i].reshape(1, D_MODEL), seq)
        x2 = _ffn(h2, x1, w_up[i].astype(BF16), conv_w[i], conv_b[i].reshape(1, 2 * D_FF),
                  w_down[i].astype(BF16), seq)
        x2d = _ple(x2, p[i].reshape(bsz * seq, D_PLE), g_ple[i].reshape(1, D_MODEL),
                   w_ple_gate[i].astype(BF16), w_ple_proj[i].astype(BF16),
                   g_final.reshape(1, D_MODEL), last_layer=(i == depth - 1))
    return x2d.reshape(bsz, seq, D_MODEL)
```

```python
import functools
import math

import numpy as np
import jax
import jax.numpy as jnp
from jax import lax
from jax.experimental import pallas as pl
from jax.experimental.pallas import tpu as pltpu

F32 = jnp.float32
BF16 = jnp.bfloat16
I32 = jnp.int32
I16 = jnp.int16

D_MODEL = 2048
D_PLE = 256
POOL_WIDTH = 1024
POOL_GROUPS = 4
POOL_GC = 256
POOL_WINDOWS = (2, 4, 8, 16)
ATTN_HEADS = 8
HEAD_DIM = 128
ATTN_WIDTH = 1024
IDX_HEADS = 16
IDX_DIM = 64
TOPK_MAX = 256
N_BUCKETS = 32
MAX_DISTANCE = 128
D_FF = 5632
CONV_WIDTH = 3
EPS = 1e-6
IN_SIZES = (POOL_WIDTH, ATTN_WIDTH, ATTN_WIDTH, ATTN_WIDTH, IDX_HEADS * IDX_DIM, IDX_DIM, IDX_HEADS)
G_POOL, G_Q, G_K, G_V, G_QI = range(5)

V7X_VMEM_BYTES = 64 * 1024 * 1024
SUBLANES = 8
PACKED_SUBLANES = 16
LANES = 128
VMEM_LIMIT = V7X_VMEM_BYTES - 8 * 1024 * 1024

TM_IN = 1024
TN_IN = 512
IN_TILES_PER_GROUP = 1024 // TN_IN
SMALL_W = IDX_DIM + IDX_HEADS
QB = 256
KC = 256
TM_MIX = 512
POOL_HALO = 16
TM_FFN = 512
TF_FFN = 512
TF_HALF = 256
TM_PLE = 512
CAST_ROWS = 256

INT_MIN = -2 ** 31
I16_MIN = -2 ** 15
I16_MAX = 2 ** 15 - 1
NEG_BIG = -1e30
LOG2E = math.log2(math.e)
Q_PRESCALE = (HEAD_DIM ** -0.5) * LOG2E


def _bucket_lower_bounds():
    n = np.arange(0, 4 * MAX_DISTANCE, dtype=np.int64)
    max_exact = N_BUCKETS // 2
    nf = np.maximum(n, 1).astype(np.float32)
    large = max_exact + (np.log(nf / np.float32(max_exact)) / np.float32(math.log(MAX_DISTANCE / max_exact))
                         * np.float32(N_BUCKETS - max_exact)).astype(np.int32)
    large = np.minimum(large, N_BUCKETS - 1)
    bucket = np.where(n < max_exact, n, large)
    assert np.all(np.diff(bucket) >= 0)
    lows = [int(np.argmax(bucket == b)) for b in range(N_BUCKETS)]
    assert all(bucket[lo] == b for b, lo in enumerate(lows))
    return lows


BUCKET_LO = _bucket_lower_bounds()


def _rms(x, g):
    ms = jnp.mean(x * x, axis=-1, keepdims=True)
    return (x * lax.rsqrt(ms + EPS)) * g


def _nt_dot(a, b):
    return lax.dot_general(a, b, (((1,), (1,)), ((), ())), preferred_element_type=F32)


def _inproj_kernel(x_ref, g_ref, wt_ref, wst_ref, pool_ref, k_ref, qT_ref, qiT_ref, vT_ref, small_ref, h_s):
    n = pl.program_id(1)
    group = n // IN_TILES_PER_GROUP

    @pl.when(n == 0)
    def _():
        hb = _rms(x_ref[...], g_ref[...]).astype(BF16)
        h_s[...] = hb
        small_ref[...] = _nt_dot(hb, wst_ref[...])

    @pl.when(group == G_POOL)
    def _():
        pool_ref[...] = _nt_dot(h_s[...], wt_ref[...])

    @pl.when(group == G_K)
    def _():
        k_ref[...] = _nt_dot(h_s[...], wt_ref[...]).astype(BF16)

    @pl.when(group == G_Q)
    def _():
        qT_ref[...] = (_nt_dot(wt_ref[...], h_s[...]) * Q_PRESCALE).astype(BF16)

    @pl.when(group == G_QI)
    def _():
        qiT_ref[...] = _nt_dot(wt_ref[...], h_s[...]).astype(BF16)

    @pl.when(group == G_V)
    def _():
        out_t = _nt_dot(wt_ref[...], h_s[...]).astype(BF16)
        for j in range(TM_IN // KC):
            vT_ref[j] = out_t[:, j * KC:(j + 1) * KC]


def _inproj(x2d, g, wt, bsz, seq):
    m = x2d.shape[0]
    tiles_per_seq = seq // TM_IN
    n_main = wt.shape[0] - SMALL_W
    n_steps = n_main // TN_IN
    assert n_main % TN_IN == 0 and n_main % SMALL_W == 0

    def half(j, group):
        return jnp.clip(j - group * IN_TILES_PER_GROUP, 0, IN_TILES_PER_GROUP - 1)

    return pl.pallas_call(
        _inproj_kernel,
        grid=(m // TM_IN, n_steps),
        in_specs=[
            pl.BlockSpec((TM_IN, D_MODEL), lambda i, j: (i, 0)),
            pl.BlockSpec((1, D_MODEL), lambda i, j: (0, 0)),
            pl.BlockSpec((TN_IN, D_MODEL), lambda i, j: (j, 0)),
            pl.BlockSpec((SMALL_W, D_MODEL), lambda i, j: (n_main // SMALL_W, 0)),
        ],
        out_specs=[
            pl.BlockSpec((TM_IN, TN_IN), lambda i, j: (i, half(j, G_POOL))),
            pl.BlockSpec((TM_IN, TN_IN), lambda i, j: (i, half(j, G_K))),
            pl.BlockSpec((None, TN_IN, TM_IN),
                         lambda i, j: (i // tiles_per_seq, half(j, G_Q), i % tiles_per_seq)),
            pl.BlockSpec((None, TN_IN, TM_IN),
                         lambda i, j: (i // tiles_per_seq, half(j, G_QI), i % tiles_per_seq)),
            pl.BlockSpec((None, TM_IN // KC, TN_IN, KC),
                         lambda i, j: (i // tiles_per_seq, i % tiles_per_seq, half(j, G_V), 0)),
            pl.BlockSpec((TM_IN, SMALL_W), lambda i, j: (i, 0)),
        ],
        out_shape=[
            jax.ShapeDtypeStruct((m, POOL_WIDTH), F32),
            jax.ShapeDtypeStruct((m, ATTN_WIDTH), BF16),
            jax.ShapeDtypeStruct((bsz, ATTN_WIDTH, seq), BF16),
            jax.ShapeDtypeStruct((bsz, IDX_HEADS * IDX_DIM, seq), BF16),
            jax.ShapeDtypeStruct((bsz, seq // KC, ATTN_WIDTH, KC), BF16),
            jax.ShapeDtypeStruct((m, SMALL_W), F32),
        ],
        scratch_shapes=[pltpu.VMEM((TM_IN, D_MODEL), BF16)],
        compiler_params=pltpu.CompilerParams(
            dimension_semantics=("arbitrary", "arbitrary"), vmem_limit_bytes=VMEM_LIMIT),
        name="inproj",
    )(x2d, g, wt, wt)


def _dsa_kernel(rel_ref, qT_ref, qiT_ref, wT_ref, qiTn_ref, wTn_ref, kidx_ref, k_ref, vT_ref, o_ref,
                hi_s, lo_s, madd_s, bias_s, mx_s, alpha_s, acc_s, s_s, *, topk, n_q):
    b = pl.program_id(0)
    i = pl.program_id(1)
    n_ch = i + 1

    @pl.when((b == 0) & (i == 0))
    def _():
        r = lax.broadcasted_iota(I32, (KC, QB), 0)
        c = lax.broadcasted_iota(I32, (KC, QB), 1)
        for off in range(2):
            dist = jnp.maximum(c - r + off * QB, 0)

            def head_body(h, carry, dist=dist, off=off):
                t = jnp.full((KC, QB), rel_ref[0, h], F32)
                for bk in range(1, N_BUCKETS):
                    t = jnp.where(dist >= BUCKET_LO[bk], rel_ref[bk, h], t)
                bias_s[off, h] = (t - rel_ref[N_BUCKETS - 1, h]) * LOG2E
                return carry

            lax.fori_loop(0, ATTN_HEADS, head_body, 0)

    idx_scale = (IDX_HEADS ** -0.5) * (IDX_DIM ** -0.5)
    row = lax.broadcasted_iota(I32, (KC, QB), 0)
    col = lax.broadcasted_iota(I32, (KC, QB), 1)

    def index_chunk(m, qi_ref, w_scaled, q_block):
        kc = kidx_ref[pl.ds(pl.multiple_of(m * KC, KC), KC), :]
        acc = jnp.zeros((KC, QB), F32)
        for h in range(IDX_HEADS):
            d = jnp.dot(kc, qi_ref[h * IDX_DIM:(h + 1) * IDX_DIM, :],
                        preferred_element_type=F32)
            acc = acc + w_scaled[h:h + 1, :] * jnp.maximum(d, 0.0)
        bits = pltpu.bitcast(acc, I32)
        key = jnp.where(bits >= 0, bits, bits ^ 0x7FFFFFFF)
        causal = (m * KC + row) <= (q_block * QB + col)
        key = jnp.where(causal, key, INT_MIN)
        hi_s[m] = (key >> 16).astype(I16)
        lo_s[m] = ((key & 0xFFFF) + I16_MIN).astype(I16)

    @pl.when(i == 0)
    def _():
        index_chunk(0, qiT_ref, wT_ref[...] * idx_scale, 0)

    def count_ge(ref, cand):
        cand16 = cand.astype(I16)

        def cnt_body(m, accs):
            ge = (ref[m] >= cand16).astype(I16)
            accs = list(accs)
            for j in range(KC // PACKED_SUBLANES):
                a = j % len(accs)
                accs[a] = accs[a] + ge[j * PACKED_SUBLANES:(j + 1) * PACKED_SUBLANES, :]
            return tuple(accs)

        zero = jnp.zeros((PACKED_SUBLANES, QB), I16)
        accs = lax.fori_loop(0, n_ch, cnt_body, (zero,) * 4)
        tot = (accs[0] + accs[1]) + (accs[2] + accs[3])
        return jnp.sum(tot.astype(I32), axis=0, keepdims=True)

    def descend(ref, want):
        def bit_body(it, prefix):
            cand = prefix + jnp.left_shift(jnp.int32(1), 15 - it)
            return jnp.where(count_ge(ref, cand) >= want, cand, prefix)

        return lax.fori_loop(0, 16, bit_body, jnp.full((1, QB), I16_MIN, I32))

    p_hi = descend(hi_s, topk)
    n_above = jnp.where(p_hi >= I16_MAX, 0, count_ge(hi_s, jnp.minimum(p_hi + 1, I16_MAX)))
    hi_eq = jnp.where(p_hi > I16_MIN, p_hi, I16_MAX).astype(I16)

    def member_body(m, carry):
        lo_s[m] = jnp.where(hi_s[m] == hi_eq, lo_s[m], I16_MIN)
        return carry

    lax.fori_loop(0, n_ch, member_body, 0)
    p_lo = descend(lo_s, topk - n_above)
    p_hi16 = p_hi.astype(I16)
    p_lo16 = p_lo.astype(I16)

    def packed_count(acc, flags):
        f16 = flags.astype(I16)
        for j in range(KC // PACKED_SUBLANES):
            acc = acc + f16[j * PACKED_SUBLANES:(j + 1) * PACKED_SUBLANES, :]
        return acc

    def total(acc16):
        return jnp.sum(acc16.astype(I32), axis=0, keepdims=True)

    zero16 = jnp.zeros((PACKED_SUBLANES, QB), I16)

    def mask_body(m, kept):
        hi = hi_s[m]
        keep = (hi > p_hi16) | ((hi == hi_eq) & (lo_s[m] >= p_lo16))
        madd_s[m] = jnp.where(keep, jnp.bfloat16(0.0), jnp.bfloat16(NEG_BIG)).astype(F32)
        return packed_count(kept, keep)

    excess = total(lax.fori_loop(0, n_ch, mask_body, zero16)) - topk

    @pl.when(jnp.max(excess) > 0)
    def _():
        def tied(m):
            return (hi_s[m] == hi_eq) & (lo_s[m] == p_lo16)

        n_tied = total(lax.fori_loop(0, n_ch, lambda m, acc: packed_count(acc, tied(m)), zero16))
        need = (n_tied - excess).astype(F32)
        r = lax.broadcasted_iota(I32, (KC, KC), 0)
        c = lax.broadcasted_iota(I32, (KC, KC), 1)
        prefix_op = jnp.where(c <= r, 1.0, 0.0).astype(BF16)

        def fix_body(m, seen):
            t = jnp.where(tied(m), jnp.bfloat16(1.0), jnp.bfloat16(0.0))
            rank = seen + jnp.dot(prefix_op, t, preferred_element_type=F32)
            drop = (t.astype(F32) > 0.0) & (rank > need)
            madd_s[m] = jnp.where(drop, NEG_BIG, madd_s[m])
            return rank[KC - 1:KC, :]

        lax.fori_loop(0, n_ch, fix_body, jnp.zeros((1, QB), F32))

    n_far = jnp.maximum(i - 1, 0)
    mx_s[...] = jnp.full(mx_s.shape, -jnp.inf, F32)
    acc_s[...] = jnp.zeros(acc_s.shape, F32)

    w_next = wTn_ref[...] * idx_scale

    def att_body(m, carry, near):
        index_chunk(m, qiTn_ref, w_next, i + 1)
        madd = madd_s[m]
        rows = pl.ds(pl.multiple_of(m * KC, KC), KC)
        for h in range(ATTN_HEADS):
            hd = slice(h * HEAD_DIM, (h + 1) * HEAD_DIM)
            s = jnp.dot(k_ref[rows, hd], qT_ref[hd, :], preferred_element_type=F32)
            if near:
                s = s + (madd + bias_s[i - m, h])
            else:
                s = s + madd
            s_s[h] = s
            mx = mx_s[h]
            m_new = jnp.maximum(mx, jnp.max(s, axis=0, keepdims=True))
            alpha_s[h] = jnp.exp2(mx - m_new)
            mx_s[h] = m_new
        ones_rows = jnp.ones((PACKED_SUBLANES, KC), BF16)
        for h in range(ATTN_HEADS):
            hd = slice(h * HEAD_DIM, (h + 1) * HEAD_DIM)
            p = jnp.exp2(s_s[h] - mx_s[h])
            v_aug = jnp.concatenate([vT_ref[m, hd, :], ones_rows], axis=0)
            pv = jnp.dot(v_aug, p.astype(BF16), preferred_element_type=F32)
            acc_s[h] = alpha_s[h] * acc_s[h] + pv
        return carry

    lax.fori_loop(0, n_far, functools.partial(att_body, near=False), 0)
    lax.fori_loop(n_far, n_ch, functools.partial(att_body, near=True), 0)

    @pl.when(i + 1 < n_q)
    def _():
        index_chunk(i + 1, qiTn_ref, w_next, i + 1)

    for h in range(ATTN_HEADS):
        acc = acc_s[h]
        out_t = acc[:HEAD_DIM] / acc[HEAD_DIM:HEAD_DIM + 1]
        o_ref[:, h * HEAD_DIM:(h + 1) * HEAD_DIM] = out_t.T.astype(o_ref.dtype)


def _dsa(rel_bias, qT, qiT, wT, kidx, k, vT4, seq):
    bsz = qT.shape[0]
    topk = min(TOPK_MAX, seq // 4)
    n_q = seq // QB
    kernel = functools.partial(_dsa_kernel, topk=topk, n_q=n_q)
    nxt = lambda i: jnp.minimum(i + 1, n_q - 1)
    return pl.pallas_call(
        kernel,
        grid=(bsz, n_q),
        in_specs=[
            pl.BlockSpec(memory_space=pltpu.SMEM),
            pl.BlockSpec((None, ATTN_WIDTH, QB), lambda b, i: (b, 0, i)),
            pl.BlockSpec((None, IDX_HEADS * IDX_DIM, QB), lambda b, i: (b, 0, i)),
            pl.BlockSpec((None, IDX_HEADS, QB), lambda b, i: (b, 0, i)),
            pl.BlockSpec((None, IDX_HEADS * IDX_DIM, QB), lambda b, i: (b, 0, nxt(i))),
            pl.BlockSpec((None, IDX_HEADS, QB), lambda b, i: (b, 0, nxt(i))),
            pl.BlockSpec((None, seq, IDX_DIM), lambda b, i: (b, 0, 0)),
            pl.BlockSpec((None, seq, ATTN_WIDTH), lambda b, i: (b, 0, 0)),
            pl.BlockSpec((None, seq // KC, ATTN_WIDTH, KC), lambda b, i: (b, 0, 0, 0)),
        ],
        out_specs=pl.BlockSpec((None, QB, ATTN_WIDTH), lambda b, i: (b, i, 0)),
        out_shape=jax.ShapeDtypeStruct((bsz, seq, ATTN_WIDTH), BF16),
        scratch_shapes=[
            pltpu.VMEM((seq // KC, KC, QB), I16),
            pltpu.VMEM((seq // KC, KC, QB), I16),
            pltpu.VMEM((seq // KC, KC, QB), F32),
            pltpu.VMEM((2, ATTN_HEADS, KC, QB), F32),
            pltpu.VMEM((ATTN_HEADS, 1, QB), F32),
            pltpu.VMEM((ATTN_HEADS, 1, QB), F32),
            pltpu.VMEM((ATTN_HEADS, HEAD_DIM + PACKED_SUBLANES, QB), F32),
            pltpu.VMEM((ATTN_HEADS, KC, QB), F32),
        ],
        compiler_params=pltpu.CompilerParams(
            dimension_semantics=("arbitrary", "arbitrary"), vmem_limit_bytes=VMEM_LIMIT),
        name="dsa",
    )(rel_bias, qT, qiT, wT, qiT, wT, kidx, k, vT4)


def _mix_kernel(x_ref, u_ref, halo_ref, attn_ref, wpool_ref, pscale_ref, wout_ref, g_ref,
                x1_ref, h2_ref, ext_s, *, seq):
    m = pl.program_id(0)
    tiles_per_seq = seq // TM_MIX
    first = (m % tiles_per_seq) == 0
    ext_s[0:POOL_HALO, :] = jnp.where(first, 0.0, halo_ref[...])
    ext_s[POOL_HALO:, :] = u_ref[...]

    t_pos = (m % tiles_per_seq) * TM_MIX + lax.broadcasted_iota(I32, (TM_MIX, POOL_GC), 0)
    acc = x_ref[...] + jnp.dot(attn_ref[...], wout_ref[POOL_WIDTH:, :], preferred_element_type=F32)
    for g, w in enumerate(POOL_WINDOWS):
        cols = slice(g * POOL_GC, (g + 1) * POOL_GC)
        u = ext_s[POOL_HALO:, cols]
        wsum = u
        for j in range(1, w):
            wsum = wsum + ext_s[pl.ds(POOL_HALO - j, TM_MIX), cols]
        count = jnp.minimum(t_pos + 1, w).astype(F32)
        pooled = (wsum / count - u).astype(BF16)
        y = jnp.dot(pooled, wpool_ref[g], preferred_element_type=F32) * pscale_ref[:, cols]
        acc = acc + jnp.dot(y.astype(BF16), wout_ref[cols, :], preferred_element_type=F32)
    x1_ref[...] = acc
    h2_ref[...] = _rms(acc, g_ref[...]).astype(BF16)


def _mix(x2d, u_pool, attn2d, w_pool, pool_scale, w_out, g_ffn, seq):
    m = x2d.shape[0]
    halo_per_tile = TM_MIX // POOL_HALO
    kernel = functools.partial(_mix_kernel, seq=seq)
    return pl.pallas_call(
        kernel,
        grid=(m // TM_MIX,),
        in_specs=[
            pl.BlockSpec((TM_MIX, D_MODEL), lambda i: (i, 0)),
            pl.BlockSpec((TM_MIX, POOL_WIDTH), lambda i: (i, 0)),
            pl.BlockSpec((POOL_HALO, POOL_WIDTH), lambda i: (jnp.maximum(i * halo_per_tile - 1, 0), 0)),
            pl.BlockSpec((TM_MIX, ATTN_WIDTH), lambda i: (i, 0)),
            pl.BlockSpec((POOL_GROUPS, POOL_GC, POOL_GC), lambda i: (0, 0, 0)),
            pl.BlockSpec((1, POOL_WIDTH), lambda i: (0, 0)),
            pl.BlockSpec((D_MODEL, D_MODEL), lambda i: (0, 0)),
            pl.BlockSpec((1, D_MODEL), lambda i: (0, 0)),
        ],
        out_specs=[
            pl.BlockSpec((TM_MIX, D_MODEL), lambda i: (i, 0)),
            pl.BlockSpec((TM_MIX, D_MODEL), lambda i: (i, 0)),
        ],
        out_shape=[
            jax.ShapeDtypeStruct((m, D_MODEL), F32),
            jax.ShapeDtypeStruct((m, D_MODEL), BF16),
        ],
        scratch_shapes=[pltpu.VMEM((TM_MIX + POOL_HALO, POOL_WIDTH), F32)],
        compiler_params=pltpu.CompilerParams(
            dimension_semantics=("arbitrary",), vmem_limit_bytes=VMEM_LIMIT),
        name="mix",
    )(x2d, u_pool, u_pool, attn2d, w_pool, pool_scale, w_out, g_ffn)


def _ffn_kernel(h_ref, x1_ref, wg_ref, wv_ref, cwg_ref, cwv_ref, cbg_ref, cbv_ref, wd_ref,
                o_ref, carry_s, *u_bufs, seq):
    m = pl.program_id(0)
    f = pl.program_id(1)
    tiles_per_seq = seq // TM_FFN
    first = (m % tiles_per_seq) == 0

    @pl.when(f == 0)
    def _():
        o_ref[...] = x1_ref[...]

    h = h_ref[...]
    n_half = TF_FFN // TF_HALF

    def up(s):
        cols = slice(s * TF_HALF, (s + 1) * TF_HALF)
        for k, w_ref in enumerate((wg_ref, wv_ref)):
            u = u_bufs[2 * s + k]
            u[0:SUBLANES, :] = jnp.where(first, 0.0, carry_s[f, 2 * s + k])
            u[SUBLANES:, :] = jnp.dot(h, w_ref[:, cols], preferred_element_type=F32)
            carry_s[f, 2 * s + k] = u[TM_FFN:, :]

    def conv(u, cols, cw_ref, cb_ref):
        out = cb_ref[:, cols]
        for j in range(CONV_WIDTH):
            shift = CONV_WIDTH - 1 - j
            out = out + u[pl.ds(SUBLANES - shift, TM_FFN), :] * cw_ref[j:j + 1, cols]
        return out

    def gated(s):
        cols = slice(s * TF_HALF, (s + 1) * TF_HALF)
        gate = conv(u_bufs[2 * s], cols, cwg_ref, cbg_ref)
        val = conv(u_bufs[2 * s + 1], cols, cwv_ref, cbv_ref)
        return (gate * jax.nn.sigmoid(gate) * val).astype(BF16)

    def down(s, act):
        return jnp.dot(act, wd_ref[s * TF_HALF:(s + 1) * TF_HALF, :], preferred_element_type=F32)

    up(0)
    act0 = gated(0)
    up(1)
    contrib = down(0, act0)
    act1 = gated(1)
    contrib = contrib + down(1, act1)
    o_ref[...] += contrib


def _ffn(h2, x1, w_up, conv_w, conv_b, w_down, seq):
    m = h2.shape[0]
    n_f = D_FF // TF_FFN
    kernel = functools.partial(_ffn_kernel, seq=seq)
    return pl.pallas_call(
        kernel,
        grid=(m // TM_FFN, n_f),
        in_specs=[
            pl.BlockSpec((TM_FFN, D_MODEL), lambda i, f: (i, 0)),
            pl.BlockSpec((TM_FFN, D_MODEL), lambda i, f: (i, 0)),
            pl.BlockSpec((D_MODEL, TF_FFN), lambda i, f: (0, f)),
            pl.BlockSpec((D_MODEL, TF_FFN), lambda i, f: (0, f + n_f)),
            pl.BlockSpec((CONV_WIDTH, TF_FFN), lambda i, f: (0, f)),
            pl.BlockSpec((CONV_WIDTH, TF_FFN), lambda i, f: (0, f + n_f)),
            pl.BlockSpec((1, TF_FFN), lambda i, f: (0, f)),
            pl.BlockSpec((1, TF_FFN), lambda i, f: (0, f + n_f)),
            pl.BlockSpec((TF_FFN, D_MODEL), lambda i, f: (f, 0)),
        ],
        out_specs=pl.BlockSpec((TM_FFN, D_MODEL), lambda i, f: (i, 0)),
        out_shape=jax.ShapeDtypeStruct((m, D_MODEL), F32),
        scratch_shapes=[pltpu.VMEM((n_f, 2 * (TF_FFN // TF_HALF), SUBLANES, TF_HALF), F32)]
        + [pltpu.VMEM((TM_FFN + SUBLANES, TF_HALF), F32)] * (2 * (TF_FFN // TF_HALF)),
        compiler_params=pltpu.CompilerParams(
            dimension_semantics=("arbitrary", "arbitrary"), vmem_limit_bytes=VMEM_LIMIT),
        name="ffn",
    )(h2, x1, w_up, w_up, conv_w, conv_w, conv_b, conv_b, w_down)


def _ple_kernel(x_ref, p_ref, gp_ref, wgate32_ref, wproj_ref, gf_ref, o_ref, wgate_ref, *, last_layer):
    @pl.when(pl.program_id(0) == 0)
    def _():
        for r in range(0, D_MODEL, CAST_ROWS):
            wgate_ref[r:r + CAST_ROWS, :] = wgate32_ref[r:r + CAST_ROWS, :].astype(BF16)

    x = x_ref[...]
    hg = _rms(x, gp_ref[...]).astype(BF16)
    gate = jax.nn.sigmoid(jnp.dot(hg, wgate_ref[...], preferred_element_type=F32))
    emb = jnp.dot(p_ref[...].astype(BF16), wproj_ref[...], preferred_element_type=F32)
    x3 = x + emb * gate
    o_ref[...] = _rms(x3, gf_ref[...]) if last_layer else x3


def _ple(x2, p2d, g_ple, w_gate, w_proj, g_final, last_layer):
    m = x2.shape[0]
    return pl.pallas_call(
        functools.partial(_ple_kernel, last_layer=last_layer),
        grid=(m // TM_PLE,),
        in_specs=[
            pl.BlockSpec((TM_PLE, D_MODEL), lambda i: (i, 0)),
            pl.BlockSpec((TM_PLE, D_PLE), lambda i: (i, 0)),
            pl.BlockSpec((1, D_MODEL), lambda i: (0, 0)),
            pl.BlockSpec((D_MODEL, D_MODEL), lambda i: (0, 0), pipeline_mode=pl.Buffered(1)),
            pl.BlockSpec((D_PLE, D_MODEL), lambda i: (0, 0)),
            pl.BlockSpec((1, D_MODEL), lambda i: (0, 0)),
        ],
        out_specs=pl.BlockSpec((TM_PLE, D_MODEL), lambda i: (i, 0)),
        out_shape=jax.ShapeDtypeStruct((m, D_MODEL), F32),
        scratch_shapes=[pltpu.VMEM((D_MODEL, D_MODEL), BF16)],
        compiler_params=pltpu.CompilerParams(
            dimension_semantics=("arbitrary",), vmem_limit_bytes=VMEM_LIMIT),
        name="ple",
    )(x2, p2d, g_ple, w_gate, w_proj, g_final)


def kernel(x, p, g_mix, w_in, w_pool, pool_scale, rel_bias, w_out, g_ffn, w_up, conv_w, conv_b,
           w_down, g_ple, w_ple_gate, w_ple_proj, g_final):
    bsz, seq, _ = x.shape
    depth = w_in.shape[0]
    assert seq % QB == 0 and seq % TM_MIX == 0 and seq % TM_FFN == 0 and (bsz * seq) % TM_IN == 0
    x2d = x.reshape(bsz * seq, D_MODEL)
    for i in range(depth):
        u_pool, k2d, qT, qiT, vT4, small = _inproj(x2d, g_mix[i].reshape(1, D_MODEL),
                                                   w_in[i].T.astype(BF16), bsz, seq)
        small3 = small.reshape(bsz, seq, SMALL_W)
        kidx = small3[:, :, :IDX_DIM].astype(BF16)
        wT = jnp.swapaxes(small3[:, :, IDX_DIM:IDX_DIM + IDX_HEADS], 1, 2)
        attn = _dsa(rel_bias, qT, qiT, wT, kidx, k2d.reshape(bsz, seq, ATTN_WIDTH), vT4, seq)

        x1, h2 = _mix(x2d, u_pool, attn.reshape(bsz * seq, ATTN_WIDTH), w_pool[i].astype(BF16),
                      pool_scale[i].reshape(1, POOL_WIDTH), w_out[i].astype(BF16),
                      g_ffn[i].reshape(1, D_MODEL), seq)
        x2 = _ffn(h2, x1, w_up[i].astype(BF16), conv_w[i], conv_b[i].reshape(1, 2 * D_FF),
                  w_down[i].astype(BF16), seq)
        x2d = _ple(x2, p[i].reshape(bsz * seq, D_PLE), g_ple[i].reshape(1, D_MODEL),
                   w_ple_gate[i], w_ple_proj[i].astype(BF16),
                   g_final.reshape(1, D_MODEL), last_layer=(i == depth - 1))
    return x2d.reshape(bsz, seq, D_MODEL)
```

```python
import functools
import math

import numpy as np
import jax
import jax.numpy as jnp
from jax import lax
from jax.experimental import pallas as pl
from jax.experimental.pallas import tpu as pltpu

F32 = jnp.float32
BF16 = jnp.bfloat16
I32 = jnp.int32
I16 = jnp.int16

D_MODEL = 2048
D_PLE = 256
POOL_WIDTH = 1024
POOL_GROUPS = 4
POOL_GC = 256
POOL_WINDOWS = (2, 4, 8, 16)
ATTN_HEADS = 8
HEAD_DIM = 128
ATTN_WIDTH = 1024
IDX_HEADS = 16
IDX_DIM = 64
TOPK_MAX = 256
N_BUCKETS = 32
MAX_DISTANCE = 128
D_FF = 5632
CONV_WIDTH = 3
EPS = 1e-6
IN_SIZES = (POOL_WIDTH, ATTN_WIDTH, ATTN_WIDTH, ATTN_WIDTH, IDX_HEADS * IDX_DIM, IDX_DIM, IDX_HEADS)
G_POOL, G_Q, G_K, G_V, G_QI = range(5)

V7X_VMEM_BYTES = 64 * 1024 * 1024
SUBLANES = 8
PACKED_SUBLANES = 16
LANES = 128
VMEM_LIMIT = V7X_VMEM_BYTES - 8 * 1024 * 1024

TM_IN = 1024
TN_IN = 512
IN_TILES_PER_GROUP = 1024 // TN_IN
SMALL_W = IDX_DIM + IDX_HEADS
QB = 256
KC = 256
TM_MIX = 512
POOL_HALO = 16
TM_FFN = 1024
TF_FFN = 512
TF_HALF = 256
TM_PLE = 512
CAST_ROWS = 256

INT_MIN = -2 ** 31
I16_MIN = -2 ** 15
I16_MAX = 2 ** 15 - 1
NEG_BIG = -1e30
LOG2E = math.log2(math.e)
Q_PRESCALE = (HEAD_DIM ** -0.5) * LOG2E


def _bucket_lower_bounds():
    n = np.arange(0, 4 * MAX_DISTANCE, dtype=np.int64)
    max_exact = N_BUCKETS // 2
    nf = np.maximum(n, 1).astype(np.float32)
    large = max_exact + (np.log(nf / np.float32(max_exact)) / np.float32(math.log(MAX_DISTANCE / max_exact))
                         * np.float32(N_BUCKETS - max_exact)).astype(np.int32)
    large = np.minimum(large, N_BUCKETS - 1)
    bucket = np.where(n < max_exact, n, large)
    assert np.all(np.diff(bucket) >= 0)
    lows = [int(np.argmax(bucket == b)) for b in range(N_BUCKETS)]
    assert all(bucket[lo] == b for b, lo in enumerate(lows))
    return lows


BUCKET_LO = _bucket_lower_bounds()


def _rms(x, g):
    ms = jnp.mean(x * x, axis=-1, keepdims=True)
    return (x * lax.rsqrt(ms + EPS)) * g


def _nt_dot(a, b):
    return lax.dot_general(a, b, (((1,), (1,)), ((), ())), preferred_element_type=F32)


def _inproj_kernel(x_ref, g_ref, wt_ref, wst_ref, pool_ref, k_ref, qT_ref, qiT_ref, vT_ref, small_ref, h_s):
    n = pl.program_id(1)
    group = n // IN_TILES_PER_GROUP

    @pl.when(n == 0)
    def _():
        hb = _rms(x_ref[...], g_ref[...]).astype(BF16)
        h_s[...] = hb
        small_ref[...] = _nt_dot(hb, wst_ref[...])

    @pl.when(group == G_POOL)
    def _():
        pool_ref[...] = _nt_dot(h_s[...], wt_ref[...])

    @pl.when(group == G_K)
    def _():
        k_ref[...] = _nt_dot(h_s[...], wt_ref[...]).astype(BF16)

    @pl.when(group == G_Q)
    def _():
        qT_ref[...] = (_nt_dot(wt_ref[...], h_s[...]) * Q_PRESCALE).astype(BF16)

    @pl.when(group == G_QI)
    def _():
        qiT_ref[...] = _nt_dot(wt_ref[...], h_s[...]).astype(BF16)

    @pl.when(group == G_V)
    def _():
        out_t = _nt_dot(wt_ref[...], h_s[...]).astype(BF16)
        for j in range(TM_IN // KC):
            vT_ref[j] = out_t[:, j * KC:(j + 1) * KC]


def _inproj(x2d, g, wt, bsz, seq):
    m = x2d.shape[0]
    tiles_per_seq = seq // TM_IN
    n_main = wt.shape[0] - SMALL_W
    n_steps = n_main // TN_IN
    assert n_main % TN_IN == 0 and n_main % SMALL_W == 0

    def half(j, group):
        return jnp.clip(j - group * IN_TILES_PER_GROUP, 0, IN_TILES_PER_GROUP - 1)

    return pl.pallas_call(
        _inproj_kernel,
        grid=(m // TM_IN, n_steps),
        in_specs=[
            pl.BlockSpec((TM_IN, D_MODEL), lambda i, j: (i, 0)),
            pl.BlockSpec((1, D_MODEL), lambda i, j: (0, 0)),
            pl.BlockSpec((TN_IN, D_MODEL), lambda i, j: (j, 0)),
            pl.BlockSpec((SMALL_W, D_MODEL), lambda i, j: (n_main // SMALL_W, 0)),
        ],
        out_specs=[
            pl.BlockSpec((TM_IN, TN_IN), lambda i, j: (i, half(j, G_POOL))),
            pl.BlockSpec((TM_IN, TN_IN), lambda i, j: (i, half(j, G_K))),
            pl.BlockSpec((None, TN_IN, TM_IN),
                         lambda i, j: (i // tiles_per_seq, half(j, G_Q), i % tiles_per_seq)),
            pl.BlockSpec((None, TN_IN, TM_IN),
                         lambda i, j: (i // tiles_per_seq, half(j, G_QI), i % tiles_per_seq)),
            pl.BlockSpec((None, TM_IN // KC, TN_IN, KC),
                         lambda i, j: (i // tiles_per_seq, i % tiles_per_seq, half(j, G_V), 0)),
            pl.BlockSpec((TM_IN, SMALL_W), lambda i, j: (i, 0)),
        ],
        out_shape=[
            jax.ShapeDtypeStruct((m, POOL_WIDTH), F32),
            jax.ShapeDtypeStruct((m, ATTN_WIDTH), BF16),
            jax.ShapeDtypeStruct((bsz, ATTN_WIDTH, seq), BF16),
            jax.ShapeDtypeStruct((bsz, IDX_HEADS * IDX_DIM, seq), BF16),
            jax.ShapeDtypeStruct((bsz, seq // KC, ATTN_WIDTH, KC), BF16),
            jax.ShapeDtypeStruct((m, SMALL_W), F32),
        ],
        scratch_shapes=[pltpu.VMEM((TM_IN, D_MODEL), BF16)],
        compiler_params=pltpu.CompilerParams(
            dimension_semantics=("arbitrary", "arbitrary"), vmem_limit_bytes=VMEM_LIMIT),
        name="inproj",
    )(x2d, g, wt, wt)


def _dsa_kernel(rel_ref, qT_ref, qiT_ref, wT_ref, qiTn_ref, wTn_ref, kidx_ref, k_ref, vT_ref, o_ref,
                hi_s, lo_s, madd_s, bias_s, mx_s, alpha_s, acc_s, s_s, *, topk, n_q):
    b = pl.program_id(0)
    i = pl.program_id(1)
    n_ch = i + 1

    @pl.when((b == 0) & (i == 0))
    def _():
        r = lax.broadcasted_iota(I32, (KC, QB), 0)
        c = lax.broadcasted_iota(I32, (KC, QB), 1)
        for off in range(2):
            dist = jnp.maximum(c - r + off * QB, 0)

            def head_body(h, carry, dist=dist, off=off):
                t = jnp.full((KC, QB), rel_ref[0, h], F32)
                for bk in range(1, N_BUCKETS):
                    t = jnp.where(dist >= BUCKET_LO[bk], rel_ref[bk, h], t)
                bias_s[off, h] = (t - rel_ref[N_BUCKETS - 1, h]) * LOG2E
                return carry

            lax.fori_loop(0, ATTN_HEADS, head_body, 0)

    idx_scale = (IDX_HEADS ** -0.5) * (IDX_DIM ** -0.5)
    row = lax.broadcasted_iota(I32, (KC, QB), 0)
    col = lax.broadcasted_iota(I32, (KC, QB), 1)

    def index_chunk(m, qi_ref, w_scaled, q_block):
        kc = kidx_ref[pl.ds(pl.multiple_of(m * KC, KC), KC), :]
        acc = jnp.zeros((KC, QB), F32)
        for h in range(IDX_HEADS):
            d = jnp.dot(kc, qi_ref[h * IDX_DIM:(h + 1) * IDX_DIM, :],
                        preferred_element_type=F32)
            acc = acc + w_scaled[h:h + 1, :] * jnp.maximum(d, 0.0)
        bits = pltpu.bitcast(acc, I32)
        key = jnp.where(bits >= 0, bits, bits ^ 0x7FFFFFFF)
        causal = (m * KC + row) <= (q_block * QB + col)
        key = jnp.where(causal, key, INT_MIN)
        hi_s[m] = (key >> 16).astype(I16)
        lo_s[m] = ((key & 0xFFFF) + I16_MIN).astype(I16)

    @pl.when(i == 0)
    def _():
        index_chunk(0, qiT_ref, wT_ref[...] * idx_scale, 0)

    def count_ge(ref, cand):
        cand16 = cand.astype(I16)

        def cnt_body(m, accs):
            ge = (ref[m] >= cand16).astype(I16)
            accs = list(accs)
            for j in range(KC // PACKED_SUBLANES):
                a = j % len(accs)
                accs[a] = accs[a] + ge[j * PACKED_SUBLANES:(j + 1) * PACKED_SUBLANES, :]
            return tuple(accs)

        zero = jnp.zeros((PACKED_SUBLANES, QB), I16)
        accs = lax.fori_loop(0, n_ch, cnt_body, (zero,) * 4)
        tot = (accs[0] + accs[1]) + (accs[2] + accs[3])
        return jnp.sum(tot.astype(I32), axis=0, keepdims=True)

    def descend(ref, want):
        def bit_body(it, prefix):
            cand = prefix + jnp.left_shift(jnp.int32(1), 15 - it)
            return jnp.where(count_ge(ref, cand) >= want, cand, prefix)

        return lax.fori_loop(0, 16, bit_body, jnp.full((1, QB), I16_MIN, I32))

    p_hi = descend(hi_s, topk)
    n_above = jnp.where(p_hi >= I16_MAX, 0, count_ge(hi_s, jnp.minimum(p_hi + 1, I16_MAX)))
    hi_eq = jnp.where(p_hi > I16_MIN, p_hi, I16_MAX).astype(I16)

    def member_body(m, carry):
        lo_s[m] = jnp.where(hi_s[m] == hi_eq, lo_s[m], I16_MIN)
        return carry

    lax.fori_loop(0, n_ch, member_body, 0)
    p_lo = descend(lo_s, topk - n_above)
    p_hi16 = p_hi.astype(I16)
    p_lo16 = p_lo.astype(I16)

    def packed_count(acc, flags):
        f16 = flags.astype(I16)
        for j in range(KC // PACKED_SUBLANES):
            acc = acc + f16[j * PACKED_SUBLANES:(j + 1) * PACKED_SUBLANES, :]
        return acc

    def total(acc16):
        return jnp.sum(acc16.astype(I32), axis=0, keepdims=True)

    zero16 = jnp.zeros((PACKED_SUBLANES, QB), I16)

    def mask_body(m, kept):
        hi = hi_s[m]
        keep = (hi > p_hi16) | ((hi == hi_eq) & (lo_s[m] >= p_lo16))
        madd_s[m] = jnp.where(keep, jnp.bfloat16(0.0), jnp.bfloat16(NEG_BIG)).astype(F32)
        return packed_count(kept, keep)

    excess = total(lax.fori_loop(0, n_ch, mask_body, zero16)) - topk

    @pl.when(jnp.max(excess) > 0)
    def _():
        def tied(m):
            return (hi_s[m] == hi_eq) & (lo_s[m] == p_lo16)

        n_tied = total(lax.fori_loop(0, n_ch, lambda m, acc: packed_count(acc, tied(m)), zero16))
        need = (n_tied - excess).astype(F32)
        r = lax.broadcasted_iota(I32, (KC, KC), 0)
        c = lax.broadcasted_iota(I32, (KC, KC), 1)
        prefix_op = jnp.where(c <= r, 1.0, 0.0).astype(BF16)

        def fix_body(m, seen):
            t = jnp.where(tied(m), jnp.bfloat16(1.0), jnp.bfloat16(0.0))
            rank = seen + jnp.dot(prefix_op, t, preferred_element_type=F32)
            drop = (t.astype(F32) > 0.0) & (rank > need)
            madd_s[m] = jnp.where(drop, NEG_BIG, madd_s[m])
            return rank[KC - 1:KC, :]

        lax.fori_loop(0, n_ch, fix_body, jnp.zeros((1, QB), F32))

    n_far = jnp.maximum(i - 1, 0)
    mx_s[...] = jnp.full(mx_s.shape, -jnp.inf, F32)
    acc_s[...] = jnp.zeros(acc_s.shape, F32)

    w_next = wTn_ref[...] * idx_scale

    def att_body(m, carry, near):
        index_chunk(m, qiTn_ref, w_next, i + 1)
        madd = madd_s[m]
        rows = pl.ds(pl.multiple_of(m * KC, KC), KC)
        for h in range(ATTN_HEADS):
            hd = slice(h * HEAD_DIM, (h + 1) * HEAD_DIM)
            s = jnp.dot(k_ref[rows, hd], qT_ref[hd, :], preferred_element_type=F32)
            if near:
                s = s + (madd + bias_s[i - m, h])
            else:
                s = s + madd
            s_s[h] = s
            mx = mx_s[h]
            m_new = jnp.maximum(mx, jnp.max(s, axis=0, keepdims=True))
            alpha_s[h] = jnp.exp2(mx - m_new)
            mx_s[h] = m_new
        ones_rows = jnp.ones((PACKED_SUBLANES, KC), BF16)
        for h in range(ATTN_HEADS):
            hd = slice(h * HEAD_DIM, (h + 1) * HEAD_DIM)
            p = jnp.exp2(s_s[h] - mx_s[h])
            v_aug = jnp.concatenate([vT_ref[m, hd, :], ones_rows], axis=0)
            pv = jnp.dot(v_aug, p.astype(BF16), preferred_element_type=F32)
            acc_s[h] = alpha_s[h] * acc_s[h] + pv
        return carry

    lax.fori_loop(0, n_far, functools.partial(att_body, near=False), 0)
    lax.fori_loop(n_far, n_ch, functools.partial(att_body, near=True), 0)

    @pl.when(i + 1 < n_q)
    def _():
        index_chunk(i + 1, qiTn_ref, w_next, i + 1)

    for h in range(ATTN_HEADS):
        acc = acc_s[h]
        out_t = acc[:HEAD_DIM] / acc[HEAD_DIM:HEAD_DIM + 1]
        o_ref[:, h * HEAD_DIM:(h + 1) * HEAD_DIM] = out_t.T.astype(o_ref.dtype)


def _dsa(rel_bias, qT, qiT, wT, kidx, k, vT4, seq):
    bsz = qT.shape[0]
    topk = min(TOPK_MAX, seq // 4)
    n_q = seq // QB
    kernel = functools.partial(_dsa_kernel, topk=topk, n_q=n_q)
    nxt = lambda i: jnp.minimum(i + 1, n_q - 1)
    return pl.pallas_call(
        kernel,
        grid=(bsz, n_q),
        in_specs=[
            pl.BlockSpec(memory_space=pltpu.SMEM),
            pl.BlockSpec((None, ATTN_WIDTH, QB), lambda b, i: (b, 0, i)),
            pl.BlockSpec((None, IDX_HEADS * IDX_DIM, QB), lambda b, i: (b, 0, i)),
            pl.BlockSpec((None, IDX_HEADS, QB), lambda b, i: (b, 0, i)),
            pl.BlockSpec((None, IDX_HEADS * IDX_DIM, QB), lambda b, i: (b, 0, nxt(i))),
            pl.BlockSpec((None, IDX_HEADS, QB), lambda b, i: (b, 0, nxt(i))),
            pl.BlockSpec((None, seq, IDX_DIM), lambda b, i: (b, 0, 0)),
            pl.BlockSpec((None, seq, ATTN_WIDTH), lambda b, i: (b, 0, 0)),
            pl.BlockSpec((None, seq // KC, ATTN_WIDTH, KC), lambda b, i: (b, 0, 0, 0)),
        ],
        out_specs=pl.BlockSpec((None, QB, ATTN_WIDTH), lambda b, i: (b, i, 0)),
        out_shape=jax.ShapeDtypeStruct((bsz, seq, ATTN_WIDTH), BF16),
        scratch_shapes=[
            pltpu.VMEM((seq // KC, KC, QB), I16),
            pltpu.VMEM((seq // KC, KC, QB), I16),
            pltpu.VMEM((seq // KC, KC, QB), F32),
            pltpu.VMEM((2, ATTN_HEADS, KC, QB), F32),
            pltpu.VMEM((ATTN_HEADS, 1, QB), F32),
            pltpu.VMEM((ATTN_HEADS, 1, QB), F32),
            pltpu.VMEM((ATTN_HEADS, HEAD_DIM + PACKED_SUBLANES, QB), F32),
            pltpu.VMEM((ATTN_HEADS, KC, QB), F32),
        ],
        compiler_params=pltpu.CompilerParams(
            dimension_semantics=("arbitrary", "arbitrary"), vmem_limit_bytes=VMEM_LIMIT),
        name="dsa",
    )(rel_bias, qT, qiT, wT, qiT, wT, kidx, k, vT4)


def _mix_kernel(x_ref, u_ref, halo_ref, attn_ref, wpool_ref, pscale_ref, wout_ref, g_ref,
                x1_ref, h2_ref, ext_s, *, seq):
    m = pl.program_id(0)
    tiles_per_seq = seq // TM_MIX
    first = (m % tiles_per_seq) == 0
    ext_s[0:POOL_HALO, :] = jnp.where(first, 0.0, halo_ref[...])
    ext_s[POOL_HALO:, :] = u_ref[...]

    t_pos = (m % tiles_per_seq) * TM_MIX + lax.broadcasted_iota(I32, (TM_MIX, POOL_GC), 0)
    acc = x_ref[...] + jnp.dot(attn_ref[...], wout_ref[POOL_WIDTH:, :], preferred_element_type=F32)
    for g, w in enumerate(POOL_WINDOWS):
        cols = slice(g * POOL_GC, (g + 1) * POOL_GC)
        u = ext_s[POOL_HALO:, cols]
        wsum = u
        for j in range(1, w):
            wsum = wsum + ext_s[pl.ds(POOL_HALO - j, TM_MIX), cols]
        count = jnp.minimum(t_pos + 1, w).astype(F32)
        pooled = (wsum / count - u).astype(BF16)
        y = jnp.dot(pooled, wpool_ref[g], preferred_element_type=F32) * pscale_ref[:, cols]
        acc = acc + jnp.dot(y.astype(BF16), wout_ref[cols, :], preferred_element_type=F32)
    x1_ref[...] = acc
    h2_ref[...] = _rms(acc, g_ref[...]).astype(BF16)


def _mix(x2d, u_pool, attn2d, w_pool, pool_scale, w_out, g_ffn, seq):
    m = x2d.shape[0]
    halo_per_tile = TM_MIX // POOL_HALO
    kernel = functools.partial(_mix_kernel, seq=seq)
    return pl.pallas_call(
        kernel,
        grid=(m // TM_MIX,),
        in_specs=[
            pl.BlockSpec((TM_MIX, D_MODEL), lambda i: (i, 0)),
            pl.BlockSpec((TM_MIX, POOL_WIDTH), lambda i: (i, 0)),
            pl.BlockSpec((POOL_HALO, POOL_WIDTH), lambda i: (jnp.maximum(i * halo_per_tile - 1, 0), 0)),
            pl.BlockSpec((TM_MIX, ATTN_WIDTH), lambda i: (i, 0)),
            pl.BlockSpec((POOL_GROUPS, POOL_GC, POOL_GC), lambda i: (0, 0, 0)),
            pl.BlockSpec((1, POOL_WIDTH), lambda i: (0, 0)),
            pl.BlockSpec((D_MODEL, D_MODEL), lambda i: (0, 0)),
            pl.BlockSpec((1, D_MODEL), lambda i: (0, 0)),
        ],
        out_specs=[
            pl.BlockSpec((TM_MIX, D_MODEL), lambda i: (i, 0)),
            pl.BlockSpec((TM_MIX, D_MODEL), lambda i: (i, 0)),
        ],
        out_shape=[
            jax.ShapeDtypeStruct((m, D_MODEL), F32),
            jax.ShapeDtypeStruct((m, D_MODEL), BF16),
        ],
        scratch_shapes=[pltpu.VMEM((TM_MIX + POOL_HALO, POOL_WIDTH), F32)],
        compiler_params=pltpu.CompilerParams(
            dimension_semantics=("arbitrary",), vmem_limit_bytes=VMEM_LIMIT),
        name="mix",
    )(x2d, u_pool, u_pool, attn2d, w_pool, pool_scale, w_out, g_ffn)


def _ffn_kernel(h_ref, x1_ref, wg_ref, wv_ref, cwg_ref, cwv_ref, cbg_ref, cbv_ref, wd_ref,
                o_ref, carry_s, *u_bufs, seq):
    m = pl.program_id(0)
    f = pl.program_id(1)
    tiles_per_seq = seq // TM_FFN
    first = (m % tiles_per_seq) == 0

    @pl.when(f == 0)
    def _():
        o_ref[...] = x1_ref[...]

    h = h_ref[...]
    n_half = TF_FFN // TF_HALF

    def up(s):
        cols = slice(s * TF_HALF, (s + 1) * TF_HALF)
        for k, w_ref in enumerate((wg_ref, wv_ref)):
            u = u_bufs[2 * s + k]
            u[0:SUBLANES, :] = jnp.where(first, 0.0, carry_s[f, 2 * s + k])
            u[SUBLANES:, :] = jnp.dot(h, w_ref[:, cols], preferred_element_type=F32)
            carry_s[f, 2 * s + k] = u[TM_FFN:, :]

    def conv(u, cols, cw_ref, cb_ref):
        out = cb_ref[:, cols]
        for j in range(CONV_WIDTH):
            shift = CONV_WIDTH - 1 - j
            out = out + u[pl.ds(SUBLANES - shift, TM_FFN), :] * cw_ref[j:j + 1, cols]
        return out

    def gated(s):
        cols = slice(s * TF_HALF, (s + 1) * TF_HALF)
        gate = conv(u_bufs[2 * s], cols, cwg_ref, cbg_ref)
        val = conv(u_bufs[2 * s + 1], cols, cwv_ref, cbv_ref)
        return (gate * jax.nn.sigmoid(gate) * val).astype(BF16)

    def down(s, act):
        return jnp.dot(act, wd_ref[s * TF_HALF:(s + 1) * TF_HALF, :], preferred_element_type=F32)

    up(0)
    act0 = gated(0)
    up(1)
    contrib = down(0, act0)
    act1 = gated(1)
    contrib = contrib + down(1, act1)
    o_ref[...] += contrib


def _ffn(h2, x1, w_up, conv_w, conv_b, w_down, seq):
    m = h2.shape[0]
    n_f = D_FF // TF_FFN
    kernel = functools.partial(_ffn_kernel, seq=seq)
    return pl.pallas_call(
        kernel,
        grid=(m // TM_FFN, n_f),
        in_specs=[
            pl.BlockSpec((TM_FFN, D_MODEL), lambda i, f: (i, 0)),
            pl.BlockSpec((TM_FFN, D_MODEL), lambda i, f: (i, 0), pipeline_mode=pl.Buffered(1)),
            pl.BlockSpec((D_MODEL, TF_FFN), lambda i, f: (0, f)),
            pl.BlockSpec((D_MODEL, TF_FFN), lambda i, f: (0, f + n_f)),
            pl.BlockSpec((CONV_WIDTH, TF_FFN), lambda i, f: (0, f)),
            pl.BlockSpec((CONV_WIDTH, TF_FFN), lambda i, f: (0, f + n_f)),
            pl.BlockSpec((1, TF_FFN), lambda i, f: (0, f)),
            pl.BlockSpec((1, TF_FFN), lambda i, f: (0, f + n_f)),
            pl.BlockSpec((TF_FFN, D_MODEL), lambda i, f: (f, 0)),
        ],
        out_specs=pl.BlockSpec((TM_FFN, D_MODEL), lambda i, f: (i, 0)),
        out_shape=jax.ShapeDtypeStruct((m, D_MODEL), F32),
        scratch_shapes=[pltpu.VMEM((n_f, 2 * (TF_FFN // TF_HALF), SUBLANES, TF_HALF), F32)]
        + [pltpu.VMEM((TM_FFN + SUBLANES, TF_HALF), F32)] * (2 * (TF_FFN // TF_HALF)),
        compiler_params=pltpu.CompilerParams(
            dimension_semantics=("arbitrary", "arbitrary"), vmem_limit_bytes=VMEM_LIMIT),
        name="ffn",
    )(h2, x1, w_up, w_up, conv_w, conv_w, conv_b, conv_b, w_down)


def _ple_kernel(x_ref, p_ref, gp_ref, wgate32_ref, wproj_ref, gf_ref, o_ref, wgate_ref, *, last_layer):
    @pl.when(pl.program_id(0) == 0)
    def _():
        for r in range(0, D_MODEL, CAST_ROWS):
            wgate_ref[r:r + CAST_ROWS, :] = wgate32_ref[r:r + CAST_ROWS, :].astype(BF16)

    x = x_ref[...]
    hg = _rms(x, gp_ref[...]).astype(BF16)
    gate = jax.nn.sigmoid(jnp.dot(hg, wgate_ref[...], preferred_element_type=F32))
    emb = jnp.dot(p_ref[...].astype(BF16), wproj_ref[...], preferred_element_type=F32)
    x3 = x + emb * gate
    o_ref[...] = _rms(x3, gf_ref[...]) if last_layer else x3


def _ple(x2, p2d, g_ple, w_gate, w_proj, g_final, last_layer):
    m = x2.shape[0]
    return pl.pallas_call(
        functools.partial(_ple_kernel, last_layer=last_layer),
        grid=(m // TM_PLE,),
        in_specs=[
            pl.BlockSpec((TM_PLE, D_MODEL), lambda i: (i, 0)),
            pl.BlockSpec((TM_PLE, D_PLE), lambda i: (i, 0)),
            pl.BlockSpec((1, D_MODEL), lambda i: (0, 0)),
            pl.BlockSpec((D_MODEL, D_MODEL), lambda i: (0, 0), pipeline_mode=pl.Buffered(1)),
            pl.BlockSpec((D_PLE, D_MODEL), lambda i: (0, 0)),
            pl.BlockSpec((1, D_MODEL), lambda i: (0, 0)),
        ],
        out_specs=pl.BlockSpec((TM_PLE, D_MODEL), lambda i: (i, 0)),
        out_shape=jax.ShapeDtypeStruct((m, D_MODEL), F32),
        scratch_shapes=[pltpu.VMEM((D_MODEL, D_MODEL), BF16)],
        compiler_params=pltpu.CompilerParams(
            dimension_semantics=("arbitrary",), vmem_limit_bytes=VMEM_LIMIT),
        name="ple",
    )(x2, p2d, g_ple, w_gate, w_proj, g_final)


def kernel(x, p, g_mix, w_in, w_pool, pool_scale, rel_bias, w_out, g_ffn, w_up, conv_w, conv_b,
           w_down, g_ple, w_ple_gate, w_ple_proj, g_final):
    bsz, seq, _ = x.shape
    depth = w_in.shape[0]
    assert seq % QB == 0 and seq % TM_MIX == 0 and seq % TM_FFN == 0 and (bsz * seq) % TM_IN == 0
    x2d = x.reshape(bsz * seq, D_MODEL)
    for i in range(depth):
        u_pool, k2d, qT, qiT, vT4, small = _inproj(x2d, g_mix[i].reshape(1, D_MODEL),
                                                   w_in[i].T.astype(BF16), bsz, seq)
        small3 = small.reshape(bsz, seq, SMALL_W)
        kidx = small3[:, :, :IDX_DIM].astype(BF16)
        wT = jnp.swapaxes(small3[:, :, IDX_DIM:IDX_DIM + IDX_HEADS], 1, 2)
        attn = _dsa(rel_bias, qT, qiT, wT, kidx, k2d.reshape(bsz, seq, ATTN_WIDTH), vT4, seq)

        x1, h2 = _mix(x2d, u_pool, attn.reshape(bsz * seq, ATTN_WIDTH), w_pool[i].astype(BF16),
                      pool_scale[i].reshape(1, POOL_WIDTH), w_out[i].astype(BF16),
                      g_ffn[i].reshape(1, D_MODEL), seq)
        x2 = _ffn(h2, x1, w_up[i].astype(BF16), conv_w[i], conv_b[i].reshape(1, 2 * D_FF),
                  w_down[i].astype(BF16), seq)
        x2d = _ple(x2, p[i].reshape(bsz * seq, D_PLE), g_ple[i].reshape(1, D_MODEL),
                   w_ple_gate[i], w_ple_proj[i].astype(BF16),
                   g_final.reshape(1, D_MODEL), last_layer=(i == depth - 1))
    return x2d.reshape(bsz, seq, D_MODEL)
```

```python
import functools
import math

import numpy as np
import jax
import jax.numpy as jnp
from jax import lax
from jax.experimental import pallas as pl
from jax.experimental.pallas import tpu as pltpu

F32 = jnp.float32
BF16 = jnp.bfloat16
I32 = jnp.int32
I16 = jnp.int16

D_MODEL = 2048
D_PLE = 256
POOL_WIDTH = 1024
POOL_GROUPS = 4
POOL_GC = 256
POOL_WINDOWS = (2, 4, 8, 16)
ATTN_HEADS = 8
HEAD_DIM = 128
ATTN_WIDTH = 1024
IDX_HEADS = 16
IDX_DIM = 64
TOPK_MAX = 256
N_BUCKETS = 32
MAX_DISTANCE = 128
D_FF = 5632
CONV_WIDTH = 3
EPS = 1e-6
IN_SIZES = (POOL_WIDTH, ATTN_WIDTH, ATTN_WIDTH, ATTN_WIDTH, IDX_HEADS * IDX_DIM, IDX_DIM, IDX_HEADS)
G_POOL, G_Q, G_K, G_V, G_QI = range(5)

V7X_VMEM_BYTES = 64 * 1024 * 1024
SUBLANES = 8
PACKED_SUBLANES = 16
LANES = 128
VMEM_LIMIT = V7X_VMEM_BYTES - 8 * 1024 * 1024

TM_IN = 1024
TN_IN = 512
IN_TILES_PER_GROUP = 1024 // TN_IN
SMALL_W = IDX_DIM + IDX_HEADS
QB = 256
KC = 256
TM_MIX = 512
POOL_HALO = 16
TM_FFN = 512
TF_FFN = 512
TF_HALF = 256
TM_PLE = 512
CAST_ROWS = 256

INT_MIN = -2 ** 31
I16_MIN = -2 ** 15
I16_MAX = 2 ** 15 - 1
NEG_BIG = -1e30
LOG2E = math.log2(math.e)
Q_PRESCALE = (HEAD_DIM ** -0.5) * LOG2E


def _bucket_lower_bounds():
    n = np.arange(0, 4 * MAX_DISTANCE, dtype=np.int64)
    max_exact = N_BUCKETS // 2
    nf = np.maximum(n, 1).astype(np.float32)
    large = max_exact + (np.log(nf / np.float32(max_exact)) / np.float32(math.log(MAX_DISTANCE / max_exact))
                         * np.float32(N_BUCKETS - max_exact)).astype(np.int32)
    large = np.minimum(large, N_BUCKETS - 1)
    bucket = np.where(n < max_exact, n, large)
    assert np.all(np.diff(bucket) >= 0)
    lows = [int(np.argmax(bucket == b)) for b in range(N_BUCKETS)]
    assert all(bucket[lo] == b for b, lo in enumerate(lows))
    return lows


BUCKET_LO = _bucket_lower_bounds()


def _rms(x, g):
    ms = jnp.mean(x * x, axis=-1, keepdims=True)
    return (x * lax.rsqrt(ms + EPS)) * g


def _nt_dot(a, b):
    return lax.dot_general(a, b, (((1,), (1,)), ((), ())), preferred_element_type=F32)


def _inproj_kernel(x_ref, g_ref, wt_ref, wst_ref, pool_ref, k_ref, qT_ref, qiT_ref, vT_ref, small_ref, h_s):
    n = pl.program_id(1)
    group = n // IN_TILES_PER_GROUP

    @pl.when(n == 0)
    def _():
        hb = _rms(x_ref[...], g_ref[...]).astype(BF16)
        h_s[...] = hb
        small_ref[...] = _nt_dot(hb, wst_ref[...])

    @pl.when(group == G_POOL)
    def _():
        pool_ref[...] = _nt_dot(h_s[...], wt_ref[...])

    @pl.when(group == G_K)
    def _():
        k_ref[...] = _nt_dot(h_s[...], wt_ref[...]).astype(BF16)

    @pl.when(group == G_Q)
    def _():
        qT_ref[...] = (_nt_dot(wt_ref[...], h_s[...]) * Q_PRESCALE).astype(BF16)

    @pl.when(group == G_QI)
    def _():
        qiT_ref[...] = _nt_dot(wt_ref[...], h_s[...]).astype(BF16)

    @pl.when(group == G_V)
    def _():
        out_t = _nt_dot(wt_ref[...], h_s[...]).astype(BF16)
        for j in range(TM_IN // KC):
            vT_ref[j] = out_t[:, j * KC:(j + 1) * KC]


def _inproj(x2d, g, wt, bsz, seq):
    m = x2d.shape[0]
    tiles_per_seq = seq // TM_IN
    n_main = wt.shape[0] - SMALL_W
    n_steps = n_main // TN_IN
    assert n_main % TN_IN == 0 and n_main % SMALL_W == 0

    def half(j, group):
        return jnp.clip(j - group * IN_TILES_PER_GROUP, 0, IN_TILES_PER_GROUP - 1)

    return pl.pallas_call(
        _inproj_kernel,
        grid=(m // TM_IN, n_steps),
        in_specs=[
            pl.BlockSpec((TM_IN, D_MODEL), lambda i, j: (i, 0)),
            pl.BlockSpec((1, D_MODEL), lambda i, j: (0, 0)),
            pl.BlockSpec((TN_IN, D_MODEL), lambda i, j: (j, 0)),
            pl.BlockSpec((SMALL_W, D_MODEL), lambda i, j: (n_main // SMALL_W, 0)),
        ],
        out_specs=[
            pl.BlockSpec((TM_IN, TN_IN), lambda i, j: (i, half(j, G_POOL))),
            pl.BlockSpec((TM_IN, TN_IN), lambda i, j: (i, half(j, G_K))),
            pl.BlockSpec((None, TN_IN, TM_IN),
                         lambda i, j: (i // tiles_per_seq, half(j, G_Q), i % tiles_per_seq)),
            pl.BlockSpec((None, TN_IN, TM_IN),
                         lambda i, j: (i // tiles_per_seq, half(j, G_QI), i % tiles_per_seq)),
            pl.BlockSpec((None, TM_IN // KC, TN_IN, KC),
                         lambda i, j: (i // tiles_per_seq, i % tiles_per_seq, half(j, G_V), 0)),
            pl.BlockSpec((TM_IN, SMALL_W), lambda i, j: (i, 0)),
        ],
        out_shape=[
            jax.ShapeDtypeStruct((m, POOL_WIDTH), F32),
            jax.ShapeDtypeStruct((m, ATTN_WIDTH), BF16),
            jax.ShapeDtypeStruct((bsz, ATTN_WIDTH, seq), BF16),
            jax.ShapeDtypeStruct((bsz, IDX_HEADS * IDX_DIM, seq), BF16),
            jax.ShapeDtypeStruct((bsz, seq // KC, ATTN_WIDTH, KC), BF16),
            jax.ShapeDtypeStruct((m, SMALL_W), F32),
        ],
        scratch_shapes=[pltpu.VMEM((TM_IN, D_MODEL), BF16)],
        compiler_params=pltpu.CompilerParams(
            dimension_semantics=("arbitrary", "arbitrary"), vmem_limit_bytes=VMEM_LIMIT),
        name="inproj",
    )(x2d, g, wt, wt)


def _dsa_kernel(rel_ref, qT_ref, qiT_ref, wT_ref, qiTn_ref, wTn_ref, kidx_ref, k_ref, vT_ref, *rest,
                topk, n_q, n_side):
    side_in, rest = rest[:n_side], rest[n_side:]
    o_ref, side_out = rest[0], rest[1:1 + n_side]
    hi_s, lo_s, madd_s, bias_s, mx_s, alpha_s, acc_s, s_s = rest[1 + n_side:]
    b = pl.program_id(0)
    i = pl.program_id(1)
    n_ch = i + 1

    for src, dst in zip(side_in, side_out):
        dst[...] = src[...].astype(dst.dtype)

    @pl.when((b == 0) & (i == 0))
    def _():
        r = lax.broadcasted_iota(I32, (KC, QB), 0)
        c = lax.broadcasted_iota(I32, (KC, QB), 1)
        for off in range(2):
            dist = jnp.maximum(c - r + off * QB, 0)

            def head_body(h, carry, dist=dist, off=off):
                t = jnp.full((KC, QB), rel_ref[0, h], F32)
                for bk in range(1, N_BUCKETS):
                    t = jnp.where(dist >= BUCKET_LO[bk], rel_ref[bk, h], t)
                bias_s[off, h] = (t - rel_ref[N_BUCKETS - 1, h]) * LOG2E
                return carry

            lax.fori_loop(0, ATTN_HEADS, head_body, 0)

    idx_scale = (IDX_HEADS ** -0.5) * (IDX_DIM ** -0.5)
    row = lax.broadcasted_iota(I32, (KC, QB), 0)
    col = lax.broadcasted_iota(I32, (KC, QB), 1)

    def index_chunk(m, qi_ref, w_scaled, q_block):
        kc = kidx_ref[pl.ds(pl.multiple_of(m * KC, KC), KC), :]
        acc = jnp.zeros((KC, QB), F32)
        for h in range(IDX_HEADS):
            d = jnp.dot(kc, qi_ref[h * IDX_DIM:(h + 1) * IDX_DIM, :],
                        preferred_element_type=F32)
            acc = acc + w_scaled[h:h + 1, :] * jnp.maximum(d, 0.0)
        bits = pltpu.bitcast(acc, I32)
        key = jnp.where(bits >= 0, bits, bits ^ 0x7FFFFFFF)
        causal = (m * KC + row) <= (q_block * QB + col)
        key = jnp.where(causal, key, INT_MIN)
        hi_s[m] = (key >> 16).astype(I16)
        lo_s[m] = ((key & 0xFFFF) + I16_MIN).astype(I16)

    @pl.when(i == 0)
    def _():
        index_chunk(0, qiT_ref, wT_ref[...] * idx_scale, 0)

    def count_ge(ref, cand):
        cand16 = cand.astype(I16)

        def cnt_body(m, accs):
            ge = (ref[m] >= cand16).astype(I16)
            accs = list(accs)
            for j in range(KC // PACKED_SUBLANES):
                a = j % len(accs)
                accs[a] = accs[a] + ge[j * PACKED_SUBLANES:(j + 1) * PACKED_SUBLANES, :]
            return tuple(accs)

        zero = jnp.zeros((PACKED_SUBLANES, QB), I16)
        accs = lax.fori_loop(0, n_ch, cnt_body, (zero,) * 4)
        tot = (accs[0] + accs[1]) + (accs[2] + accs[3])
        return jnp.sum(tot.astype(I32), axis=0, keepdims=True)

    def descend(ref, want):
        def bit_body(it, prefix):
            cand = prefix + jnp.left_shift(jnp.int32(1), 15 - it)
            return jnp.where(count_ge(ref, cand) >= want, cand, prefix)

        return lax.fori_loop(0, 16, bit_body, jnp.full((1, QB), I16_MIN, I32))

    p_hi = descend(hi_s, topk)
    n_above = jnp.where(p_hi >= I16_MAX, 0, count_ge(hi_s, jnp.minimum(p_hi + 1, I16_MAX)))
    hi_eq = jnp.where(p_hi > I16_MIN, p_hi, I16_MAX).astype(I16)

    def member_body(m, carry):
        lo_s[m] = jnp.where(hi_s[m] == hi_eq, lo_s[m], I16_MIN)
        return carry

    lax.fori_loop(0, n_ch, member_body, 0)
    p_lo = descend(lo_s, topk - n_above)
    p_hi16 = p_hi.astype(I16)
    p_lo16 = p_lo.astype(I16)

    def packed_count(acc, flags):
        f16 = flags.astype(I16)
        for j in range(KC // PACKED_SUBLANES):
            acc = acc + f16[j * PACKED_SUBLANES:(j + 1) * PACKED_SUBLANES, :]
        return acc

    def total(acc16):
        return jnp.sum(acc16.astype(I32), axis=0, keepdims=True)

    zero16 = jnp.zeros((PACKED_SUBLANES, QB), I16)

    def mask_body(m, kept):
        hi = hi_s[m]
        keep = (hi > p_hi16) | ((hi == hi_eq) & (lo_s[m] >= p_lo16))
        madd_s[m] = jnp.where(keep, jnp.bfloat16(0.0), jnp.bfloat16(NEG_BIG)).astype(F32)
        return packed_count(kept, keep)

    excess = total(lax.fori_loop(0, n_ch, mask_body, zero16)) - topk

    @pl.when(jnp.max(excess) > 0)
    def _():
        def tied(m):
            return (hi_s[m] == hi_eq) & (lo_s[m] == p_lo16)

        n_tied = total(lax.fori_loop(0, n_ch, lambda m, acc: packed_count(acc, tied(m)), zero16))
        need = (n_tied - excess).astype(F32)
        r = lax.broadcasted_iota(I32, (KC, KC), 0)
        c = lax.broadcasted_iota(I32, (KC, KC), 1)
        prefix_op = jnp.where(c <= r, 1.0, 0.0).astype(BF16)

        def fix_body(m, seen):
            t = jnp.where(tied(m), jnp.bfloat16(1.0), jnp.bfloat16(0.0))
            rank = seen + jnp.dot(prefix_op, t, preferred_element_type=F32)
            drop = (t.astype(F32) > 0.0) & (rank > need)
            madd_s[m] = jnp.where(drop, NEG_BIG, madd_s[m])
            return rank[KC - 1:KC, :]

        lax.fori_loop(0, n_ch, fix_body, jnp.zeros((1, QB), F32))

    n_far = jnp.maximum(i - 1, 0)
    mx_s[...] = jnp.full(mx_s.shape, -jnp.inf, F32)
    acc_s[...] = jnp.zeros(acc_s.shape, F32)

    w_next = wTn_ref[...] * idx_scale

    def att_body(m, carry, near):
        index_chunk(m, qiTn_ref, w_next, i + 1)
        madd = madd_s[m]
        rows = pl.ds(pl.multiple_of(m * KC, KC), KC)
        for h in range(ATTN_HEADS):
            hd = slice(h * HEAD_DIM, (h + 1) * HEAD_DIM)
            s = jnp.dot(k_ref[rows, hd], qT_ref[hd, :], preferred_element_type=F32)
            if near:
                s = s + (madd + bias_s[i - m, h])
            else:
                s = s + madd
            s_s[h] = s
            mx = mx_s[h]
            m_new = jnp.maximum(mx, jnp.max(s, axis=0, keepdims=True))
            alpha_s[h] = jnp.exp2(mx - m_new)
            mx_s[h] = m_new
        ones_rows = jnp.ones((PACKED_SUBLANES, KC), BF16)
        for h in range(ATTN_HEADS):
            hd = slice(h * HEAD_DIM, (h + 1) * HEAD_DIM)
            p = jnp.exp2(s_s[h] - mx_s[h])
            v_aug = jnp.concatenate([vT_ref[m, hd, :], ones_rows], axis=0)
            pv = jnp.dot(v_aug, p.astype(BF16), preferred_element_type=F32)
            acc_s[h] = alpha_s[h] * acc_s[h] + pv
        return carry

    lax.fori_loop(0, n_far, functools.partial(att_body, near=False), 0)
    lax.fori_loop(n_far, n_ch, functools.partial(att_body, near=True), 0)

    @pl.when(i + 1 < n_q)
    def _():
        index_chunk(i + 1, qiTn_ref, w_next, i + 1)

    for h in range(ATTN_HEADS):
        acc = acc_s[h]
        out_t = acc[:HEAD_DIM] / acc[HEAD_DIM:HEAD_DIM + 1]
        o_ref[:, h * HEAD_DIM:(h + 1) * HEAD_DIM] = out_t.T.astype(o_ref.dtype)


def _dsa(rel_bias, qT, qiT, wT, kidx, k, vT4, seq, side_weights):
    bsz = qT.shape[0]
    topk = min(TOPK_MAX, seq // 4)
    n_q = seq // QB
    n_steps = bsz * n_q
    kernel = functools.partial(_dsa_kernel, topk=topk, n_q=n_q, n_side=len(side_weights))
    nxt = lambda i: jnp.minimum(i + 1, n_q - 1)
    side_specs = []
    for w in side_weights:
        rows = w.shape[0] // n_steps
        assert rows * n_steps == w.shape[0] and rows % PACKED_SUBLANES == 0
        side_specs.append(pl.BlockSpec((rows, w.shape[1]), lambda b, i: (b * n_q + i, 0)))
    resident = dict(pipeline_mode=pl.Buffered(1))
    return pl.pallas_call(
        kernel,
        grid=(bsz, n_q),
        in_specs=[
            pl.BlockSpec(memory_space=pltpu.SMEM),
            pl.BlockSpec((None, ATTN_WIDTH, QB), lambda b, i: (b, 0, i)),
            pl.BlockSpec((None, IDX_HEADS * IDX_DIM, QB), lambda b, i: (b, 0, i)),
            pl.BlockSpec((None, IDX_HEADS, QB), lambda b, i: (b, 0, i)),
            pl.BlockSpec((None, IDX_HEADS * IDX_DIM, QB), lambda b, i: (b, 0, nxt(i))),
            pl.BlockSpec((None, IDX_HEADS, QB), lambda b, i: (b, 0, nxt(i))),
            pl.BlockSpec((None, seq, IDX_DIM), lambda b, i: (b, 0, 0)),
            pl.BlockSpec((None, seq, ATTN_WIDTH), lambda b, i: (b, 0, 0), **resident),
            pl.BlockSpec((None, seq // KC, ATTN_WIDTH, KC), lambda b, i: (b, 0, 0, 0), **resident),
        ] + side_specs,
        out_specs=[pl.BlockSpec((None, QB, ATTN_WIDTH), lambda b, i: (b, i, 0))] + side_specs,
        out_shape=[jax.ShapeDtypeStruct((bsz, seq, ATTN_WIDTH), BF16)]
        + [jax.ShapeDtypeStruct(w.shape, BF16) for w in side_weights],
        scratch_shapes=[
            pltpu.VMEM((seq // KC, KC, QB), I16),
            pltpu.VMEM((seq // KC, KC, QB), I16),
            pltpu.VMEM((seq // KC, KC, QB), F32),
            pltpu.VMEM((2, ATTN_HEADS, KC, QB), F32),
            pltpu.VMEM((ATTN_HEADS, 1, QB), F32),
            pltpu.VMEM((ATTN_HEADS, 1, QB), F32),
            pltpu.VMEM((ATTN_HEADS, HEAD_DIM + PACKED_SUBLANES, QB), F32),
            pltpu.VMEM((ATTN_HEADS, KC, QB), F32),
        ],
        compiler_params=pltpu.CompilerParams(
            dimension_semantics=("arbitrary", "arbitrary"), vmem_limit_bytes=VMEM_LIMIT),
        name="dsa",
    )(rel_bias, qT, qiT, wT, qiT, wT, kidx, k, vT4, *side_weights)


def _mix_kernel(x_ref, u_ref, halo_ref, attn_ref, wpool_ref, pscale_ref, wout_ref, g_ref,
                x1_ref, h2_ref, ext_s, *, seq):
    m = pl.program_id(0)
    tiles_per_seq = seq // TM_MIX
    first = (m % tiles_per_seq) == 0
    ext_s[0:POOL_HALO, :] = jnp.where(first, 0.0, halo_ref[...])
    ext_s[POOL_HALO:, :] = u_ref[...]

    t_pos = (m % tiles_per_seq) * TM_MIX + lax.broadcasted_iota(I32, (TM_MIX, POOL_GC), 0)
    acc = x_ref[...] + jnp.dot(attn_ref[...], wout_ref[POOL_WIDTH:, :], preferred_element_type=F32)
    for g, w in enumerate(POOL_WINDOWS):
        cols = slice(g * POOL_GC, (g + 1) * POOL_GC)
        u = ext_s[POOL_HALO:, cols]
        wsum = u
        for j in range(1, w):
            wsum = wsum + ext_s[pl.ds(POOL_HALO - j, TM_MIX), cols]
        count = jnp.minimum(t_pos + 1, w).astype(F32)
        pooled = (wsum / count - u).astype(BF16)
        y = jnp.dot(pooled, wpool_ref[g], preferred_element_type=F32) * pscale_ref[:, cols]
        acc = acc + jnp.dot(y.astype(BF16), wout_ref[cols, :], preferred_element_type=F32)
    x1_ref[...] = acc
    h2_ref[...] = _rms(acc, g_ref[...]).astype(BF16)


def _mix(x2d, u_pool, attn2d, w_pool, pool_scale, w_out, g_ffn, seq):
    m = x2d.shape[0]
    halo_per_tile = TM_MIX // POOL_HALO
    kernel = functools.partial(_mix_kernel, seq=seq)
    return pl.pallas_call(
        kernel,
        grid=(m // TM_MIX,),
        in_specs=[
            pl.BlockSpec((TM_MIX, D_MODEL), lambda i: (i, 0)),
            pl.BlockSpec((TM_MIX, POOL_WIDTH), lambda i: (i, 0)),
            pl.BlockSpec((POOL_HALO, POOL_WIDTH), lambda i: (jnp.maximum(i * halo_per_tile - 1, 0), 0)),
            pl.BlockSpec((TM_MIX, ATTN_WIDTH), lambda i: (i, 0)),
            pl.BlockSpec((POOL_GROUPS, POOL_GC, POOL_GC), lambda i: (0, 0, 0)),
            pl.BlockSpec((1, POOL_WIDTH), lambda i: (0, 0)),
            pl.BlockSpec((D_MODEL, D_MODEL), lambda i: (0, 0)),
            pl.BlockSpec((1, D_MODEL), lambda i: (0, 0)),
        ],
        out_specs=[
            pl.BlockSpec((TM_MIX, D_MODEL), lambda i: (i, 0)),
            pl.BlockSpec((TM_MIX, D_MODEL), lambda i: (i, 0)),
        ],
        out_shape=[
            jax.ShapeDtypeStruct((m, D_MODEL), F32),
            jax.ShapeDtypeStruct((m, D_MODEL), BF16),
        ],
        scratch_shapes=[pltpu.VMEM((TM_MIX + POOL_HALO, POOL_WIDTH), F32)],
        compiler_params=pltpu.CompilerParams(
            dimension_semantics=("arbitrary",), vmem_limit_bytes=VMEM_LIMIT),
        name="mix",
    )(x2d, u_pool, u_pool, attn2d, w_pool, pool_scale, w_out, g_ffn)


def _ffn_kernel(h_ref, x1_ref, wg_ref, wv_ref, cwg_ref, cwv_ref, cbg_ref, cbv_ref, wd_ref,
                o_ref, carry_s, *u_bufs, seq):
    m = pl.program_id(0)
    f = pl.program_id(1)
    tiles_per_seq = seq // TM_FFN
    first = (m % tiles_per_seq) == 0

    @pl.when(f == 0)
    def _():
        o_ref[...] = x1_ref[...]

    h = h_ref[...]
    n_half = TF_FFN // TF_HALF

    def up(s):
        cols = slice(s * TF_HALF, (s + 1) * TF_HALF)
        for k, w_ref in enumerate((wg_ref, wv_ref)):
            u = u_bufs[2 * s + k]
            u[0:SUBLANES, :] = jnp.where(first, 0.0, carry_s[f, 2 * s + k])
            u[SUBLANES:, :] = jnp.dot(h, w_ref[:, cols], preferred_element_type=F32)
            carry_s[f, 2 * s + k] = u[TM_FFN:, :]

    def conv(u, cols, cw_ref, cb_ref):
        out = cb_ref[:, cols]
        for j in range(CONV_WIDTH):
            shift = CONV_WIDTH - 1 - j
            out = out + u[pl.ds(SUBLANES - shift, TM_FFN), :] * cw_ref[j:j + 1, cols]
        return out

    def gated(s):
        cols = slice(s * TF_HALF, (s + 1) * TF_HALF)
        gate = conv(u_bufs[2 * s], cols, cwg_ref, cbg_ref)
        val = conv(u_bufs[2 * s + 1], cols, cwv_ref, cbv_ref)
        return (gate * jax.nn.sigmoid(gate) * val).astype(BF16)

    def down(s, act):
        return jnp.dot(act, wd_ref[s * TF_HALF:(s + 1) * TF_HALF, :], preferred_element_type=F32)

    up(0)
    act0 = gated(0)
    up(1)
    contrib = down(0, act0)
    act1 = gated(1)
    contrib = contrib + down(1, act1)
    o_ref[...] += contrib


def _ffn(h2, x1, w_up, conv_w, conv_b, w_down, seq):
    m = h2.shape[0]
    n_f = D_FF // TF_FFN
    kernel = functools.partial(_ffn_kernel, seq=seq)
    return pl.pallas_call(
        kernel,
        grid=(m // TM_FFN, n_f),
        in_specs=[
            pl.BlockSpec((TM_FFN, D_MODEL), lambda i, f: (i, 0)),
            pl.BlockSpec((TM_FFN, D_MODEL), lambda i, f: (i, 0)),
            pl.BlockSpec((D_MODEL, TF_FFN), lambda i, f: (0, f)),
            pl.BlockSpec((D_MODEL, TF_FFN), lambda i, f: (0, f + n_f)),
            pl.BlockSpec((CONV_WIDTH, TF_FFN), lambda i, f: (0, f)),
            pl.BlockSpec((CONV_WIDTH, TF_FFN), lambda i, f: (0, f + n_f)),
            pl.BlockSpec((1, TF_FFN), lambda i, f: (0, f)),
            pl.BlockSpec((1, TF_FFN), lambda i, f: (0, f + n_f)),
            pl.BlockSpec((TF_FFN, D_MODEL), lambda i, f: (f, 0)),
        ],
        out_specs=pl.BlockSpec((TM_FFN, D_MODEL), lambda i, f: (i, 0)),
        out_shape=jax.ShapeDtypeStruct((m, D_MODEL), F32),
        scratch_shapes=[pltpu.VMEM((n_f, 2 * (TF_FFN // TF_HALF), SUBLANES, TF_HALF), F32)]
        + [pltpu.VMEM((TM_FFN + SUBLANES, TF_HALF), F32)] * (2 * (TF_FFN // TF_HALF)),
        compiler_params=pltpu.CompilerParams(
            dimension_semantics=("arbitrary", "arbitrary"), vmem_limit_bytes=VMEM_LIMIT),
        name="ffn",
    )(h2, x1, w_up, w_up, conv_w, conv_w, conv_b, conv_b, w_down)


def _ple_kernel(x_ref, p_ref, gp_ref, wgate32_ref, wproj_ref, gf_ref, o_ref, wgate_ref, *, last_layer):
    @pl.when(pl.program_id(0) == 0)
    def _():
        for r in range(0, D_MODEL, CAST_ROWS):
            wgate_ref[r:r + CAST_ROWS, :] = wgate32_ref[r:r + CAST_ROWS, :].astype(BF16)

    x = x_ref[...]
    hg = _rms(x, gp_ref[...]).astype(BF16)
    gate = jax.nn.sigmoid(jnp.dot(hg, wgate_ref[...], preferred_element_type=F32))
    emb = jnp.dot(p_ref[...].astype(BF16), wproj_ref[...], preferred_element_type=F32)
    x3 = x + emb * gate
    o_ref[...] = _rms(x3, gf_ref[...]) if last_layer else x3


def _ple(x2, p2d, g_ple, w_gate, w_proj, g_final, last_layer):
    m = x2.shape[0]
    return pl.pallas_call(
        functools.partial(_ple_kernel, last_layer=last_layer),
        grid=(m // TM_PLE,),
        in_specs=[
            pl.BlockSpec((TM_PLE, D_MODEL), lambda i: (i, 0)),
            pl.BlockSpec((TM_PLE, D_PLE), lambda i: (i, 0)),
            pl.BlockSpec((1, D_MODEL), lambda i: (0, 0)),
            pl.BlockSpec((D_MODEL, D_MODEL), lambda i: (0, 0), pipeline_mode=pl.Buffered(1)),
            pl.BlockSpec((D_PLE, D_MODEL), lambda i: (0, 0)),
            pl.BlockSpec((1, D_MODEL), lambda i: (0, 0)),
        ],
        out_specs=pl.BlockSpec((TM_PLE, D_MODEL), lambda i: (i, 0)),
        out_shape=jax.ShapeDtypeStruct((m, D_MODEL), F32),
        scratch_shapes=[pltpu.VMEM((D_MODEL, D_MODEL), BF16)],
        compiler_params=pltpu.CompilerParams(
            dimension_semantics=("arbitrary",), vmem_limit_bytes=VMEM_LIMIT),
        name="ple",
    )(x2, p2d, g_ple, w_gate, w_proj, g_final)


def kernel(x, p, g_mix, w_in, w_pool, pool_scale, rel_bias, w_out, g_ffn, w_up, conv_w, conv_b,
           w_down, g_ple, w_ple_gate, w_ple_proj, g_final):
    bsz, seq, _ = x.shape
    depth = w_in.shape[0]
    assert seq % QB == 0 and seq % TM_MIX == 0 and seq % TM_FFN == 0 and (bsz * seq) % TM_IN == 0
    x2d = x.reshape(bsz * seq, D_MODEL)
    for i in range(depth):
        u_pool, k2d, qT, qiT, vT4, small = _inproj(x2d, g_mix[i].reshape(1, D_MODEL),
                                                   w_in[i].T.astype(BF16), bsz, seq)
        small3 = small.reshape(bsz, seq, SMALL_W)
        kidx = small3[:, :, :IDX_DIM].astype(BF16)
        wT = jnp.swapaxes(small3[:, :, IDX_DIM:IDX_DIM + IDX_HEADS], 1, 2)
        attn, w_out_b, w_up_b, w_down_b = _dsa(rel_bias, qT, qiT, wT, kidx,
                                               k2d.reshape(bsz, seq, ATTN_WIDTH), vT4, seq,
                                               (w_out[i], w_up[i], w_down[i]))

        x1, h2 = _mix(x2d, u_pool, attn.reshape(bsz * seq, ATTN_WIDTH), w_pool[i].astype(BF16),
                      pool_scale[i].reshape(1, POOL_WIDTH), w_out_b,
                      g_ffn[i].reshape(1, D_MODEL), seq)
        x2 = _ffn(h2, x1, w_up_b, conv_w[i], conv_b[i].reshape(1, 2 * D_FF), w_down_b, seq)
        x2d = _ple(x2, p[i].reshape(bsz * seq, D_PLE), g_ple[i].reshape(1, D_MODEL),
                   w_ple_gate[i], w_ple_proj[i].astype(BF16),
                   g_final.reshape(1, D_MODEL), last_layer=(i == depth - 1))
    return x2d.reshape(bsz, seq, D_MODEL)
```

```python
import functools
import math

import numpy as np
import jax
import jax.numpy as jnp
from jax import lax
from jax.experimental import pallas as pl
from jax.experimental.pallas import tpu as pltpu

F32 = jnp.float32
BF16 = jnp.bfloat16
I32 = jnp.int32
I16 = jnp.int16

D_MODEL = 2048
D_PLE = 256
POOL_WIDTH = 1024
POOL_GROUPS = 4
POOL_GC = 256
POOL_WINDOWS = (2, 4, 8, 16)
ATTN_HEADS = 8
HEAD_DIM = 128
ATTN_WIDTH = 1024
IDX_HEADS = 16
IDX_DIM = 64
TOPK_MAX = 256
N_BUCKETS = 32
MAX_DISTANCE = 128
D_FF = 5632
CONV_WIDTH = 3
EPS = 1e-6
IN_SIZES = (POOL_WIDTH, ATTN_WIDTH, ATTN_WIDTH, ATTN_WIDTH, IDX_HEADS * IDX_DIM, IDX_DIM, IDX_HEADS)
G_POOL, G_Q, G_K, G_V, G_QI = range(5)
N_GROUPS = 5
GROUP_W = 1024

V7X_VMEM_BYTES = 64 * 1024 * 1024
SUBLANES = 8
PACKED_SUBLANES = 16
LANES = 128
VMEM_LIMIT = V7X_VMEM_BYTES - 8 * 1024 * 1024

TM_IN = 512
SMALL_W = IDX_DIM + IDX_HEADS
QB = 256
KC = 256
TM_MIX = 512
POOL_HALO = 16
TM_FFN = 512
TF_FFN = 512
TF_HALF = 256
TM_PLE = 512
CAST_ROWS = 256

INT_MIN = -2 ** 31
I16_MIN = -2 ** 15
I16_MAX = 2 ** 15 - 1
NEG_BIG = -1e30
LOG2E = math.log2(math.e)
Q_PRESCALE = (HEAD_DIM ** -0.5) * LOG2E


def _bucket_lower_bounds():
    n = np.arange(0, 4 * MAX_DISTANCE, dtype=np.int64)
    max_exact = N_BUCKETS // 2
    nf = np.maximum(n, 1).astype(np.float32)
    large = max_exact + (np.log(nf / np.float32(max_exact)) / np.float32(math.log(MAX_DISTANCE / max_exact))
                         * np.float32(N_BUCKETS - max_exact)).astype(np.int32)
    large = np.minimum(large, N_BUCKETS - 1)
    bucket = np.where(n < max_exact, n, large)
    assert np.all(np.diff(bucket) >= 0)
    lows = [int(np.argmax(bucket == b)) for b in range(N_BUCKETS)]
    assert all(bucket[lo] == b for b, lo in enumerate(lows))
    return lows


BUCKET_LO = _bucket_lower_bounds()


def _rms(x, g):
    ms = jnp.mean(x * x, axis=-1, keepdims=True)
    return (x * lax.rsqrt(ms + EPS)) * g


def _nt_dot(a, b):
    return lax.dot_general(a, b, (((1,), (1,)), ((), ())), preferred_element_type=F32)


def _inproj_kernel(x_ref, g_ref, wt_ref, pool_ref, k_ref, qT_ref, qiT_ref, vT_ref, small_ref):
    hb = _rms(x_ref[...], g_ref[...]).astype(BF16)

    def w(group):
        return wt_ref[group * GROUP_W:(group + 1) * GROUP_W, :]

    pool_ref[...] = _nt_dot(hb, w(G_POOL))
    k_ref[...] = _nt_dot(hb, w(G_K)).astype(BF16)
    small_ref[...] = _nt_dot(hb, wt_ref[N_GROUPS * GROUP_W:, :])
    qT_ref[...] = (_nt_dot(w(G_Q), hb) * Q_PRESCALE).astype(BF16)
    qiT_ref[...] = _nt_dot(w(G_QI), hb).astype(BF16)
    out_t = _nt_dot(w(G_V), hb).astype(BF16)
    for j in range(TM_IN // KC):
        vT_ref[j] = out_t[:, j * KC:(j + 1) * KC]


def _inproj(x2d, g, wt, bsz, seq):
    m = x2d.shape[0]
    tiles_per_seq = seq // TM_IN
    assert wt.shape[0] == N_GROUPS * GROUP_W + SMALL_W
    return pl.pallas_call(
        _inproj_kernel,
        grid=(m // TM_IN,),
        in_specs=[
            pl.BlockSpec((TM_IN, D_MODEL), lambda i: (i, 0)),
            pl.BlockSpec((1, D_MODEL), lambda i: (0, 0)),
            pl.BlockSpec(wt.shape, lambda i: (0, 0), pipeline_mode=pl.Buffered(1)),
        ],
        out_specs=[
            pl.BlockSpec((TM_IN, GROUP_W), lambda i: (i, 0)),
            pl.BlockSpec((TM_IN, GROUP_W), lambda i: (i, 0)),
            pl.BlockSpec((None, GROUP_W, TM_IN), lambda i: (i // tiles_per_seq, 0, i % tiles_per_seq)),
            pl.BlockSpec((None, GROUP_W, TM_IN), lambda i: (i // tiles_per_seq, 0, i % tiles_per_seq)),
            pl.BlockSpec((None, TM_IN // KC, GROUP_W, KC),
                         lambda i: (i // tiles_per_seq, i % tiles_per_seq, 0, 0)),
            pl.BlockSpec((TM_IN, SMALL_W), lambda i: (i, 0)),
        ],
        out_shape=[
            jax.ShapeDtypeStruct((m, POOL_WIDTH), F32),
            jax.ShapeDtypeStruct((m, ATTN_WIDTH), BF16),
            jax.ShapeDtypeStruct((bsz, ATTN_WIDTH, seq), BF16),
            jax.ShapeDtypeStruct((bsz, IDX_HEADS * IDX_DIM, seq), BF16),
            jax.ShapeDtypeStruct((bsz, seq // KC, ATTN_WIDTH, KC), BF16),
            jax.ShapeDtypeStruct((m, SMALL_W), F32),
        ],
        compiler_params=pltpu.CompilerParams(
            dimension_semantics=("arbitrary",), vmem_limit_bytes=VMEM_LIMIT),
        name="inproj",
    )(x2d, g, wt)


def _dsa_kernel(rel_ref, qT_ref, qiT_ref, wT_ref, qiTn_ref, wTn_ref, kidx_ref, k_ref, vT_ref, *rest,
                topk, n_q, n_side):
    side_in, rest = rest[:n_side], rest[n_side:]
    o_ref, side_out = rest[0], rest[1:1 + n_side]
    hi_s, lo_s, madd_s, bias_s, mx_s, alpha_s, acc_s, s_s = rest[1 + n_side:]
    b = pl.program_id(0)
    i = pl.program_id(1)
    n_ch = i + 1

    for src, dst in zip(side_in, side_out):
        dst[...] = src[...].astype(dst.dtype)

    @pl.when((b == 0) & (i == 0))
    def _():
        r = lax.broadcasted_iota(I32, (KC, QB), 0)
        c = lax.broadcasted_iota(I32, (KC, QB), 1)
        for off in range(2):
            dist = jnp.maximum(c - r + off * QB, 0)

            def head_body(h, carry, dist=dist, off=off):
                t = jnp.full((KC, QB), rel_ref[0, h], F32)
                for bk in range(1, N_BUCKETS):
                    t = jnp.where(dist >= BUCKET_LO[bk], rel_ref[bk, h], t)
                bias_s[off, h] = (t - rel_ref[N_BUCKETS - 1, h]) * LOG2E
                return carry

            lax.fori_loop(0, ATTN_HEADS, head_body, 0)

    idx_scale = (IDX_HEADS ** -0.5) * (IDX_DIM ** -0.5)
    row = lax.broadcasted_iota(I32, (KC, QB), 0)
    col = lax.broadcasted_iota(I32, (KC, QB), 1)

    def index_chunk(m, qi_ref, w_scaled, q_block):
        kc = kidx_ref[pl.ds(pl.multiple_of(m * KC, KC), KC), :]
        acc = jnp.zeros((KC, QB), F32)
        for h in range(IDX_HEADS):
            d = jnp.dot(kc, qi_ref[h * IDX_DIM:(h + 1) * IDX_DIM, :],
                        preferred_element_type=F32)
            acc = acc + w_scaled[h:h + 1, :] * jnp.maximum(d, 0.0)
        bits = pltpu.bitcast(acc, I32)
        key = jnp.where(bits >= 0, bits, bits ^ 0x7FFFFFFF)
        causal = (m * KC + row) <= (q_block * QB + col)
        key = jnp.where(causal, key, INT_MIN)
        hi_s[m] = (key >> 16).astype(I16)
        lo_s[m] = ((key & 0xFFFF) + I16_MIN).astype(I16)

    @pl.when(i == 0)
    def _():
        index_chunk(0, qiT_ref, wT_ref[...] * idx_scale, 0)

    def count_ge(ref, cand):
        cand16 = cand.astype(I16)

        def cnt_body(m, accs):
            ge = (ref[m] >= cand16).astype(I16)
            accs = list(accs)
            for j in range(KC // PACKED_SUBLANES):
                a = j % len(accs)
                accs[a] = accs[a] + ge[j * PACKED_SUBLANES:(j + 1) * PACKED_SUBLANES, :]
            return tuple(accs)

        zero = jnp.zeros((PACKED_SUBLANES, QB), I16)
        accs = lax.fori_loop(0, n_ch, cnt_body, (zero,) * 4)
        tot = (accs[0] + accs[1]) + (accs[2] + accs[3])
        return jnp.sum(tot.astype(I32), axis=0, keepdims=True)

    def descend(ref, want):
        def bit_body(it, prefix):
            cand = prefix + jnp.left_shift(jnp.int32(1), 15 - it)
            return jnp.where(count_ge(ref, cand) >= want, cand, prefix)

        return lax.fori_loop(0, 16, bit_body, jnp.full((1, QB), I16_MIN, I32))

    p_hi = descend(hi_s, topk)
    n_above = jnp.where(p_hi >= I16_MAX, 0, count_ge(hi_s, jnp.minimum(p_hi + 1, I16_MAX)))
    hi_eq = jnp.where(p_hi > I16_MIN, p_hi, I16_MAX).astype(I16)

    def member_body(m, carry):
        lo_s[m] = jnp.where(hi_s[m] == hi_eq, lo_s[m], I16_MIN)
        return carry

    lax.fori_loop(0, n_ch, member_body, 0)
    p_lo = descend(lo_s, topk - n_above)
    p_hi16 = p_hi.astype(I16)
    p_lo16 = p_lo.astype(I16)

    def packed_count(acc, flags):
        f16 = flags.astype(I16)
        for j in range(KC // PACKED_SUBLANES):
            acc = acc + f16[j * PACKED_SUBLANES:(j + 1) * PACKED_SUBLANES, :]
        return acc

    def total(acc16):
        return jnp.sum(acc16.astype(I32), axis=0, keepdims=True)

    zero16 = jnp.zeros((PACKED_SUBLANES, QB), I16)

    def mask_body(m, kept):
        hi = hi_s[m]
        keep = (hi > p_hi16) | ((hi == hi_eq) & (lo_s[m] >= p_lo16))
        madd_s[m] = jnp.where(keep, jnp.bfloat16(0.0), jnp.bfloat16(NEG_BIG)).astype(F32)
        return packed_count(kept, keep)

    excess = total(lax.fori_loop(0, n_ch, mask_body, zero16)) - topk

    @pl.when(jnp.max(excess) > 0)
    def _():
        def tied(m):
            return (hi_s[m] == hi_eq) & (lo_s[m] == p_lo16)

        n_tied = total(lax.fori_loop(0, n_ch, lambda m, acc: packed_count(acc, tied(m)), zero16))
        need = (n_tied - excess).astype(F32)
        r = lax.broadcasted_iota(I32, (KC, KC), 0)
        c = lax.broadcasted_iota(I32, (KC, KC), 1)
        prefix_op = jnp.where(c <= r, 1.0, 0.0).astype(BF16)

        def fix_body(m, seen):
            t = jnp.where(tied(m), jnp.bfloat16(1.0), jnp.bfloat16(0.0))
            rank = seen + jnp.dot(prefix_op, t, preferred_element_type=F32)
            drop = (t.astype(F32) > 0.0) & (rank > need)
            madd_s[m] = jnp.where(drop, NEG_BIG, madd_s[m])
            return rank[KC - 1:KC, :]

        lax.fori_loop(0, n_ch, fix_body, jnp.zeros((1, QB), F32))

    n_far = jnp.maximum(i - 1, 0)
    mx_s[...] = jnp.full(mx_s.shape, -jnp.inf, F32)
    acc_s[...] = jnp.zeros(acc_s.shape, F32)

    w_next = wTn_ref[...] * idx_scale

    def att_body(m, carry, near):
        index_chunk(m, qiTn_ref, w_next, i + 1)
        madd = madd_s[m]
        rows = pl.ds(pl.multiple_of(m * KC, KC), KC)
        for h in range(ATTN_HEADS):
            hd = slice(h * HEAD_DIM, (h + 1) * HEAD_DIM)
            s = jnp.dot(k_ref[rows, hd], qT_ref[hd, :], preferred_element_type=F32)
            if near:
                s = s + (madd + bias_s[i - m, h])
            else:
                s = s + madd
            s_s[h] = s
            mx = mx_s[h]
            m_new = jnp.maximum(mx, jnp.max(s, axis=0, keepdims=True))
            alpha_s[h] = jnp.exp2(mx - m_new)
            mx_s[h] = m_new
        ones_rows = jnp.ones((PACKED_SUBLANES, KC), BF16)
        for h in range(ATTN_HEADS):
            hd = slice(h * HEAD_DIM, (h + 1) * HEAD_DIM)
            p = jnp.exp2(s_s[h] - mx_s[h])
            v_aug = jnp.concatenate([vT_ref[m, hd, :], ones_rows], axis=0)
            pv = jnp.dot(v_aug, p.astype(BF16), preferred_element_type=F32)
            acc_s[h] = alpha_s[h] * acc_s[h] + pv
        return carry

    lax.fori_loop(0, n_far, functools.partial(att_body, near=False), 0)
    lax.fori_loop(n_far, n_ch, functools.partial(att_body, near=True), 0)

    @pl.when(i + 1 < n_q)
    def _():
        index_chunk(i + 1, qiTn_ref, w_next, i + 1)

    for h in range(ATTN_HEADS):
        acc = acc_s[h]
        out_t = acc[:HEAD_DIM] / acc[HEAD_DIM:HEAD_DIM + 1]
        o_ref[:, h * HEAD_DIM:(h + 1) * HEAD_DIM] = out_t.T.astype(o_ref.dtype)


def _dsa(rel_bias, qT, qiT, wT, kidx, k, vT4, seq, side_weights):
    bsz = qT.shape[0]
    topk = min(TOPK_MAX, seq // 4)
    n_q = seq // QB
    n_steps = bsz * n_q
    kernel = functools.partial(_dsa_kernel, topk=topk, n_q=n_q, n_side=len(side_weights))
    nxt = lambda i: jnp.minimum(i + 1, n_q - 1)
    side_specs = []
    for w in side_weights:
        rows = w.shape[0] // n_steps
        assert rows * n_steps == w.shape[0] and rows % PACKED_SUBLANES == 0
        side_specs.append(pl.BlockSpec((rows, w.shape[1]), lambda b, i: (b * n_q + i, 0)))
    resident = dict(pipeline_mode=pl.Buffered(1))
    return pl.pallas_call(
        kernel,
        grid=(bsz, n_q),
        in_specs=[
            pl.BlockSpec(memory_space=pltpu.SMEM),
            pl.BlockSpec((None, ATTN_WIDTH, QB), lambda b, i: (b, 0, i)),
            pl.BlockSpec((None, IDX_HEADS * IDX_DIM, QB), lambda b, i: (b, 0, i)),
            pl.BlockSpec((None, IDX_HEADS, QB), lambda b, i: (b, 0, i)),
            pl.BlockSpec((None, IDX_HEADS * IDX_DIM, QB), lambda b, i: (b, 0, nxt(i))),
            pl.BlockSpec((None, IDX_HEADS, QB), lambda b, i: (b, 0, nxt(i))),
            pl.BlockSpec((None, seq, IDX_DIM), lambda b, i: (b, 0, 0)),
            pl.BlockSpec((None, seq, ATTN_WIDTH), lambda b, i: (b, 0, 0), **resident),
            pl.BlockSpec((None, seq // KC, ATTN_WIDTH, KC), lambda b, i: (b, 0, 0, 0), **resident),
        ] + side_specs,
        out_specs=[pl.BlockSpec((None, QB, ATTN_WIDTH), lambda b, i: (b, i, 0))] + side_specs,
        out_shape=[jax.ShapeDtypeStruct((bsz, seq, ATTN_WIDTH), BF16)]
        + [jax.ShapeDtypeStruct(w.shape, BF16) for w in side_weights],
        scratch_shapes=[
            pltpu.VMEM((seq // KC, KC, QB), I16),
            pltpu.VMEM((seq // KC, KC, QB), I16),
            pltpu.VMEM((seq // KC, KC, QB), F32),
            pltpu.VMEM((2, ATTN_HEADS, KC, QB), F32),
            pltpu.VMEM((ATTN_HEADS, 1, QB), F32),
            pltpu.VMEM((ATTN_HEADS, 1, QB), F32),
            pltpu.VMEM((ATTN_HEADS, HEAD_DIM + PACKED_SUBLANES, QB), F32),
            pltpu.VMEM((ATTN_HEADS, KC, QB), F32),
        ],
        compiler_params=pltpu.CompilerParams(
            dimension_semantics=("arbitrary", "arbitrary"), vmem_limit_bytes=VMEM_LIMIT),
        name="dsa",
    )(rel_bias, qT, qiT, wT, qiT, wT, kidx, k, vT4, *side_weights)


def _mix_kernel(x_ref, u_ref, halo_ref, attn_ref, wpool_ref, pscale_ref, wout_ref, g_ref,
                x1_ref, h2_ref, ext_s, *, seq):
    m = pl.program_id(0)
    tiles_per_seq = seq // TM_MIX
    first = (m % tiles_per_seq) == 0
    ext_s[0:POOL_HALO, :] = jnp.where(first, 0.0, halo_ref[...])
    ext_s[POOL_HALO:, :] = u_ref[...]

    t_pos = (m % tiles_per_seq) * TM_MIX + lax.broadcasted_iota(I32, (TM_MIX, POOL_GC), 0)
    acc = x_ref[...] + jnp.dot(attn_ref[...], wout_ref[POOL_WIDTH:, :], preferred_element_type=F32)
    for g, w in enumerate(POOL_WINDOWS):
        cols = slice(g * POOL_GC, (g + 1) * POOL_GC)
        u = ext_s[POOL_HALO:, cols]
        wsum = u
        for j in range(1, w):
            wsum = wsum + ext_s[pl.ds(POOL_HALO - j, TM_MIX), cols]
        count = jnp.minimum(t_pos + 1, w).astype(F32)
        pooled = (wsum / count - u).astype(BF16)
        y = jnp.dot(pooled, wpool_ref[g], preferred_element_type=F32) * pscale_ref[:, cols]
        acc = acc + jnp.dot(y.astype(BF16), wout_ref[cols, :], preferred_element_type=F32)
    x1_ref[...] = acc
    h2_ref[...] = _rms(acc, g_ref[...]).astype(BF16)


def _mix(x2d, u_pool, attn2d, w_pool, pool_scale, w_out, g_ffn, seq):
    m = x2d.shape[0]
    halo_per_tile = TM_MIX // POOL_HALO
    kernel = functools.partial(_mix_kernel, seq=seq)
    return pl.pallas_call(
        kernel,
        grid=(m // TM_MIX,),
        in_specs=[
            pl.BlockSpec((TM_MIX, D_MODEL), lambda i: (i, 0)),
            pl.BlockSpec((TM_MIX, POOL_WIDTH), lambda i: (i, 0)),
            pl.BlockSpec((POOL_HALO, POOL_WIDTH), lambda i: (jnp.maximum(i * halo_per_tile - 1, 0), 0)),
            pl.BlockSpec((TM_MIX, ATTN_WIDTH), lambda i: (i, 0)),
            pl.BlockSpec((POOL_GROUPS, POOL_GC, POOL_GC), lambda i: (0, 0, 0)),
            pl.BlockSpec((1, POOL_WIDTH), lambda i: (0, 0)),
            pl.BlockSpec((D_MODEL, D_MODEL), lambda i: (0, 0)),
            pl.BlockSpec((1, D_MODEL), lambda i: (0, 0)),
        ],
        out_specs=[
            pl.BlockSpec((TM_MIX, D_MODEL), lambda i: (i, 0)),
            pl.BlockSpec((TM_MIX, D_MODEL), lambda i: (i, 0)),
        ],
        out_shape=[
            jax.ShapeDtypeStruct((m, D_MODEL), F32),
            jax.ShapeDtypeStruct((m, D_MODEL), BF16),
        ],
        scratch_shapes=[pltpu.VMEM((TM_MIX + POOL_HALO, POOL_WIDTH), F32)],
        compiler_params=pltpu.CompilerParams(
            dimension_semantics=("arbitrary",), vmem_limit_bytes=VMEM_LIMIT),
        name="mix",
    )(x2d, u_pool, u_pool, attn2d, w_pool, pool_scale, w_out, g_ffn)


def _ffn_kernel(h_ref, x1_ref, wg_ref, wv_ref, cwg_ref, cwv_ref, cbg_ref, cbv_ref, wd_ref,
                o_ref, carry_s, *u_bufs, seq):
    m = pl.program_id(0)
    f = pl.program_id(1)
    tiles_per_seq = seq // TM_FFN
    first = (m % tiles_per_seq) == 0

    @pl.when(f == 0)
    def _():
        o_ref[...] = x1_ref[...]

    h = h_ref[...]
    n_half = TF_FFN // TF_HALF

    def up(s):
        cols = slice(s * TF_HALF, (s + 1) * TF_HALF)
        for k, w_ref in enumerate((wg_ref, wv_ref)):
            u = u_bufs[2 * s + k]
            u[0:SUBLANES, :] = jnp.where(first, 0.0, carry_s[f, 2 * s + k])
            u[SUBLANES:, :] = jnp.dot(h, w_ref[:, cols], preferred_element_type=F32)
            carry_s[f, 2 * s + k] = u[TM_FFN:, :]

    def conv(u, cols, cw_ref, cb_ref):
        out = cb_ref[:, cols]
        for j in range(CONV_WIDTH):
            shift = CONV_WIDTH - 1 - j
            out = out + u[pl.ds(SUBLANES - shift, TM_FFN), :] * cw_ref[j:j + 1, cols]
        return out

    def gated(s):
        cols = slice(s * TF_HALF, (s + 1) * TF_HALF)
        gate = conv(u_bufs[2 * s], cols, cwg_ref, cbg_ref)
        val = conv(u_bufs[2 * s + 1], cols, cwv_ref, cbv_ref)
        return (gate * jax.nn.sigmoid(gate) * val).astype(BF16)

    def down(s, act):
        return jnp.dot(act, wd_ref[s * TF_HALF:(s + 1) * TF_HALF, :], preferred_element_type=F32)

    up(0)
    act0 = gated(0)
    up(1)
    contrib = down(0, act0)
    act1 = gated(1)
    contrib = contrib + down(1, act1)
    o_ref[...] += contrib


def _ffn(h2, x1, w_up, conv_w, conv_b, w_down, seq):
    m = h2.shape[0]
    n_f = D_FF // TF_FFN
    kernel = functools.partial(_ffn_kernel, seq=seq)
    return pl.pallas_call(
        kernel,
        grid=(m // TM_FFN, n_f),
        in_specs=[
            pl.BlockSpec((TM_FFN, D_MODEL), lambda i, f: (i, 0)),
            pl.BlockSpec((TM_FFN, D_MODEL), lambda i, f: (i, 0)),
            pl.BlockSpec((D_MODEL, TF_FFN), lambda i, f: (0, f)),
            pl.BlockSpec((D_MODEL, TF_FFN), lambda i, f: (0, f + n_f)),
            pl.BlockSpec((CONV_WIDTH, TF_FFN), lambda i, f: (0, f)),
            pl.BlockSpec((CONV_WIDTH, TF_FFN), lambda i, f: (0, f + n_f)),
            pl.BlockSpec((1, TF_FFN), lambda i, f: (0, f)),
            pl.BlockSpec((1, TF_FFN), lambda i, f: (0, f + n_f)),
            pl.BlockSpec((TF_FFN, D_MODEL), lambda i, f: (f, 0)),
        ],
        out_specs=pl.BlockSpec((TM_FFN, D_MODEL), lambda i, f: (i, 0)),
        out_shape=jax.ShapeDtypeStruct((m, D_MODEL), F32),
        scratch_shapes=[pltpu.VMEM((n_f, 2 * (TF_FFN // TF_HALF), SUBLANES, TF_HALF), F32)]
        + [pltpu.VMEM((TM_FFN + SUBLANES, TF_HALF), F32)] * (2 * (TF_FFN // TF_HALF)),
        compiler_params=pltpu.CompilerParams(
            dimension_semantics=("arbitrary", "arbitrary"), vmem_limit_bytes=VMEM_LIMIT),
        name="ffn",
    )(h2, x1, w_up, w_up, conv_w, conv_w, conv_b, conv_b, w_down)


def _ple_kernel(x_ref, p_ref, gp_ref, wgate32_ref, wproj_ref, gf_ref, o_ref, wgate_ref, *, last_layer):
    @pl.when(pl.program_id(0) == 0)
    def _():
        for r in range(0, D_MODEL, CAST_ROWS):
            wgate_ref[r:r + CAST_ROWS, :] = wgate32_ref[r:r + CAST_ROWS, :].astype(BF16)

    x = x_ref[...]
    hg = _rms(x, gp_ref[...]).astype(BF16)
    gate = jax.nn.sigmoid(jnp.dot(hg, wgate_ref[...], preferred_element_type=F32))
    emb = jnp.dot(p_ref[...].astype(BF16), wproj_ref[...], preferred_element_type=F32)
    x3 = x + emb * gate
    o_ref[...] = _rms(x3, gf_ref[...]) if last_layer else x3


def _ple(x2, p2d, g_ple, w_gate, w_proj, g_final, last_layer):
    m = x2.shape[0]
    return pl.pallas_call(
        functools.partial(_ple_kernel, last_layer=last_layer),
        grid=(m // TM_PLE,),
        in_specs=[
            pl.BlockSpec((TM_PLE, D_MODEL), lambda i: (i, 0)),
            pl.BlockSpec((TM_PLE, D_PLE), lambda i: (i, 0)),
            pl.BlockSpec((1, D_MODEL), lambda i: (0, 0)),
            pl.BlockSpec((D_MODEL, D_MODEL), lambda i: (0, 0), pipeline_mode=pl.Buffered(1)),
            pl.BlockSpec((D_PLE, D_MODEL), lambda i: (0, 0)),
            pl.BlockSpec((1, D_MODEL), lambda i: (0, 0)),
        ],
        out_specs=pl.BlockSpec((TM_PLE, D_MODEL), lambda i: (i, 0)),
        out_shape=jax.ShapeDtypeStruct((m, D_MODEL), F32),
        scratch_shapes=[pltpu.VMEM((D_MODEL, D_MODEL), BF16)],
        compiler_params=pltpu.CompilerParams(
            dimension_semantics=("arbitrary",), vmem_limit_bytes=VMEM_LIMIT),
        name="ple",
    )(x2, p2d, g_ple, w_gate, w_proj, g_final)


def kernel(x, p, g_mix, w_in, w_pool, pool_scale, rel_bias, w_out, g_ffn, w_up, conv_w, conv_b,
           w_down, g_ple, w_ple_gate, w_ple_proj, g_final):
    bsz, seq, _ = x.shape
    depth = w_in.shape[0]
    assert seq % QB == 0 and seq % TM_MIX == 0 and seq % TM_FFN == 0 and (bsz * seq) % TM_IN == 0
    x2d = x.reshape(bsz * seq, D_MODEL)
    for i in range(depth):
        u_pool, k2d, qT, qiT, vT4, small = _inproj(x2d, g_mix[i].reshape(1, D_MODEL),
                                                   w_in[i].T.astype(BF16), bsz, seq)
        small3 = small.reshape(bsz, seq, SMALL_W)
        kidx = small3[:, :, :IDX_DIM].astype(BF16)
        wT = jnp.swapaxes(small3[:, :, IDX_DIM:IDX_DIM + IDX_HEADS], 1, 2)
        attn, w_out_b, w_up_b, w_down_b = _dsa(rel_bias, qT, qiT, wT, kidx,
                                               k2d.reshape(bsz, seq, ATTN_WIDTH), vT4, seq,
                                               (w_out[i], w_up[i], w_down[i]))

        x1, h2 = _mix(x2d, u_pool, attn.reshape(bsz * seq, ATTN_WIDTH), w_pool[i].astype(BF16),
                      pool_scale[i].reshape(1, POOL_WIDTH), w_out_b,
                      g_ffn[i].reshape(1, D_MODEL), seq)
        x2 = _ffn(h2, x1, w_up_b, conv_w[i], conv_b[i].reshape(1, 2 * D_FF), w_down_b, seq)
        x2d = _ple(x2, p[i].reshape(bsz * seq, D_PLE), g_ple[i].reshape(1, D_MODEL),
                   w_ple_gate[i], w_ple_proj[i].astype(BF16),
                   g_final.reshape(1, D_MODEL), last_layer=(i == depth - 1))
    return x2d.reshape(bsz, seq, D_MODEL)
```

```python
import functools
import math

import numpy as np
import jax
import jax.numpy as jnp
from jax import lax
from jax.experimental import pallas as pl
from jax.experimental.pallas import tpu as pltpu

F32 = jnp.float32
BF16 = jnp.bfloat16
I32 = jnp.int32
I16 = jnp.int16

D_MODEL = 2048
D_PLE = 256
POOL_WIDTH = 1024
POOL_GROUPS = 4
POOL_GC = 256
POOL_WINDOWS = (2, 4, 8, 16)
ATTN_HEADS = 8
HEAD_DIM = 128
ATTN_WIDTH = 1024
IDX_HEADS = 16
IDX_DIM = 64
TOPK_MAX = 256
N_BUCKETS = 32
MAX_DISTANCE = 128
D_FF = 5632
CONV_WIDTH = 3
EPS = 1e-6
IN_SIZES = (POOL_WIDTH, ATTN_WIDTH, ATTN_WIDTH, ATTN_WIDTH, IDX_HEADS * IDX_DIM, IDX_DIM, IDX_HEADS)
G_POOL, G_Q, G_K, G_V, G_QI = range(5)
N_GROUPS = 5
GROUP_W = 1024

V7X_VMEM_BYTES = 64 * 1024 * 1024
SUBLANES = 8
PACKED_SUBLANES = 16
LANES = 128
VMEM_LIMIT = V7X_VMEM_BYTES - 8 * 1024 * 1024

TM_IN = 512
SMALL_W = IDX_DIM + IDX_HEADS
QB = 256
KC = 256
TM_MIX = 512
POOL_HALO = 16
TM_FFN = 512
TF_FFN = 512
TF_HALF = 256
TM_PLE = 512
CAST_ROWS = 256

INT_MIN = -2 ** 31
I16_MIN = -2 ** 15
I16_MAX = 2 ** 15 - 1
NEG_BIG = -1e30
LOG2E = math.log2(math.e)
Q_PRESCALE = (HEAD_DIM ** -0.5) * LOG2E


def _bucket_lower_bounds():
    n = np.arange(0, 4 * MAX_DISTANCE, dtype=np.int64)
    max_exact = N_BUCKETS // 2
    nf = np.maximum(n, 1).astype(np.float32)
    large = max_exact + (np.log(nf / np.float32(max_exact)) / np.float32(math.log(MAX_DISTANCE / max_exact))
                         * np.float32(N_BUCKETS - max_exact)).astype(np.int32)
    large = np.minimum(large, N_BUCKETS - 1)
    bucket = np.where(n < max_exact, n, large)
    assert np.all(np.diff(bucket) >= 0)
    lows = [int(np.argmax(bucket == b)) for b in range(N_BUCKETS)]
    assert all(bucket[lo] == b for b, lo in enumerate(lows))
    return lows


BUCKET_LO = _bucket_lower_bounds()


def _rms(x, g):
    ms = jnp.mean(x * x, axis=-1, keepdims=True)
    return (x * lax.rsqrt(ms + EPS)) * g


def _nt_dot(a, b):
    return lax.dot_general(a, b, (((1,), (1,)), ((), ())), preferred_element_type=F32)


def _inproj_kernel(x_ref, g_ref, wt_ref, pool_ref, k_ref, qT_ref, qiT_ref, vT_ref, small_ref):
    hb = _rms(x_ref[...], g_ref[...]).astype(BF16)

    def w(group):
        return wt_ref[group * GROUP_W:(group + 1) * GROUP_W, :]

    pool_ref[...] = _nt_dot(hb, w(G_POOL))
    k_ref[...] = _nt_dot(hb, w(G_K)).astype(BF16)
    small_ref[...] = _nt_dot(hb, wt_ref[N_GROUPS * GROUP_W:, :])
    qT_ref[...] = (_nt_dot(w(G_Q), hb) * Q_PRESCALE).astype(BF16)
    qiT_ref[...] = _nt_dot(w(G_QI), hb).astype(BF16)
    out_t = _nt_dot(w(G_V), hb).astype(BF16)
    for j in range(TM_IN // KC):
        vT_ref[j] = out_t[:, j * KC:(j + 1) * KC]


def _inproj(x2d, g, wt, bsz, seq):
    m = x2d.shape[0]
    tiles_per_seq = seq // TM_IN
    assert wt.shape[0] == N_GROUPS * GROUP_W + SMALL_W
    return pl.pallas_call(
        _inproj_kernel,
        grid=(m // TM_IN,),
        in_specs=[
            pl.BlockSpec((TM_IN, D_MODEL), lambda i: (i, 0)),
            pl.BlockSpec((1, D_MODEL), lambda i: (0, 0)),
            pl.BlockSpec(wt.shape, lambda i: (0, 0), pipeline_mode=pl.Buffered(1)),
        ],
        out_specs=[
            pl.BlockSpec((TM_IN, GROUP_W), lambda i: (i, 0)),
            pl.BlockSpec((TM_IN, GROUP_W), lambda i: (i, 0)),
            pl.BlockSpec((None, GROUP_W, TM_IN), lambda i: (i // tiles_per_seq, 0, i % tiles_per_seq)),
            pl.BlockSpec((None, GROUP_W, TM_IN), lambda i: (i // tiles_per_seq, 0, i % tiles_per_seq)),
            pl.BlockSpec((None, TM_IN // KC, GROUP_W, KC),
                         lambda i: (i // tiles_per_seq, i % tiles_per_seq, 0, 0)),
            pl.BlockSpec((TM_IN, SMALL_W), lambda i: (i, 0)),
        ],
        out_shape=[
            jax.ShapeDtypeStruct((m, POOL_WIDTH), F32),
            jax.ShapeDtypeStruct((m, ATTN_WIDTH), BF16),
            jax.ShapeDtypeStruct((bsz, ATTN_WIDTH, seq), BF16),
            jax.ShapeDtypeStruct((bsz, IDX_HEADS * IDX_DIM, seq), BF16),
            jax.ShapeDtypeStruct((bsz, seq // KC, ATTN_WIDTH, KC), BF16),
            jax.ShapeDtypeStruct((m, SMALL_W), F32),
        ],
        compiler_params=pltpu.CompilerParams(
            dimension_semantics=("arbitrary",), vmem_limit_bytes=VMEM_LIMIT),
        name="inproj",
    )(x2d, g, wt)


def _dsa_kernel(rel_ref, qT_ref, qiT_ref, wT_ref, qiTn_ref, wTn_ref, kidx_ref, k_ref, vT_ref, *rest,
                topk, n_q, n_side):
    side_in, rest = rest[:n_side], rest[n_side:]
    o_ref, side_out = rest[0], rest[1:1 + n_side]
    hi_s, lo_s, madd_s, bias_s, mx_s, alpha_s, acc_s, s_s = rest[1 + n_side:]
    b = pl.program_id(0)
    i = pl.program_id(1)
    n_ch = i + 1

    for src, dst in zip(side_in, side_out):
        dst[...] = src[...].astype(dst.dtype)

    @pl.when((b == 0) & (i == 0))
    def _():
        r = lax.broadcasted_iota(I32, (KC, QB), 0)
        c = lax.broadcasted_iota(I32, (KC, QB), 1)
        for off in range(2):
            dist = jnp.maximum(c - r + off * QB, 0)

            def head_body(h, carry, dist=dist, off=off):
                t = jnp.full((KC, QB), rel_ref[0, h], F32)
                for bk in range(1, N_BUCKETS):
                    t = jnp.where(dist >= BUCKET_LO[bk], rel_ref[bk, h], t)
                bias_s[off, h] = (t - rel_ref[N_BUCKETS - 1, h]) * LOG2E
                return carry

            lax.fori_loop(0, ATTN_HEADS, head_body, 0)

    idx_scale = (IDX_HEADS ** -0.5) * (IDX_DIM ** -0.5)
    row = lax.broadcasted_iota(I32, (KC, QB), 0)
    col = lax.broadcasted_iota(I32, (KC, QB), 1)

    def index_chunk(m, qi_ref, w_scaled, q_block):
        kc = kidx_ref[pl.ds(pl.multiple_of(m * KC, KC), KC), :]
        acc = jnp.zeros((KC, QB), F32)
        for h in range(IDX_HEADS):
            d = jnp.dot(kc, qi_ref[h * IDX_DIM:(h + 1) * IDX_DIM, :],
                        preferred_element_type=F32)
            acc = acc + w_scaled[h:h + 1, :] * jnp.maximum(d, 0.0)
        bits = pltpu.bitcast(acc, I32)
        key = jnp.where(bits >= 0, bits, bits ^ 0x7FFFFFFF)
        causal = (m * KC + row) <= (q_block * QB + col)
        key = jnp.where(causal, key, INT_MIN)
        hi_s[m] = (key >> 16).astype(I16)
        lo_s[m] = ((key & 0xFFFF) + I16_MIN).astype(I16)

    @pl.when(i == 0)
    def _():
        index_chunk(0, qiT_ref, wT_ref[...] * idx_scale, 0)

    def count_ge(ref, cand):
        cand16 = cand.astype(I16)

        def cnt_body(m, accs):
            ge = (ref[m] >= cand16).astype(I16)
            accs = list(accs)
            for j in range(KC // PACKED_SUBLANES):
                a = j % len(accs)
                accs[a] = accs[a] + ge[j * PACKED_SUBLANES:(j + 1) * PACKED_SUBLANES, :]
            return tuple(accs)

        zero = jnp.zeros((PACKED_SUBLANES, QB), I16)
        accs = lax.fori_loop(0, n_ch, cnt_body, (zero,) * 4)
        tot = (accs[0] + accs[1]) + (accs[2] + accs[3])
        return jnp.sum(tot.astype(I32), axis=0, keepdims=True)

    def descend(ref, want):
        def bit_body(it, prefix):
            cand = prefix + jnp.left_shift(jnp.int32(1), 15 - it)
            return jnp.where(count_ge(ref, cand) >= want, cand, prefix)

        return lax.fori_loop(0, 16, bit_body, jnp.full((1, QB), I16_MIN, I32))

    p_hi = descend(hi_s, topk)
    n_above = jnp.where(p_hi >= I16_MAX, 0, count_ge(hi_s, jnp.minimum(p_hi + 1, I16_MAX)))
    hi_eq = jnp.where(p_hi > I16_MIN, p_hi, I16_MAX).astype(I16)

    def member_body(m, carry):
        lo_s[m] = jnp.where(hi_s[m] == hi_eq, lo_s[m], I16_MIN)
        return carry

    lax.fori_loop(0, n_ch, member_body, 0)
    p_lo = descend(lo_s, topk - n_above)
    p_hi16 = p_hi.astype(I16)
    p_lo16 = p_lo.astype(I16)

    def packed_count(acc, flags):
        f16 = flags.astype(I16)
        for j in range(KC // PACKED_SUBLANES):
            acc = acc + f16[j * PACKED_SUBLANES:(j + 1) * PACKED_SUBLANES, :]
        return acc

    def total(acc16):
        return jnp.sum(acc16.astype(I32), axis=0, keepdims=True)

    zero16 = jnp.zeros((PACKED_SUBLANES, QB), I16)

    def mask_body(m, kept):
        hi = hi_s[m]
        keep = (hi > p_hi16) | ((hi == hi_eq) & (lo_s[m] >= p_lo16))
        madd_s[m] = jnp.where(keep, jnp.bfloat16(0.0), jnp.bfloat16(NEG_BIG)).astype(F32)
        return packed_count(kept, keep)

    excess = total(lax.fori_loop(0, n_ch, mask_body, zero16)) - topk

    @pl.when(jnp.max(excess) > 0)
    def _():
        def tied(m):
            return (hi_s[m] == hi_eq) & (lo_s[m] == p_lo16)

        n_tied = total(lax.fori_loop(0, n_ch, lambda m, acc: packed_count(acc, tied(m)), zero16))
        need = (n_tied - excess).astype(F32)
        r = lax.broadcasted_iota(I32, (KC, KC), 0)
        c = lax.broadcasted_iota(I32, (KC, KC), 1)
        prefix_op = jnp.where(c <= r, 1.0, 0.0).astype(BF16)

        def fix_body(m, seen):
            t = jnp.where(tied(m), jnp.bfloat16(1.0), jnp.bfloat16(0.0))
            rank = seen + jnp.dot(prefix_op, t, preferred_element_type=F32)
            drop = (t.astype(F32) > 0.0) & (rank > need)
            madd_s[m] = jnp.where(drop, NEG_BIG, madd_s[m])
            return rank[KC - 1:KC, :]

        lax.fori_loop(0, n_ch, fix_body, jnp.zeros((1, QB), F32))

    n_far = jnp.maximum(i - 1, 0)
    mx_s[...] = jnp.full(mx_s.shape, -jnp.inf, F32)
    acc_s[...] = jnp.zeros(acc_s.shape, F32)

    w_next = wTn_ref[...] * idx_scale

    def att_body(m, carry, near):
        index_chunk(m, qiTn_ref, w_next, i + 1)
        madd = madd_s[m]
        rows = pl.ds(pl.multiple_of(m * KC, KC), KC)
        for h in range(ATTN_HEADS):
            hd = slice(h * HEAD_DIM, (h + 1) * HEAD_DIM)
            s = jnp.dot(k_ref[rows, hd], qT_ref[hd, :], preferred_element_type=F32)
            if near:
                s = s + (madd + bias_s[i - m, h])
            else:
                s = s + madd
            s_s[h] = s
            mx = mx_s[h]
            m_new = jnp.maximum(mx, jnp.max(s, axis=0, keepdims=True))
            alpha_s[h] = jnp.exp2(mx - m_new)
            mx_s[h] = m_new
        ones_rows = jnp.ones((PACKED_SUBLANES, KC), BF16)
        for h in range(ATTN_HEADS):
            hd = slice(h * HEAD_DIM, (h + 1) * HEAD_DIM)
            p = jnp.exp2(s_s[h] - mx_s[h])
            v_aug = jnp.concatenate([vT_ref[m, hd, :], ones_rows], axis=0)
            pv = jnp.dot(v_aug, p.astype(BF16), preferred_element_type=F32)
            acc_s[h] = alpha_s[h] * acc_s[h] + pv
        return carry

    def far_pair(pair, carry):
        att_body(2 * pair, carry, near=False)
        att_body(2 * pair + 1, carry, near=False)
        return carry

    lax.fori_loop(0, n_far // 2, far_pair, 0)

    @pl.when(n_far % 2 == 1)
    def _():
        att_body(n_far - 1, 0, near=False)

    @pl.when(i >= 1)
    def _():
        att_body(i - 1, 0, near=True)

    att_body(i, 0, near=True)
    index_chunk(jnp.minimum(i + 1, n_q - 1), qiTn_ref, w_next, i + 1)

    for h in range(ATTN_HEADS):
        acc = acc_s[h]
        out_t = acc[:HEAD_DIM] / acc[HEAD_DIM:HEAD_DIM + 1]
        o_ref[:, h * HEAD_DIM:(h + 1) * HEAD_DIM] = out_t.T.astype(o_ref.dtype)


def _dsa(rel_bias, qT, qiT, wT, kidx, k, vT4, seq, side_weights):
    bsz = qT.shape[0]
    topk = min(TOPK_MAX, seq // 4)
    n_q = seq // QB
    n_steps = bsz * n_q
    kernel = functools.partial(_dsa_kernel, topk=topk, n_q=n_q, n_side=len(side_weights))
    nxt = lambda i: jnp.minimum(i + 1, n_q - 1)
    side_specs = []
    for w in side_weights:
        rows = w.shape[0] // n_steps
        assert rows * n_steps == w.shape[0] and rows % PACKED_SUBLANES == 0
        side_specs.append(pl.BlockSpec((rows, w.shape[1]), lambda b, i: (b * n_q + i, 0)))
    resident = dict(pipeline_mode=pl.Buffered(1))
    return pl.pallas_call(
        kernel,
        grid=(bsz, n_q),
        in_specs=[
            pl.BlockSpec(memory_space=pltpu.SMEM),
            pl.BlockSpec((None, ATTN_WIDTH, QB), lambda b, i: (b, 0, i)),
            pl.BlockSpec((None, IDX_HEADS * IDX_DIM, QB), lambda b, i: (b, 0, i)),
            pl.BlockSpec((None, IDX_HEADS, QB), lambda b, i: (b, 0, i)),
            pl.BlockSpec((None, IDX_HEADS * IDX_DIM, QB), lambda b, i: (b, 0, nxt(i))),
            pl.BlockSpec((None, IDX_HEADS, QB), lambda b, i: (b, 0, nxt(i))),
            pl.BlockSpec((None, seq, IDX_DIM), lambda b, i: (b, 0, 0)),
            pl.BlockSpec((None, seq, ATTN_WIDTH), lambda b, i: (b, 0, 0), **resident),
            pl.BlockSpec((None, seq // KC, ATTN_WIDTH, KC), lambda b, i: (b, 0, 0, 0), **resident),
        ] + side_specs,
        out_specs=[pl.BlockSpec((None, QB, ATTN_WIDTH), lambda b, i: (b, i, 0))] + side_specs,
        out_shape=[jax.ShapeDtypeStruct((bsz, seq, ATTN_WIDTH), BF16)]
        + [jax.ShapeDtypeStruct(w.shape, BF16) for w in side_weights],
        scratch_shapes=[
            pltpu.VMEM((seq // KC, KC, QB), I16),
            pltpu.VMEM((seq // KC, KC, QB), I16),
            pltpu.VMEM((seq // KC, KC, QB), F32),
            pltpu.VMEM((2, ATTN_HEADS, KC, QB), F32),
            pltpu.VMEM((ATTN_HEADS, 1, QB), F32),
            pltpu.VMEM((ATTN_HEADS, 1, QB), F32),
            pltpu.VMEM((ATTN_HEADS, HEAD_DIM + PACKED_SUBLANES, QB), F32),
            pltpu.VMEM((ATTN_HEADS, KC, QB), F32),
        ],
        compiler_params=pltpu.CompilerParams(
            dimension_semantics=("arbitrary", "arbitrary"), vmem_limit_bytes=VMEM_LIMIT),
        name="dsa",
    )(rel_bias, qT, qiT, wT, qiT, wT, kidx, k, vT4, *side_weights)


def _mix_kernel(x_ref, u_ref, halo_ref, attn_ref, wpool_ref, pscale_ref, wout_ref, g_ref,
                x1_ref, h2_ref, ext_s, *, seq):
    m = pl.program_id(0)
    tiles_per_seq = seq // TM_MIX
    first = (m % tiles_per_seq) == 0
    ext_s[0:POOL_HALO, :] = jnp.where(first, 0.0, halo_ref[...])
    ext_s[POOL_HALO:, :] = u_ref[...]

    t_pos = (m % tiles_per_seq) * TM_MIX + lax.broadcasted_iota(I32, (TM_MIX, POOL_GC), 0)
    acc = x_ref[...] + jnp.dot(attn_ref[...], wout_ref[POOL_WIDTH:, :], preferred_element_type=F32)
    for g, w in enumerate(POOL_WINDOWS):
        cols = slice(g * POOL_GC, (g + 1) * POOL_GC)
        u = ext_s[POOL_HALO:, cols]
        wsum = u
        for j in range(1, w):
            wsum = wsum + ext_s[pl.ds(POOL_HALO - j, TM_MIX), cols]
        count = jnp.minimum(t_pos + 1, w).astype(F32)
        pooled = (wsum / count - u).astype(BF16)
        y = jnp.dot(pooled, wpool_ref[g], preferred_element_type=F32) * pscale_ref[:, cols]
        acc = acc + jnp.dot(y.astype(BF16), wout_ref[cols, :], preferred_element_type=F32)
    x1_ref[...] = acc
    h2_ref[...] = _rms(acc, g_ref[...]).astype(BF16)


def _mix(x2d, u_pool, attn2d, w_pool, pool_scale, w_out, g_ffn, seq):
    m = x2d.shape[0]
    halo_per_tile = TM_MIX // POOL_HALO
    kernel = functools.partial(_mix_kernel, seq=seq)
    return pl.pallas_call(
        kernel,
        grid=(m // TM_MIX,),
        in_specs=[
            pl.BlockSpec((TM_MIX, D_MODEL), lambda i: (i, 0)),
            pl.BlockSpec((TM_MIX, POOL_WIDTH), lambda i: (i, 0)),
            pl.BlockSpec((POOL_HALO, POOL_WIDTH), lambda i: (jnp.maximum(i * halo_per_tile - 1, 0), 0)),
            pl.BlockSpec((TM_MIX, ATTN_WIDTH), lambda i: (i, 0)),
            pl.BlockSpec((POOL_GROUPS, POOL_GC, POOL_GC), lambda i: (0, 0, 0)),
            pl.BlockSpec((1, POOL_WIDTH), lambda i: (0, 0)),
            pl.BlockSpec((D_MODEL, D_MODEL), lambda i: (0, 0)),
            pl.BlockSpec((1, D_MODEL), lambda i: (0, 0)),
        ],
        out_specs=[
            pl.BlockSpec((TM_MIX, D_MODEL), lambda i: (i, 0)),
            pl.BlockSpec((TM_MIX, D_MODEL), lambda i: (i, 0)),
        ],
        out_shape=[
            jax.ShapeDtypeStruct((m, D_MODEL), F32),
            jax.ShapeDtypeStruct((m, D_MODEL), BF16),
        ],
        scratch_shapes=[pltpu.VMEM((TM_MIX + POOL_HALO, POOL_WIDTH), F32)],
        compiler_params=pltpu.CompilerParams(
            dimension_semantics=("arbitrary",), vmem_limit_bytes=VMEM_LIMIT),
        name="mix",
    )(x2d, u_pool, u_pool, attn2d, w_pool, pool_scale, w_out, g_ffn)


def _ffn_kernel(h_ref, x1_ref, wg_ref, wv_ref, cwg_ref, cwv_ref, cbg_ref, cbv_ref, wd_ref,
                o_ref, carry_s, *u_bufs, seq):
    m = pl.program_id(0)
    f = pl.program_id(1)
    tiles_per_seq = seq // TM_FFN
    first = (m % tiles_per_seq) == 0

    @pl.when(f == 0)
    def _():
        o_ref[...] = x1_ref[...]

    h = h_ref[...]
    n_half = TF_FFN // TF_HALF

    def up(s):
        cols = slice(s * TF_HALF, (s + 1) * TF_HALF)
        for k, w_ref in enumerate((wg_ref, wv_ref)):
            u = u_bufs[2 * s + k]
            u[0:SUBLANES, :] = jnp.where(first, 0.0, carry_s[f, 2 * s + k])
            u[SUBLANES:, :] = jnp.dot(h, w_ref[:, cols], preferred_element_type=F32)
            carry_s[f, 2 * s + k] = u[TM_FFN:, :]

    def conv(u, cols, cw_ref, cb_ref):
        out = cb_ref[:, cols]
        for j in range(CONV_WIDTH):
            shift = CONV_WIDTH - 1 - j
            out = out + u[pl.ds(SUBLANES - shift, TM_FFN), :] * cw_ref[j:j + 1, cols]
        return out

    def gated(s):
        cols = slice(s * TF_HALF, (s + 1) * TF_HALF)
        gate = conv(u_bufs[2 * s], cols, cwg_ref, cbg_ref)
        val = conv(u_bufs[2 * s + 1], cols, cwv_ref, cbv_ref)
        return (gate * jax.nn.sigmoid(gate) * val).astype(BF16)

    def down(s, act):
        return jnp.dot(act, wd_ref[s * TF_HALF:(s + 1) * TF_HALF, :], preferred_element_type=F32)

    up(0)
    act0 = gated(0)
    up(1)
    contrib = down(0, act0)
    act1 = gated(1)
    contrib = contrib + down(1, act1)
    o_ref[...] += contrib


def _ffn(h2, x1, w_up, conv_w, conv_b, w_down, seq):
    m = h2.shape[0]
    n_f = D_FF // TF_FFN
    kernel = functools.partial(_ffn_kernel, seq=seq)
    return pl.pallas_call(
        kernel,
        grid=(m // TM_FFN, n_f),
        in_specs=[
            pl.BlockSpec((TM_FFN, D_MODEL), lambda i, f: (i, 0)),
            pl.BlockSpec((TM_FFN, D_MODEL), lambda i, f: (i, 0)),
            pl.BlockSpec((D_MODEL, TF_FFN), lambda i, f: (0, f)),
            pl.BlockSpec((D_MODEL, TF_FFN), lambda i, f: (0, f + n_f)),
            pl.BlockSpec((CONV_WIDTH, TF_FFN), lambda i, f: (0, f)),
            pl.BlockSpec((CONV_WIDTH, TF_FFN), lambda i, f: (0, f + n_f)),
            pl.BlockSpec((1, TF_FFN), lambda i, f: (0, f)),
            pl.BlockSpec((1, TF_FFN), lambda i, f: (0, f + n_f)),
            pl.BlockSpec((TF_FFN, D_MODEL), lambda i, f: (f, 0)),
        ],
        out_specs=pl.BlockSpec((TM_FFN, D_MODEL), lambda i, f: (i, 0)),
        out_shape=jax.ShapeDtypeStruct((m, D_MODEL), F32),
        scratch_shapes=[pltpu.VMEM((n_f, 2 * (TF_FFN // TF_HALF), SUBLANES, TF_HALF), F32)]
        + [pltpu.VMEM((TM_FFN + SUBLANES, TF_HALF), F32)] * (2 * (TF_FFN // TF_HALF)),
        compiler_params=pltpu.CompilerParams(
            dimension_semantics=("arbitrary", "arbitrary"), vmem_limit_bytes=VMEM_LIMIT),
        name="ffn",
    )(h2, x1, w_up, w_up, conv_w, conv_w, conv_b, conv_b, w_down)


def _ple_kernel(x_ref, p_ref, gp_ref, wgate32_ref, wproj_ref, gf_ref, o_ref, wgate_ref, *, last_layer):
    @pl.when(pl.program_id(0) == 0)
    def _():
        for r in range(0, D_MODEL, CAST_ROWS):
            wgate_ref[r:r + CAST_ROWS, :] = wgate32_ref[r:r + CAST_ROWS, :].astype(BF16)

    x = x_ref[...]
    hg = _rms(x, gp_ref[...]).astype(BF16)
    gate = jax.nn.sigmoid(jnp.dot(hg, wgate_ref[...], preferred_element_type=F32))
    emb = jnp.dot(p_ref[...].astype(BF16), wproj_ref[...], preferred_element_type=F32)
    x3 = x + emb * gate
    o_ref[...] = _rms(x3, gf_ref[...]) if last_layer else x3


def _ple(x2, p2d, g_ple, w_gate, w_proj, g_final, last_layer):
    m = x2.shape[0]
    return pl.pallas_call(
        functools.partial(_ple_kernel, last_layer=last_layer),
        grid=(m // TM_PLE,),
        in_specs=[
            pl.BlockSpec((TM_PLE, D_MODEL), lambda i: (i, 0)),
            pl.BlockSpec((TM_PLE, D_PLE), lambda i: (i, 0)),
            pl.BlockSpec((1, D_MODEL), lambda i: (0, 0)),
            pl.BlockSpec((D_MODEL, D_MODEL), lambda i: (0, 0), pipeline_mode=pl.Buffered(1)),
            pl.BlockSpec((D_PLE, D_MODEL), lambda i: (0, 0)),
            pl.BlockSpec((1, D_MODEL), lambda i: (0, 0)),
        ],
        out_specs=pl.BlockSpec((TM_PLE, D_MODEL), lambda i: (i, 0)),
        out_shape=jax.ShapeDtypeStruct((m, D_MODEL), F32),
        scratch_shapes=[pltpu.VMEM((D_MODEL, D_MODEL), BF16)],
        compiler_params=pltpu.CompilerParams(
            dimension_semantics=("arbitrary",), vmem_limit_bytes=VMEM_LIMIT),
        name="ple",
    )(x2, p2d, g_ple, w_gate, w_proj, g_final)


def kernel(x, p, g_mix, w_in, w_pool, pool_scale, rel_bias, w_out, g_ffn, w_up, conv_w, conv_b,
           w_down, g_ple, w_ple_gate, w_ple_proj, g_final):
    bsz, seq, _ = x.shape
    depth = w_in.shape[0]
    assert seq % QB == 0 and seq % TM_MIX == 0 and seq % TM_FFN == 0 and (bsz * seq) % TM_IN == 0
    x2d = x.reshape(bsz * seq, D_MODEL)
    for i in range(depth):
        u_pool, k2d, qT, qiT, vT4, small = _inproj(x2d, g_mix[i].reshape(1, D_MODEL),
                                                   w_in[i].T.astype(BF16), bsz, seq)
        small3 = small.reshape(bsz, seq, SMALL_W)
        kidx = small3[:, :, :IDX_DIM].astype(BF16)
        wT = jnp.swapaxes(small3[:, :, IDX_DIM:IDX_DIM + IDX_HEADS], 1, 2)
        attn, w_out_b, w_up_b, w_down_b = _dsa(rel_bias, qT, qiT, wT, kidx,
                                               k2d.reshape(bsz, seq, ATTN_WIDTH), vT4, seq,
                                               (w_out[i], w_up[i], w_down[i]))

        x1, h2 = _mix(x2d, u_pool, attn.reshape(bsz * seq, ATTN_WIDTH), w_pool[i].astype(BF16),
                      pool_scale[i].reshape(1, POOL_WIDTH), w_out_b,
                      g_ffn[i].reshape(1, D_MODEL), seq)
        x2 = _ffn(h2, x1, w_up_b, conv_w[i], conv_b[i].reshape(1, 2 * D_FF), w_down_b, seq)
        x2d = _ple(x2, p[i].reshape(bsz * seq, D_PLE), g_ple[i].reshape(1, D_MODEL),
                   w_ple_gate[i], w_ple_proj[i].astype(BF16),
                   g_final.reshape(1, D_MODEL), last_layer=(i == depth - 1))
    return x2d.reshape(bsz, seq, D_MODEL)
```

```python
import functools
import math

import numpy as np
import jax
import jax.numpy as jnp
from jax import lax
from jax.experimental import pallas as pl
from jax.experimental.pallas import tpu as pltpu

F32 = jnp.float32
BF16 = jnp.bfloat16
I32 = jnp.int32
I16 = jnp.int16

D_MODEL = 2048
D_PLE = 256
POOL_WIDTH = 1024
POOL_GROUPS = 4
POOL_GC = 256
POOL_WINDOWS = (2, 4, 8, 16)
ATTN_HEADS = 8
HEAD_DIM = 128
ATTN_WIDTH = 1024
IDX_HEADS = 16
IDX_DIM = 64
TOPK_MAX = 256
N_BUCKETS = 32
MAX_DISTANCE = 128
D_FF = 5632
CONV_WIDTH = 3
EPS = 1e-6
G_POOL, G_Q, G_K, G_V, G_QI = range(5)
N_GROUPS = 5
GROUP_W = 1024

V7X_VMEM_BYTES = 64 * 1024 * 1024
SUBLANES = 8
PACKED_SUBLANES = 16
VMEM_LIMIT = V7X_VMEM_BYTES - 8 * 1024 * 1024

TM_IN = 512
SMALL_W = IDX_DIM + IDX_HEADS
QB = 256
KC = 256
TM_MIX = 512
POOL_HALO = 32
TM_FFN = 512
TF_FFN = 512
TF_HALF = 256
TM_PLE = 512
CAST_ROWS = 256

INT_MIN = -2 ** 31
I16_MIN = -2 ** 15
I16_MAX = 2 ** 15 - 1
NEG_BIG = -1e30
F32_MAX = float(np.finfo(np.float32).max)
LOG2E = math.log2(math.e)
Q_PRESCALE = (HEAD_DIM ** -0.5) * LOG2E


def _bucket_lower_bounds():
    n = np.arange(0, 4 * MAX_DISTANCE, dtype=np.int64)
    max_exact = N_BUCKETS // 2
    nf = np.maximum(n, 1).astype(np.float32)
    large = max_exact + (np.log(nf / np.float32(max_exact)) / np.float32(math.log(MAX_DISTANCE / max_exact))
                         * np.float32(N_BUCKETS - max_exact)).astype(np.int32)
    large = np.minimum(large, N_BUCKETS - 1)
    bucket = np.where(n < max_exact, n, large)
    assert np.all(np.diff(bucket) >= 0)
    lows = [int(np.argmax(bucket == b)) for b in range(N_BUCKETS)]
    assert all(bucket[lo] == b for b, lo in enumerate(lows))
    return lows


BUCKET_LO = _bucket_lower_bounds()


def _rms(x, g):
    ms = jnp.mean(x * x, axis=-1, keepdims=True)
    return (x * lax.rsqrt(ms + EPS)) * g


def _nt_dot(a, b):
    return lax.dot_general(a, b, (((1,), (1,)), ((), ())), preferred_element_type=F32)


def _inproj_kernel(x_ref, g_ref, wt_ref, pool_ref, k_ref, qT_ref, qiT_ref, vT_ref, small_ref):
    hb = _rms(x_ref[...], g_ref[...]).astype(BF16)

    def w(group):
        return wt_ref[group * GROUP_W:(group + 1) * GROUP_W, :]

    pool_ref[...] = _nt_dot(hb, w(G_POOL))
    k_ref[...] = _nt_dot(hb, w(G_K)).astype(BF16)
    small_ref[...] = _nt_dot(hb, wt_ref[N_GROUPS * GROUP_W:, :])
    qT_ref[...] = (_nt_dot(w(G_Q), hb) * Q_PRESCALE).astype(BF16)
    qiT_ref[...] = _nt_dot(w(G_QI), hb).astype(BF16)
    out_t = _nt_dot(w(G_V), hb).astype(BF16)
    for j in range(TM_IN // KC):
        vT_ref[j] = out_t[:, j * KC:(j + 1) * KC]


def _inproj(x2d, g, wt, bsz, seq):
    m = x2d.shape[0]
    tiles_per_seq = seq // TM_IN
    assert wt.shape[0] == N_GROUPS * GROUP_W + SMALL_W
    return pl.pallas_call(
        _inproj_kernel,
        grid=(m // TM_IN,),
        in_specs=[
            pl.BlockSpec((TM_IN, D_MODEL), lambda i: (i, 0)),
            pl.BlockSpec((1, D_MODEL), lambda i: (0, 0)),
            pl.BlockSpec(wt.shape, lambda i: (0, 0), pipeline_mode=pl.Buffered(1)),
        ],
        out_specs=[
            pl.BlockSpec((TM_IN, GROUP_W), lambda i: (i, 0)),
            pl.BlockSpec((TM_IN, GROUP_W), lambda i: (i, 0)),
            pl.BlockSpec((None, GROUP_W, TM_IN), lambda i: (i // tiles_per_seq, 0, i % tiles_per_seq)),
            pl.BlockSpec((None, GROUP_W, TM_IN), lambda i: (i // tiles_per_seq, 0, i % tiles_per_seq)),
            pl.BlockSpec((None, TM_IN // KC, GROUP_W, KC),
                         lambda i: (i // tiles_per_seq, i % tiles_per_seq, 0, 0)),
            pl.BlockSpec((TM_IN, SMALL_W), lambda i: (i, 0)),
        ],
        out_shape=[
            jax.ShapeDtypeStruct((m, POOL_WIDTH), F32),
            jax.ShapeDtypeStruct((m, ATTN_WIDTH), BF16),
            jax.ShapeDtypeStruct((bsz, ATTN_WIDTH, seq), BF16),
            jax.ShapeDtypeStruct((bsz, IDX_HEADS * IDX_DIM, seq), BF16),
            jax.ShapeDtypeStruct((bsz, seq // KC, ATTN_WIDTH, KC), BF16),
            jax.ShapeDtypeStruct((m, SMALL_W), F32),
        ],
        compiler_params=pltpu.CompilerParams(
            dimension_semantics=("arbitrary",), vmem_limit_bytes=VMEM_LIMIT),
        name="inproj",
    )(x2d, g, wt)


def _dsa_kernel(rel_ref, qT_ref, qiT_ref, wT_ref, qiTn_ref, wTn_ref, kidx_ref, k_ref, vT_ref, *rest,
                topk, n_q, n_side):
    side_in, rest = rest[:n_side], rest[n_side:]
    o_ref, side_out = rest[0], rest[1:1 + n_side]
    sc_s, madd_s, bias_s, mx_s, alpha_s, acc_s, s_s = rest[1 + n_side:]
    b = pl.program_id(0)
    i = pl.program_id(1)
    n_ch = i + 1

    for src, dst in zip(side_in, side_out):
        dst[...] = src[...].astype(dst.dtype)

    @pl.when((b == 0) & (i == 0))
    def _():
        r = lax.broadcasted_iota(I32, (KC, QB), 0)
        c = lax.broadcasted_iota(I32, (KC, QB), 1)
        for off in range(2):
            dist = jnp.maximum(c - r + off * QB, 0)

            def head_body(h, carry, dist=dist, off=off):
                t = jnp.full((KC, QB), rel_ref[0, h], F32)
                for bk in range(1, N_BUCKETS):
                    t = jnp.where(dist >= BUCKET_LO[bk], rel_ref[bk, h], t)
                bias_s[off, h] = (t - rel_ref[N_BUCKETS - 1, h]) * LOG2E
                return carry

            lax.fori_loop(0, ATTN_HEADS, head_body, 0)

    idx_scale = (IDX_HEADS ** -0.5) * (IDX_DIM ** -0.5)
    row = lax.broadcasted_iota(I32, (KC, QB), 0)
    col = lax.broadcasted_iota(I32, (KC, QB), 1)

    def index_chunk(m, qi_ref, w_scaled, q_block):
        kc = kidx_ref[pl.ds(pl.multiple_of(m * KC, KC), KC), :]
        acc = jnp.zeros((KC, QB), F32)
        for h in range(IDX_HEADS):
            d = jnp.dot(kc, qi_ref[h * IDX_DIM:(h + 1) * IDX_DIM, :],
                        preferred_element_type=F32)
            acc = acc + w_scaled[h:h + 1, :] * jnp.maximum(d, 0.0)
        causal = (m * KC + row) <= (q_block * QB + col)
        sc_s[m] = jnp.where(causal, acc, -jnp.inf)

    @pl.when(i == 0)
    def _():
        index_chunk(0, qiT_ref, wT_ref[...] * idx_scale, 0)

    def code_to_score(code):
        bits = jnp.where(code >= 0, code, code ^ 0x7FFFFFFF)
        return pltpu.bitcast(bits, F32)

    def fold(acc, flags):
        return acc + jnp.sum(jnp.where(flags, 1.0, 0.0).reshape(KC // SUBLANES, SUBLANES, QB), axis=0)

    def total(acc):
        return jnp.sum(acc, axis=0, keepdims=True)

    zero8 = jnp.zeros((SUBLANES, QB), F32)

    def count_ge(cand):
        return total(lax.fori_loop(0, n_ch, lambda m, acc: fold(acc, sc_s[m] >= cand), zero8))

    def bit_body(it, prefix):
        cand = prefix + jnp.left_shift(jnp.int32(1), 31 - it)
        return jnp.where(count_ge(code_to_score(cand)) >= topk, cand, prefix)

    prefix = lax.fori_loop(0, 32, bit_body, jnp.full((1, QB), INT_MIN, I32))
    thr = jnp.where(prefix == INT_MIN, -F32_MAX, code_to_score(prefix))

    def mask_body(m, kept):
        keep = sc_s[m] >= thr
        madd_s[m] = jnp.where(keep, 0.0, NEG_BIG)
        return fold(kept, keep)

    excess = total(lax.fori_loop(0, n_ch, mask_body, zero8)) - topk

    @pl.when(jnp.max(excess) > 0)
    def _():
        n_tied = total(lax.fori_loop(0, n_ch, lambda m, acc: fold(acc, sc_s[m] == thr), zero8))
        need = n_tied - excess
        r = lax.broadcasted_iota(I32, (KC, KC), 0)
        c = lax.broadcasted_iota(I32, (KC, KC), 1)
        prefix_op = jnp.where(c <= r, 1.0, 0.0).astype(BF16)

        def fix_body(m, seen):
            tied = sc_s[m] == thr
            t = jnp.where(tied, 1.0, 0.0).astype(BF16)
            rank = seen + jnp.dot(prefix_op, t, preferred_element_type=F32)
            madd_s[m] = jnp.where(tied & (rank > need), NEG_BIG, madd_s[m])
            return rank[KC - 1:KC, :]

        lax.fori_loop(0, n_ch, fix_body, jnp.zeros((1, QB), F32))

    n_far = jnp.maximum(i - 1, 0)
    mx_s[...] = jnp.full(mx_s.shape, -jnp.inf, F32)
    acc_s[...] = jnp.zeros(acc_s.shape, F32)

    w_next = wTn_ref[...] * idx_scale

    def att_body(m, carry, near):
        index_chunk(m, qiTn_ref, w_next, i + 1)
        madd = madd_s[m]
        rows = pl.ds(pl.multiple_of(m * KC, KC), KC)
        for h in range(ATTN_HEADS):
            hd = slice(h * HEAD_DIM, (h + 1) * HEAD_DIM)
            s = jnp.dot(k_ref[rows, hd], qT_ref[hd, :], preferred_element_type=F32)
            if near:
                s = s + (madd + bias_s[i - m, h])
            else:
                s = s + madd
            s_s[h] = s
            mx = mx_s[h]
            m_new = jnp.maximum(mx, jnp.max(s, axis=0, keepdims=True))
            alpha_s[h] = jnp.exp2(mx - m_new)
            mx_s[h] = m_new
        ones_rows = jnp.ones((PACKED_SUBLANES, KC), BF16)
        for h in range(ATTN_HEADS):
            hd = slice(h * HEAD_DIM, (h + 1) * HEAD_DIM)
            p = jnp.exp2(s_s[h] - mx_s[h])
            v_aug = jnp.concatenate([vT_ref[m, hd, :], ones_rows], axis=0)
            pv = jnp.dot(v_aug, p.astype(BF16), preferred_element_type=F32)
            acc_s[h] = alpha_s[h] * acc_s[h] + pv
        return carry

    def far_pair(pair, carry):
        att_body(2 * pair, carry, near=False)
        att_body(2 * pair + 1, carry, near=False)
        return carry

    lax.fori_loop(0, n_far // 2, far_pair, 0)

    @pl.when(n_far % 2 == 1)
    def _():
        att_body(n_far - 1, 0, near=False)

    @pl.when(i >= 1)
    def _():
        att_body(i - 1, 0, near=True)

    att_body(i, 0, near=True)
    index_chunk(jnp.minimum(i + 1, n_q - 1), qiTn_ref, w_next, i + 1)

    for h in range(ATTN_HEADS):
        acc = acc_s[h]
        out_t = acc[:HEAD_DIM] / acc[HEAD_DIM:HEAD_DIM + 1]
        o_ref[:, h * HEAD_DIM:(h + 1) * HEAD_DIM] = out_t.T.astype(o_ref.dtype)


def _dsa(rel_bias, qT, qiT, wT, kidx, k, vT4, seq, side_weights):
    bsz = qT.shape[0]
    topk = min(TOPK_MAX, seq // 4)
    n_q = seq // QB
    n_steps = bsz * n_q
    kernel = functools.partial(_dsa_kernel, topk=topk, n_q=n_q, n_side=len(side_weights))
    nxt = lambda i: jnp.minimum(i + 1, n_q - 1)
    side_specs = []
    for w in side_weights:
        rows = w.shape[0] // n_steps
        assert rows * n_steps == w.shape[0] and rows % PACKED_SUBLANES == 0
        side_specs.append(pl.BlockSpec((rows, w.shape[1]), lambda b, i: (b * n_q + i, 0)))
    resident = dict(pipeline_mode=pl.Buffered(1))
    return pl.pallas_call(
        kernel,
        grid=(bsz, n_q),
        in_specs=[
            pl.BlockSpec(memory_space=pltpu.SMEM),
            pl.BlockSpec((None, ATTN_WIDTH, QB), lambda b, i: (b, 0, i)),
            pl.BlockSpec((None, IDX_HEADS * IDX_DIM, QB), lambda b, i: (b, 0, i)),
            pl.BlockSpec((None, IDX_HEADS, QB), lambda b, i: (b, 0, i)),
            pl.BlockSpec((None, IDX_HEADS * IDX_DIM, QB), lambda b, i: (b, 0, nxt(i))),
            pl.BlockSpec((None, IDX_HEADS, QB), lambda b, i: (b, 0, nxt(i))),
            pl.BlockSpec((None, seq, IDX_DIM), lambda b, i: (b, 0, 0)),
            pl.BlockSpec((None, seq, ATTN_WIDTH), lambda b, i: (b, 0, 0), **resident),
            pl.BlockSpec((None, seq // KC, ATTN_WIDTH, KC), lambda b, i: (b, 0, 0, 0), **resident),
        ] + side_specs,
        out_specs=[pl.BlockSpec((None, QB, ATTN_WIDTH), lambda b, i: (b, i, 0))] + side_specs,
        out_shape=[jax.ShapeDtypeStruct((bsz, seq, ATTN_WIDTH), BF16)]
        + [jax.ShapeDtypeStruct(w.shape, BF16) for w in side_weights],
        scratch_shapes=[
            pltpu.VMEM((seq // KC, KC, QB), F32),
            pltpu.VMEM((seq // KC, KC, QB), F32),
            pltpu.VMEM((2, ATTN_HEADS, KC, QB), F32),
            pltpu.VMEM((ATTN_HEADS, 1, QB), F32),
            pltpu.VMEM((ATTN_HEADS, 1, QB), F32),
            pltpu.VMEM((ATTN_HEADS, HEAD_DIM + PACKED_SUBLANES, QB), F32),
            pltpu.VMEM((ATTN_HEADS, KC, QB), F32),
        ],
        compiler_params=pltpu.CompilerParams(
            dimension_semantics=("arbitrary", "arbitrary"), vmem_limit_bytes=VMEM_LIMIT),
        name="dsa",
    )(rel_bias, qT, qiT, wT, qiT, wT, kidx, k, vT4, *side_weights)


def _mix_kernel(x_ref, u_ref, halo_ref, attn_ref, wpool_ref, pscale_ref, wout_ref, g_ref,
                x1_ref, h2_ref, ext_s, lvl0_s, lvl1_s, *, seq):
    m = pl.program_id(0)
    tiles_per_seq = seq // TM_MIX
    first = (m % tiles_per_seq) == 0
    ext_s[0:POOL_HALO, :] = jnp.where(first, 0.0, halo_ref[...])
    ext_s[POOL_HALO:, :] = u_ref[...]
    n_ext = TM_MIX + POOL_HALO
    lvl_bufs = (lvl0_s, lvl1_s)

    t_pos = (m % tiles_per_seq) * TM_MIX + lax.broadcasted_iota(I32, (TM_MIX, POOL_GC), 0)
    acc = x_ref[...] + jnp.dot(attn_ref[...], wout_ref[POOL_WIDTH:, :], preferred_element_type=F32)
    for g, w in enumerate(POOL_WINDOWS):
        cols = slice(g * POOL_GC, (g + 1) * POOL_GC)
        u = ext_s[POOL_HALO:, cols]
        levels = w.bit_length() - 1
        assert w == 1 << levels and SUBLANES * levels <= POOL_HALO
        src, src_cols = ext_s, cols
        for lv in range(levels):
            start = SUBLANES * (lv + 1)
            n_rows = n_ext - start
            t = src[pl.ds(start, n_rows), src_cols] + src[pl.ds(start - (1 << lv), n_rows), src_cols]
            if lv == levels - 1:
                wsum = t[POOL_HALO - start:, :]
            else:
                dst = lvl_bufs[lv % 2]
                dst[pl.ds(start, n_rows), :] = t
                src, src_cols = dst, slice(None)
        count = jnp.minimum(t_pos + 1, w).astype(F32)
        pooled = (wsum / count - u).astype(BF16)
        y = jnp.dot(pooled, wpool_ref[g], preferred_element_type=F32) * pscale_ref[:, cols]
        acc = acc + jnp.dot(y.astype(BF16), wout_ref[cols, :], preferred_element_type=F32)
    x1_ref[...] = acc
    h2_ref[...] = _rms(acc, g_ref[...]).astype(BF16)


def _mix(x2d, u_pool, attn2d, w_pool, pool_scale, w_out, g_ffn, seq):
    m = x2d.shape[0]
    halo_per_tile = TM_MIX // POOL_HALO
    kernel = functools.partial(_mix_kernel, seq=seq)
    return pl.pallas_call(
        kernel,
        grid=(m // TM_MIX,),
        in_specs=[
            pl.BlockSpec((TM_MIX, D_MODEL), lambda i: (i, 0)),
            pl.BlockSpec((TM_MIX, POOL_WIDTH), lambda i: (i, 0)),
            pl.BlockSpec((POOL_HALO, POOL_WIDTH), lambda i: (jnp.maximum(i * halo_per_tile - 1, 0), 0)),
            pl.BlockSpec((TM_MIX, ATTN_WIDTH), lambda i: (i, 0)),
            pl.BlockSpec((POOL_GROUPS, POOL_GC, POOL_GC), lambda i: (0, 0, 0)),
            pl.BlockSpec((1, POOL_WIDTH), lambda i: (0, 0)),
            pl.BlockSpec((D_MODEL, D_MODEL), lambda i: (0, 0)),
            pl.BlockSpec((1, D_MODEL), lambda i: (0, 0)),
        ],
        out_specs=[
            pl.BlockSpec((TM_MIX, D_MODEL), lambda i: (i, 0)),
            pl.BlockSpec((TM_MIX, D_MODEL), lambda i: (i, 0)),
        ],
        out_shape=[
            jax.ShapeDtypeStruct((m, D_MODEL), F32),
            jax.ShapeDtypeStruct((m, D_MODEL), BF16),
        ],
        scratch_shapes=[pltpu.VMEM((TM_MIX + POOL_HALO, POOL_WIDTH), F32),
                        pltpu.VMEM((TM_MIX + POOL_HALO, POOL_GC), F32),
                        pltpu.VMEM((TM_MIX + POOL_HALO, POOL_GC), F32)],
        compiler_params=pltpu.CompilerParams(
            dimension_semantics=("arbitrary",), vmem_limit_bytes=VMEM_LIMIT),
        name="mix",
    )(x2d, u_pool, u_pool, attn2d, w_pool, pool_scale, w_out, g_ffn)


def _ffn_kernel(h_ref, x1_ref, wg_ref, wv_ref, cwg_ref, cwv_ref, cbg_ref, cbv_ref, wd_ref,
                o_ref, carry_s, *u_bufs, seq):
    m = pl.program_id(0)
    f = pl.program_id(1)
    tiles_per_seq = seq // TM_FFN
    first = (m % tiles_per_seq) == 0

    @pl.when(f == 0)
    def _():
        o_ref[...] = x1_ref[...]

    h = h_ref[...]
    n_half = TF_FFN // TF_HALF

    def up(s):
        cols = slice(s * TF_HALF, (s + 1) * TF_HALF)
        for k, w_ref in enumerate((wg_ref, wv_ref)):
            u = u_bufs[2 * s + k]
            u[0:SUBLANES, :] = jnp.where(first, 0.0, carry_s[f, 2 * s + k])
            u[SUBLANES:, :] = jnp.dot(h, w_ref[:, cols], preferred_element_type=F32)
            carry_s[f, 2 * s + k] = u[TM_FFN:, :]

    def conv(u, cols, cw_ref, cb_ref):
        out = cb_ref[:, cols]
        for j in range(CONV_WIDTH):
            shift = CONV_WIDTH - 1 - j
            out = out + u[pl.ds(SUBLANES - shift, TM_FFN), :] * cw_ref[j:j + 1, cols]
        return out

    def gated(s):
        cols = slice(s * TF_HALF, (s + 1) * TF_HALF)
        gate = conv(u_bufs[2 * s], cols, cwg_ref, cbg_ref)
        val = conv(u_bufs[2 * s + 1], cols, cwv_ref, cbv_ref)
        return (gate * jax.nn.sigmoid(gate) * val).astype(BF16)

    def down(s, act):
        return jnp.dot(act, wd_ref[s * TF_HALF:(s + 1) * TF_HALF, :], preferred_element_type=F32)

    up(0)
    act0 = gated(0)
    up(1)
    contrib = down(0, act0)
    act1 = gated(1)
    contrib = contrib + down(1, act1)
    o_ref[...] += contrib


def _ffn(h2, x1, w_up, conv_w, conv_b, w_down, seq):
    m = h2.shape[0]
    n_f = D_FF // TF_FFN
    kernel = functools.partial(_ffn_kernel, seq=seq)
    return pl.pallas_call(
        kernel,
        grid=(m // TM_FFN, n_f),
        in_specs=[
            pl.BlockSpec((TM_FFN, D_MODEL), lambda i, f: (i, 0)),
            pl.BlockSpec((TM_FFN, D_MODEL), lambda i, f: (i, 0)),
            pl.BlockSpec((D_MODEL, TF_FFN), lambda i, f: (0, f)),
            pl.BlockSpec((D_MODEL, TF_FFN), lambda i, f: (0, f + n_f)),
            pl.BlockSpec((CONV_WIDTH, TF_FFN), lambda i, f: (0, f)),
            pl.BlockSpec((CONV_WIDTH, TF_FFN), lambda i, f: (0, f + n_f)),
            pl.BlockSpec((1, TF_FFN), lambda i, f: (0, f)),
            pl.BlockSpec((1, TF_FFN), lambda i, f: (0, f + n_f)),
            pl.BlockSpec((TF_FFN, D_MODEL), lambda i, f: (f, 0)),
        ],
        out_specs=pl.BlockSpec((TM_FFN, D_MODEL), lambda i, f: (i, 0)),
        out_shape=jax.ShapeDtypeStruct((m, D_MODEL), F32),
        scratch_shapes=[pltpu.VMEM((n_f, 2 * (TF_FFN // TF_HALF), SUBLANES, TF_HALF), F32)]
        + [pltpu.VMEM((TM_FFN + SUBLANES, TF_HALF), F32)] * (2 * (TF_FFN // TF_HALF)),
        compiler_params=pltpu.CompilerParams(
            dimension_semantics=("arbitrary", "arbitrary"), vmem_limit_bytes=VMEM_LIMIT),
        name="ffn",
    )(h2, x1, w_up, w_up, conv_w, conv_w, conv_b, conv_b, w_down)


def _ple_kernel(x_ref, p_ref, gp_ref, wgate32_ref, wproj_ref, gf_ref, o_ref, wgate_ref, *, last_layer):
    @pl.when(pl.program_id(0) == 0)
    def _():
        for r in range(0, D_MODEL, CAST_ROWS):
            wgate_ref[r:r + CAST_ROWS, :] = wgate32_ref[r:r + CAST_ROWS, :].astype(BF16)

    x = x_ref[...]
    hg = _rms(x, gp_ref[...]).astype(BF16)
    gate = jax.nn.sigmoid(jnp.dot(hg, wgate_ref[...], preferred_element_type=F32))
    emb = jnp.dot(p_ref[...].astype(BF16), wproj_ref[...], preferred_element_type=F32)
    x3 = x + emb * gate
    o_ref[...] = _rms(x3, gf_ref[...]) if last_layer else x3


def _ple(x2, p2d, g_ple, w_gate, w_proj, g_final, last_layer):
    m = x2.shape[0]
    return pl.pallas_call(
        functools.partial(_ple_kernel, last_layer=last_layer),
        grid=(m // TM_PLE,),
        in_specs=[
            pl.BlockSpec((TM_PLE, D_MODEL), lambda i: (i, 0)),
            pl.BlockSpec((TM_PLE, D_PLE), lambda i: (i, 0)),
            pl.BlockSpec((1, D_MODEL), lambda i: (0, 0)),
            pl.BlockSpec((D_MODEL, D_MODEL), lambda i: (0, 0), pipeline_mode=pl.Buffered(1)),
            pl.BlockSpec((D_PLE, D_MODEL), lambda i: (0, 0)),
            pl.BlockSpec((1, D_MODEL), lambda i: (0, 0)),
        ],
        out_specs=pl.BlockSpec((TM_PLE, D_MODEL), lambda i: (i, 0)),
        out_shape=jax.ShapeDtypeStruct((m, D_MODEL), F32),
        scratch_shapes=[pltpu.VMEM((D_MODEL, D_MODEL), BF16)],
        compiler_params=pltpu.CompilerParams(
            dimension_semantics=("arbitrary",), vmem_limit_bytes=VMEM_LIMIT),
        name="ple",
    )(x2, p2d, g_ple, w_gate, w_proj, g_final)


def kernel(x, p, g_mix, w_in, w_pool, pool_scale, rel_bias, w_out, g_ffn, w_up, conv_w, conv_b,
           w_down, g_ple, w_ple_gate, w_ple_proj, g_final):
    bsz, seq, _ = x.shape
    depth = w_in.shape[0]
    assert seq % QB == 0 and seq % TM_MIX == 0 and seq % TM_FFN == 0 and (bsz * seq) % TM_IN == 0
    x2d = x.reshape(bsz * seq, D_MODEL)
    for i in range(depth):
        u_pool, k2d, qT, qiT, vT4, small = _inproj(x2d, g_mix[i].reshape(1, D_MODEL),
                                                   w_in[i].T.astype(BF16), bsz, seq)
        small3 = small.reshape(bsz, seq, SMALL_W)
        kidx = small3[:, :, :IDX_DIM].astype(BF16)
        wT = jnp.swapaxes(small3[:, :, IDX_DIM:IDX_DIM + IDX_HEADS], 1, 2)
        attn, w_out_b, w_up_b, w_down_b = _dsa(rel_bias, qT, qiT, wT, kidx,
                                               k2d.reshape(bsz, seq, ATTN_WIDTH), vT4, seq,
                                               (w_out[i], w_up[i], w_down[i]))

        x1, h2 = _mix(x2d, u_pool, attn.reshape(bsz * seq, ATTN_WIDTH), w_pool[i].astype(BF16),
                      pool_scale[i].reshape(1, POOL_WIDTH), w_out_b,
                      g_ffn[i].reshape(1, D_MODEL), seq)
        x2 = _ffn(h2, x1, w_up_b, conv_w[i], conv_b[i].reshape(1, 2 * D_FF), w_down_b, seq)
        x2d = _ple(x2, p[i].reshape(bsz * seq, D_PLE), g_ple[i].reshape(1, D_MODEL),
                   w_ple_gate[i], w_ple_proj[i].astype(BF16),
                   g_final.reshape(1, D_MODEL), last_layer=(i == depth - 1))
    return x2d.reshape(bsz, seq, D_MODEL)
```

```python
import functools
import math

import numpy as np
import jax
import jax.numpy as jnp
from jax import lax
from jax.experimental import pallas as pl
from jax.experimental.pallas import tpu as pltpu

F32 = jnp.float32
BF16 = jnp.bfloat16
I32 = jnp.int32
I16 = jnp.int16

D_MODEL = 2048
D_PLE = 256
POOL_WIDTH = 1024
POOL_GROUPS = 4
POOL_GC = 256
POOL_WINDOWS = (2, 4, 8, 16)
ATTN_HEADS = 8
HEAD_DIM = 128
ATTN_WIDTH = 1024
IDX_HEADS = 16
IDX_DIM = 64
TOPK_MAX = 256
N_BUCKETS = 32
MAX_DISTANCE = 128
D_FF = 5632
CONV_WIDTH = 3
EPS = 1e-6
G_POOL, G_Q, G_K, G_V, G_QI = range(5)
N_GROUPS = 5
GROUP_W = 1024

V7X_VMEM_BYTES = 64 * 1024 * 1024
SUBLANES = 8
PACKED_SUBLANES = 16
VMEM_LIMIT = V7X_VMEM_BYTES - 8 * 1024 * 1024

TM_IN = 512
SMALL_W = IDX_DIM + IDX_HEADS
QB = 256
KC = 256
TM_MIX = 512
POOL_HALO = 32
TM_FFN = 512
TF_FFN = 512
TF_HALF = 256
TM_PLE = 512
CAST_ROWS = 256

INT_MIN = -2 ** 31
I16_MIN = -2 ** 15
I16_MAX = 2 ** 15 - 1
NEG_BIG = -1e30
F32_MAX = float(np.finfo(np.float32).max)
LOG2E = math.log2(math.e)
Q_PRESCALE = (HEAD_DIM ** -0.5) * LOG2E


def _bucket_lower_bounds():
    n = np.arange(0, 4 * MAX_DISTANCE, dtype=np.int64)
    max_exact = N_BUCKETS // 2
    nf = np.maximum(n, 1).astype(np.float32)
    large = max_exact + (np.log(nf / np.float32(max_exact)) / np.float32(math.log(MAX_DISTANCE / max_exact))
                         * np.float32(N_BUCKETS - max_exact)).astype(np.int32)
    large = np.minimum(large, N_BUCKETS - 1)
    bucket = np.where(n < max_exact, n, large)
    assert np.all(np.diff(bucket) >= 0)
    lows = [int(np.argmax(bucket == b)) for b in range(N_BUCKETS)]
    assert all(bucket[lo] == b for b, lo in enumerate(lows))
    return lows


BUCKET_LO = _bucket_lower_bounds()


def _rms(x, g):
    ms = jnp.mean(x * x, axis=-1, keepdims=True)
    return (x * lax.rsqrt(ms + EPS)) * g


def _nt_dot(a, b):
    return lax.dot_general(a, b, (((1,), (1,)), ((), ())), preferred_element_type=F32)


def _inproj_kernel(x_ref, g_ref, wt_ref, pool_ref, k_ref, qT_ref, qiT_ref, vT_ref, small_ref):
    hb = _rms(x_ref[...], g_ref[...]).astype(BF16)

    def w(group):
        return wt_ref[group * GROUP_W:(group + 1) * GROUP_W, :]

    pool_ref[...] = _nt_dot(hb, w(G_POOL))
    k_ref[...] = _nt_dot(hb, w(G_K)).astype(BF16)
    small_ref[...] = _nt_dot(hb, wt_ref[N_GROUPS * GROUP_W:, :])
    qT_ref[...] = (_nt_dot(w(G_Q), hb) * Q_PRESCALE).astype(BF16)
    qiT_ref[...] = _nt_dot(w(G_QI), hb).astype(BF16)
    out_t = _nt_dot(w(G_V), hb).astype(BF16)
    for j in range(TM_IN // KC):
        vT_ref[j] = out_t[:, j * KC:(j + 1) * KC]


def _inproj(x2d, g, wt, bsz, seq):
    m = x2d.shape[0]
    tiles_per_seq = seq // TM_IN
    assert wt.shape[0] == N_GROUPS * GROUP_W + SMALL_W
    return pl.pallas_call(
        _inproj_kernel,
        grid=(m // TM_IN,),
        in_specs=[
            pl.BlockSpec((TM_IN, D_MODEL), lambda i: (i, 0)),
            pl.BlockSpec((1, D_MODEL), lambda i: (0, 0)),
            pl.BlockSpec(wt.shape, lambda i: (0, 0), pipeline_mode=pl.Buffered(1)),
        ],
        out_specs=[
            pl.BlockSpec((TM_IN, GROUP_W), lambda i: (i, 0)),
            pl.BlockSpec((TM_IN, GROUP_W), lambda i: (i, 0)),
            pl.BlockSpec((None, GROUP_W, TM_IN), lambda i: (i // tiles_per_seq, 0, i % tiles_per_seq)),
            pl.BlockSpec((None, GROUP_W, TM_IN), lambda i: (i // tiles_per_seq, 0, i % tiles_per_seq)),
            pl.BlockSpec((None, TM_IN // KC, GROUP_W, KC),
                         lambda i: (i // tiles_per_seq, i % tiles_per_seq, 0, 0)),
            pl.BlockSpec((TM_IN, SMALL_W), lambda i: (i, 0)),
        ],
        out_shape=[
            jax.ShapeDtypeStruct((m, POOL_WIDTH), F32),
            jax.ShapeDtypeStruct((m, ATTN_WIDTH), BF16),
            jax.ShapeDtypeStruct((bsz, ATTN_WIDTH, seq), BF16),
            jax.ShapeDtypeStruct((bsz, IDX_HEADS * IDX_DIM, seq), BF16),
            jax.ShapeDtypeStruct((bsz, seq // KC, ATTN_WIDTH, KC), BF16),
            jax.ShapeDtypeStruct((m, SMALL_W), F32),
        ],
        compiler_params=pltpu.CompilerParams(
            dimension_semantics=("arbitrary",), vmem_limit_bytes=VMEM_LIMIT),
        name="inproj",
    )(x2d, g, wt)


def _dsa_kernel(rel_ref, qT_ref, qiT_ref, wT_ref, qiTn_ref, wTn_ref, kidx_ref, k_ref, vT_ref, *rest,
                topk, n_q, n_side):
    side_in, rest = rest[:n_side], rest[n_side:]
    o_ref, side_out = rest[0], rest[1:1 + n_side]
    sc_s, sb_s, madd_s, bias_s, mx_s, alpha_s, acc_s, s_s = rest[1 + n_side:]
    b = pl.program_id(0)
    i = pl.program_id(1)
    n_ch = i + 1

    for src, dst in zip(side_in, side_out):
        dst[...] = src[...].astype(dst.dtype)

    @pl.when((b == 0) & (i == 0))
    def _():
        r = lax.broadcasted_iota(I32, (KC, QB), 0)
        c = lax.broadcasted_iota(I32, (KC, QB), 1)
        for off in range(2):
            dist = jnp.maximum(c - r + off * QB, 0)

            def head_body(h, carry, dist=dist, off=off):
                t = jnp.full((KC, QB), rel_ref[0, h], F32)
                for bk in range(1, N_BUCKETS):
                    t = jnp.where(dist >= BUCKET_LO[bk], rel_ref[bk, h], t)
                bias_s[off, h] = (t - rel_ref[N_BUCKETS - 1, h]) * LOG2E
                return carry

            lax.fori_loop(0, ATTN_HEADS, head_body, 0)

    idx_scale = (IDX_HEADS ** -0.5) * (IDX_DIM ** -0.5)
    row = lax.broadcasted_iota(I32, (KC, QB), 0)
    col = lax.broadcasted_iota(I32, (KC, QB), 1)

    def index_chunk(m, qi_ref, w_scaled, q_block):
        kc = kidx_ref[pl.ds(pl.multiple_of(m * KC, KC), KC), :]
        acc = jnp.zeros((KC, QB), F32)
        for h in range(IDX_HEADS):
            d = jnp.dot(kc, qi_ref[h * IDX_DIM:(h + 1) * IDX_DIM, :],
                        preferred_element_type=F32)
            acc = acc + w_scaled[h:h + 1, :] * jnp.maximum(d, 0.0)
        causal = (m * KC + row) <= (q_block * QB + col)
        score = jnp.where(causal, acc, -jnp.inf)
        sc_s[m] = score
        sb_s[m] = score.astype(BF16)

    @pl.when(i == 0)
    def _():
        index_chunk(0, qiT_ref, wT_ref[...] * idx_scale, 0)

    def code_to_score(code):
        bits = jnp.where(code >= 0, code, code ^ 0x7FFFFFFF)
        return pltpu.bitcast(bits, F32)

    def code16_to_bf16(code):
        bits = jnp.where(code >= 0, code, code ^ 0x7FFF)
        return pltpu.bitcast(bits.astype(I16), BF16)

    def count16_ge(cand):
        def body(m, acc):
            ge = (sb_s[m] >= cand).astype(I16)
            for j in range(KC // PACKED_SUBLANES):
                acc = acc + ge[j * PACKED_SUBLANES:(j + 1) * PACKED_SUBLANES, :]
            return acc

        acc = lax.fori_loop(0, n_ch, body, jnp.zeros((PACKED_SUBLANES, QB), I16))
        return jnp.sum(acc.astype(I32), axis=0, keepdims=True)

    def bit16_body(it, prefix):
        cand = prefix + jnp.left_shift(jnp.int32(1), 15 - it)
        return jnp.where(count16_ge(code16_to_bf16(cand)) >= topk, cand, prefix)

    p16 = lax.fori_loop(0, 16, bit16_body, jnp.full((1, QB), I16_MIN, I32))

    def fold(acc, flags):
        return acc + jnp.sum(jnp.where(flags, 1.0, 0.0).reshape(KC // SUBLANES, SUBLANES, QB), axis=0)

    def total(acc):
        return jnp.sum(acc, axis=0, keepdims=True)

    zero8 = jnp.zeros((SUBLANES, QB), F32)

    def count_ge(cand):
        return total(lax.fori_loop(0, n_ch, lambda m, acc: fold(acc, sc_s[m] >= cand), zero8))

    below = p16 - 1
    lo = jnp.left_shift(below, 16) + jnp.where(below < 0, 0xFFFF, 0)

    def bit_body(it, prefix):
        cand = prefix + jnp.left_shift(jnp.int32(1), 16 - it)
        return jnp.where(count_ge(code_to_score(cand)) >= topk, cand, prefix)

    prefix = lax.fori_loop(0, 17, bit_body, lo)
    thr = jnp.where(p16 == I16_MIN, -F32_MAX, code_to_score(prefix))

    def mask_body(m, kept):
        keep = sc_s[m] >= thr
        madd_s[m] = jnp.where(keep, 0.0, NEG_BIG)
        return fold(kept, keep)

    excess = total(lax.fori_loop(0, n_ch, mask_body, zero8)) - topk

    @pl.when(jnp.max(excess) > 0)
    def _():
        n_tied = total(lax.fori_loop(0, n_ch, lambda m, acc: fold(acc, sc_s[m] == thr), zero8))
        need = n_tied - excess
        r = lax.broadcasted_iota(I32, (KC, KC), 0)
        c = lax.broadcasted_iota(I32, (KC, KC), 1)
        prefix_op = jnp.where(c <= r, 1.0, 0.0).astype(BF16)

        def fix_body(m, seen):
            tied = sc_s[m] == thr
            t = jnp.where(tied, 1.0, 0.0).astype(BF16)
            rank = seen + jnp.dot(prefix_op, t, preferred_element_type=F32)
            madd_s[m] = jnp.where(tied & (rank > need), NEG_BIG, madd_s[m])
            return rank[KC - 1:KC, :]

        lax.fori_loop(0, n_ch, fix_body, jnp.zeros((1, QB), F32))

    n_far = jnp.maximum(i - 1, 0)
    mx_s[...] = jnp.full(mx_s.shape, -jnp.inf, F32)
    acc_s[...] = jnp.zeros(acc_s.shape, F32)

    w_next = wTn_ref[...] * idx_scale

    def att_body(m, carry, near):
        index_chunk(m, qiTn_ref, w_next, i + 1)
        madd = madd_s[m]
        rows = pl.ds(pl.multiple_of(m * KC, KC), KC)
        for h in range(ATTN_HEADS):
            hd = slice(h * HEAD_DIM, (h + 1) * HEAD_DIM)
            s = jnp.dot(k_ref[rows, hd], qT_ref[hd, :], preferred_element_type=F32)
            if near:
                s = s + (madd + bias_s[i - m, h])
            else:
                s = s + madd
            s_s[h] = s
            mx = mx_s[h]
            m_new = jnp.maximum(mx, jnp.max(s, axis=0, keepdims=True))
            alpha_s[h] = jnp.exp2(mx - m_new)
            mx_s[h] = m_new
        ones_rows = jnp.ones((PACKED_SUBLANES, KC), BF16)
        for h in range(ATTN_HEADS):
            hd = slice(h * HEAD_DIM, (h + 1) * HEAD_DIM)
            p = jnp.exp2(s_s[h] - mx_s[h])
            v_aug = jnp.concatenate([vT_ref[m, hd, :], ones_rows], axis=0)
            pv = jnp.dot(v_aug, p.astype(BF16), preferred_element_type=F32)
            acc_s[h] = alpha_s[h] * acc_s[h] + pv
        return carry

    def far_pair(pair, carry):
        att_body(2 * pair, carry, near=False)
        att_body(2 * pair + 1, carry, near=False)
        return carry

    lax.fori_loop(0, n_far // 2, far_pair, 0)

    @pl.when(n_far % 2 == 1)
    def _():
        att_body(n_far - 1, 0, near=False)

    @pl.when(i >= 1)
    def _():
        att_body(i - 1, 0, near=True)

    att_body(i, 0, near=True)
    index_chunk(jnp.minimum(i + 1, n_q - 1), qiTn_ref, w_next, i + 1)

    for h in range(ATTN_HEADS):
        acc = acc_s[h]
        out_t = acc[:HEAD_DIM] / acc[HEAD_DIM:HEAD_DIM + 1]
        o_ref[:, h * HEAD_DIM:(h + 1) * HEAD_DIM] = out_t.T.astype(o_ref.dtype)


def _dsa(rel_bias, qT, qiT, wT, kidx, k, vT4, seq, side_weights):
    bsz = qT.shape[0]
    topk = min(TOPK_MAX, seq // 4)
    n_q = seq // QB
    n_steps = bsz * n_q
    kernel = functools.partial(_dsa_kernel, topk=topk, n_q=n_q, n_side=len(side_weights))
    nxt = lambda i: jnp.minimum(i + 1, n_q - 1)
    side_specs = []
    for w in side_weights:
        rows = w.shape[0] // n_steps
        assert rows * n_steps == w.shape[0] and rows % PACKED_SUBLANES == 0
        side_specs.append(pl.BlockSpec((rows, w.shape[1]), lambda b, i: (b * n_q + i, 0)))
    resident = dict(pipeline_mode=pl.Buffered(1))
    return pl.pallas_call(
        kernel,
        grid=(bsz, n_q),
        in_specs=[
            pl.BlockSpec(memory_space=pltpu.SMEM),
            pl.BlockSpec((None, ATTN_WIDTH, QB), lambda b, i: (b, 0, i)),
            pl.BlockSpec((None, IDX_HEADS * IDX_DIM, QB), lambda b, i: (b, 0, i)),
            pl.BlockSpec((None, IDX_HEADS, QB), lambda b, i: (b, 0, i)),
            pl.BlockSpec((None, IDX_HEADS * IDX_DIM, QB), lambda b, i: (b, 0, nxt(i))),
            pl.BlockSpec((None, IDX_HEADS, QB), lambda b, i: (b, 0, nxt(i))),
            pl.BlockSpec((None, seq, IDX_DIM), lambda b, i: (b, 0, 0)),
            pl.BlockSpec((None, seq, ATTN_WIDTH), lambda b, i: (b, 0, 0), **resident),
            pl.BlockSpec((None, seq // KC, ATTN_WIDTH, KC), lambda b, i: (b, 0, 0, 0), **resident),
        ] + side_specs,
        out_specs=[pl.BlockSpec((None, QB, ATTN_WIDTH), lambda b, i: (b, i, 0))] + side_specs,
        out_shape=[jax.ShapeDtypeStruct((bsz, seq, ATTN_WIDTH), BF16)]
        + [jax.ShapeDtypeStruct(w.shape, BF16) for w in side_weights],
        scratch_shapes=[
            pltpu.VMEM((seq // KC, KC, QB), F32),
            pltpu.VMEM((seq // KC, KC, QB), BF16),
            pltpu.VMEM((seq // KC, KC, QB), F32),
            pltpu.VMEM((2, ATTN_HEADS, KC, QB), F32),
            pltpu.VMEM((ATTN_HEADS, 1, QB), F32),
            pltpu.VMEM((ATTN_HEADS, 1, QB), F32),
            pltpu.VMEM((ATTN_HEADS, HEAD_DIM + PACKED_SUBLANES, QB), F32),
            pltpu.VMEM((ATTN_HEADS, KC, QB), F32),
        ],
        compiler_params=pltpu.CompilerParams(
            dimension_semantics=("arbitrary", "arbitrary"), vmem_limit_bytes=VMEM_LIMIT),
        name="dsa",
    )(rel_bias, qT, qiT, wT, qiT, wT, kidx, k, vT4, *side_weights)


def _mix_kernel(x_ref, u_ref, halo_ref, attn_ref, wpool_ref, pscale_ref, wout_ref, g_ref,
                x1_ref, h2_ref, ext_s, lvl0_s, lvl1_s, *, seq):
    m = pl.program_id(0)
    tiles_per_seq = seq // TM_MIX
    first = (m % tiles_per_seq) == 0
    ext_s[0:POOL_HALO, :] = jnp.where(first, 0.0, halo_ref[...])
    ext_s[POOL_HALO:, :] = u_ref[...]
    n_ext = TM_MIX + POOL_HALO
    lvl_bufs = (lvl0_s, lvl1_s)

    t_pos = (m % tiles_per_seq) * TM_MIX + lax.broadcasted_iota(I32, (TM_MIX, POOL_GC), 0)
    acc = x_ref[...] + jnp.dot(attn_ref[...], wout_ref[POOL_WIDTH:, :], preferred_element_type=F32)
    for g, w in enumerate(POOL_WINDOWS):
        cols = slice(g * POOL_GC, (g + 1) * POOL_GC)
        u = ext_s[POOL_HALO:, cols]
        levels = w.bit_length() - 1
        assert w == 1 << levels and SUBLANES * levels <= POOL_HALO
        src, src_cols = ext_s, cols
        for lv in range(levels):
            start = SUBLANES * (lv + 1)
            n_rows = n_ext - start
            t = src[pl.ds(start, n_rows), src_cols] + src[pl.ds(start - (1 << lv), n_rows), src_cols]
            if lv == levels - 1:
                wsum = t[POOL_HALO - start:, :]
            else:
                dst = lvl_bufs[lv % 2]
                dst[pl.ds(start, n_rows), :] = t
                src, src_cols = dst, slice(None)
        count = jnp.minimum(t_pos + 1, w).astype(F32)
        pooled = (wsum / count - u).astype(BF16)
        y = jnp.dot(pooled, wpool_ref[g], preferred_element_type=F32) * pscale_ref[:, cols]
        acc = acc + jnp.dot(y.astype(BF16), wout_ref[cols, :], preferred_element_type=F32)
    x1_ref[...] = acc
    h2_ref[...] = _rms(acc, g_ref[...]).astype(BF16)


def _mix(x2d, u_pool, attn2d, w_pool, pool_scale, w_out, g_ffn, seq):
    m = x2d.shape[0]
    halo_per_tile = TM_MIX // POOL_HALO
    kernel = functools.partial(_mix_kernel, seq=seq)
    return pl.pallas_call(
        kernel,
        grid=(m // TM_MIX,),
        in_specs=[
            pl.BlockSpec((TM_MIX, D_MODEL), lambda i: (i, 0)),
            pl.BlockSpec((TM_MIX, POOL_WIDTH), lambda i: (i, 0)),
            pl.BlockSpec((POOL_HALO, POOL_WIDTH), lambda i: (jnp.maximum(i * halo_per_tile - 1, 0), 0)),
            pl.BlockSpec((TM_MIX, ATTN_WIDTH), lambda i: (i, 0)),
            pl.BlockSpec((POOL_GROUPS, POOL_GC, POOL_GC), lambda i: (0, 0, 0)),
            pl.BlockSpec((1, POOL_WIDTH), lambda i: (0, 0)),
            pl.BlockSpec((D_MODEL, D_MODEL), lambda i: (0, 0)),
            pl.BlockSpec((1, D_MODEL), lambda i: (0, 0)),
        ],
        out_specs=[
            pl.BlockSpec((TM_MIX, D_MODEL), lambda i: (i, 0)),
            pl.BlockSpec((TM_MIX, D_MODEL), lambda i: (i, 0)),
        ],
        out_shape=[
            jax.ShapeDtypeStruct((m, D_MODEL), F32),
            jax.ShapeDtypeStruct((m, D_MODEL), BF16),
        ],
        scratch_shapes=[pltpu.VMEM((TM_MIX + POOL_HALO, POOL_WIDTH), F32),
                        pltpu.VMEM((TM_MIX + POOL_HALO, POOL_GC), F32),
                        pltpu.VMEM((TM_MIX + POOL_HALO, POOL_GC), F32)],
        compiler_params=pltpu.CompilerParams(
            dimension_semantics=("arbitrary",), vmem_limit_bytes=VMEM_LIMIT),
        name="mix",
    )(x2d, u_pool, u_pool, attn2d, w_pool, pool_scale, w_out, g_ffn)


def _ffn_kernel(h_ref, x1_ref, wg_ref, wv_ref, cwg_ref, cwv_ref, cbg_ref, cbv_ref, wd_ref,
                o_ref, carry_s, *u_bufs, seq):
    m = pl.program_id(0)
    f = pl.program_id(1)
    tiles_per_seq = seq // TM_FFN
    first = (m % tiles_per_seq) == 0

    @pl.when(f == 0)
    def _():
        o_ref[...] = x1_ref[...]

    h = h_ref[...]
    n_half = TF_FFN // TF_HALF

    def up(s):
        cols = slice(s * TF_HALF, (s + 1) * TF_HALF)
        for k, w_ref in enumerate((wg_ref, wv_ref)):
            u = u_bufs[2 * s + k]
            u[0:SUBLANES, :] = jnp.where(first, 0.0, carry_s[f, 2 * s + k])
            u[SUBLANES:, :] = jnp.dot(h, w_ref[:, cols], preferred_element_type=F32)
            carry_s[f, 2 * s + k] = u[TM_FFN:, :]

    def conv(u, cols, cw_ref, cb_ref):
        out = cb_ref[:, cols]
        for j in range(CONV_WIDTH):
            shift = CONV_WIDTH - 1 - j
            out = out + u[pl.ds(SUBLANES - shift, TM_FFN), :] * cw_ref[j:j + 1, cols]
        return out

    def gated(s):
        cols = slice(s * TF_HALF, (s + 1) * TF_HALF)
        gate = conv(u_bufs[2 * s], cols, cwg_ref, cbg_ref)
        val = conv(u_bufs[2 * s + 1], cols, cwv_ref, cbv_ref)
        return (gate * jax.nn.sigmoid(gate) * val).astype(BF16)

    def down(s, act):
        return jnp.dot(act, wd_ref[s * TF_HALF:(s + 1) * TF_HALF, :], preferred_element_type=F32)

    up(0)
    act0 = gated(0)
    up(1)
    contrib = down(0, act0)
    act1 = gated(1)
    contrib = contrib + down(1, act1)
    o_ref[...] += contrib


def _ffn(h2, x1, w_up, conv_w, conv_b, w_down, seq):
    m = h2.shape[0]
    n_f = D_FF // TF_FFN
    kernel = functools.partial(_ffn_kernel, seq=seq)
    return pl.pallas_call(
        kernel,
        grid=(m // TM_FFN, n_f),
        in_specs=[
            pl.BlockSpec((TM_FFN, D_MODEL), lambda i, f: (i, 0)),
            pl.BlockSpec((TM_FFN, D_MODEL), lambda i, f: (i, 0)),
            pl.BlockSpec((D_MODEL, TF_FFN), lambda i, f: (0, f)),
            pl.BlockSpec((D_MODEL, TF_FFN), lambda i, f: (0, f + n_f)),
            pl.BlockSpec((CONV_WIDTH, TF_FFN), lambda i, f: (0, f)),
            pl.BlockSpec((CONV_WIDTH, TF_FFN), lambda i, f: (0, f + n_f)),
            pl.BlockSpec((1, TF_FFN), lambda i, f: (0, f)),
            pl.BlockSpec((1, TF_FFN), lambda i, f: (0, f + n_f)),
            pl.BlockSpec((TF_FFN, D_MODEL), lambda i, f: (f, 0)),
        ],
        out_specs=pl.BlockSpec((TM_FFN, D_MODEL), lambda i, f: (i, 0)),
        out_shape=jax.ShapeDtypeStruct((m, D_MODEL), F32),
        scratch_shapes=[pltpu.VMEM((n_f, 2 * (TF_FFN // TF_HALF), SUBLANES, TF_HALF), F32)]
        + [pltpu.VMEM((TM_FFN + SUBLANES, TF_HALF), F32)] * (2 * (TF_FFN // TF_HALF)),
        compiler_params=pltpu.CompilerParams(
            dimension_semantics=("arbitrary", "arbitrary"), vmem_limit_bytes=VMEM_LIMIT),
        name="ffn",
    )(h2, x1, w_up, w_up, conv_w, conv_w, conv_b, conv_b, w_down)


def _ple_kernel(x_ref, p_ref, gp_ref, wgate32_ref, wproj_ref, gf_ref, o_ref, wgate_ref, *, last_layer):
    @pl.when(pl.program_id(0) == 0)
    def _():
        for r in range(0, D_MODEL, CAST_ROWS):
            wgate_ref[r:r + CAST_ROWS, :] = wgate32_ref[r:r + CAST_ROWS, :].astype(BF16)

    x = x_ref[...]
    hg = _rms(x, gp_ref[...]).astype(BF16)
    gate = jax.nn.sigmoid(jnp.dot(hg, wgate_ref[...], preferred_element_type=F32))
    emb = jnp.dot(p_ref[...].astype(BF16), wproj_ref[...], preferred_element_type=F32)
    x3 = x + emb * gate
    o_ref[...] = _rms(x3, gf_ref[...]) if last_layer else x3


def _ple(x2, p2d, g_ple, w_gate, w_proj, g_final, last_layer):
    m = x2.shape[0]
    return pl.pallas_call(
        functools.partial(_ple_kernel, last_layer=last_layer),
        grid=(m // TM_PLE,),
        in_specs=[
            pl.BlockSpec((TM_PLE, D_MODEL), lambda i: (i, 0)),
            pl.BlockSpec((TM_PLE, D_PLE), lambda i: (i, 0)),
            pl.BlockSpec((1, D_MODEL), lambda i: (0, 0)),
            pl.BlockSpec((D_MODEL, D_MODEL), lambda i: (0, 0), pipeline_mode=pl.Buffered(1)),
            pl.BlockSpec((D_PLE, D_MODEL), lambda i: (0, 0)),
            pl.BlockSpec((1, D_MODEL), lambda i: (0, 0)),
        ],
        out_specs=pl.BlockSpec((TM_PLE, D_MODEL), lambda i: (i, 0)),
        out_shape=jax.ShapeDtypeStruct((m, D_MODEL), F32),
        scratch_shapes=[pltpu.VMEM((D_MODEL, D_MODEL), BF16)],
        compiler_params=pltpu.CompilerParams(
            dimension_semantics=("arbitrary",), vmem_limit_bytes=VMEM_LIMIT),
        name="ple",
    )(x2, p2d, g_ple, w_gate, w_proj, g_final)


def kernel(x, p, g_mix, w_in, w_pool, pool_scale, rel_bias, w_out, g_ffn, w_up, conv_w, conv_b,
           w_down, g_ple, w_ple_gate, w_ple_proj, g_final):
    bsz, seq, _ = x.shape
    depth = w_in.shape[0]
    assert seq % QB == 0 and seq % TM_MIX == 0 and seq % TM_FFN == 0 and (bsz * seq) % TM_IN == 0
    x2d = x.reshape(bsz * seq, D_MODEL)
    for i in range(depth):
        u_pool, k2d, qT, qiT, vT4, small = _inproj(x2d, g_mix[i].reshape(1, D_MODEL),
                                                   w_in[i].T.astype(BF16), bsz, seq)
        small3 = small.reshape(bsz, seq, SMALL_W)
        kidx = small3[:, :, :IDX_DIM].astype(BF16)
        wT = jnp.swapaxes(small3[:, :, IDX_DIM:IDX_DIM + IDX_HEADS], 1, 2)
        attn, w_out_b, w_up_b, w_down_b = _dsa(rel_bias, qT, qiT, wT, kidx,
                                               k2d.reshape(bsz, seq, ATTN_WIDTH), vT4, seq,
                                               (w_out[i], w_up[i], w_down[i]))

        x1, h2 = _mix(x2d, u_pool, attn.reshape(bsz * seq, ATTN_WIDTH), w_pool[i].astype(BF16),
                      pool_scale[i].reshape(1, POOL_WIDTH), w_out_b,
                      g_ffn[i].reshape(1, D_MODEL), seq)
        x2 = _ffn(h2, x1, w_up_b, conv_w[i], conv_b[i].reshape(1, 2 * D_FF), w_down_b, seq)
        x2d = _ple(x2, p[i].reshape(bsz * seq, D_PLE), g_ple[i].reshape(1, D_MODEL),
                   w_ple_gate[i], w_ple_proj[i].astype(BF16),
                   g_final.reshape(1, D_MODEL), last_layer=(i == depth - 1))
    return x2d.reshape(bsz, seq, D_MODEL)
```

```python
import functools
import math

import numpy as np
import jax
import jax.numpy as jnp
from jax import lax
from jax.experimental import pallas as pl
from jax.experimental.pallas import tpu as pltpu

F32 = jnp.float32
BF16 = jnp.bfloat16
I32 = jnp.int32
I16 = jnp.int16

D_MODEL = 2048
D_PLE = 256
POOL_WIDTH = 1024
POOL_GROUPS = 4
POOL_GC = 256
POOL_WINDOWS = (2, 4, 8, 16)
ATTN_HEADS = 8
HEAD_DIM = 128
ATTN_WIDTH = 1024
IDX_HEADS = 16
IDX_DIM = 64
TOPK_MAX = 256
N_BUCKETS = 32
MAX_DISTANCE = 128
D_FF = 5632
CONV_WIDTH = 3
EPS = 1e-6
G_POOL, G_Q, G_K, G_V, G_QI = range(5)
N_GROUPS = 5
GROUP_W = 1024

V7X_VMEM_BYTES = 64 * 1024 * 1024
SUBLANES = 8
PACKED_SUBLANES = 16
VMEM_LIMIT = V7X_VMEM_BYTES - 8 * 1024 * 1024

TM_IN = 512
SMALL_W = IDX_DIM + IDX_HEADS
QB = 256
KC = 256
N_ACC = 4
TM_MIX = 512
POOL_HALO = 32
TM_FFN = 512
TF_FFN = 512
TF_HALF = 256
TM_PLE = 512
CAST_ROWS = 256

INT_MIN = -2 ** 31
I16_MIN = -2 ** 15
I16_MAX = 2 ** 15 - 1
NEG_BIG = -1e30
F32_MAX = float(np.finfo(np.float32).max)
LOG2E = math.log2(math.e)
Q_PRESCALE = (HEAD_DIM ** -0.5) * LOG2E


def _bucket_lower_bounds():
    n = np.arange(0, 4 * MAX_DISTANCE, dtype=np.int64)
    max_exact = N_BUCKETS // 2
    nf = np.maximum(n, 1).astype(np.float32)
    large = max_exact + (np.log(nf / np.float32(max_exact)) / np.float32(math.log(MAX_DISTANCE / max_exact))
                         * np.float32(N_BUCKETS - max_exact)).astype(np.int32)
    large = np.minimum(large, N_BUCKETS - 1)
    bucket = np.where(n < max_exact, n, large)
    assert np.all(np.diff(bucket) >= 0)
    lows = [int(np.argmax(bucket == b)) for b in range(N_BUCKETS)]
    assert all(bucket[lo] == b for b, lo in enumerate(lows))
    return lows


BUCKET_LO = _bucket_lower_bounds()


def _rms(x, g):
    ms = jnp.mean(x * x, axis=-1, keepdims=True)
    return (x * lax.rsqrt(ms + EPS)) * g


def _nt_dot(a, b):
    return lax.dot_general(a, b, (((1,), (1,)), ((), ())), preferred_element_type=F32)


def _inproj_kernel(x_ref, g_ref, wt_ref, pool_ref, k_ref, qT_ref, qiT_ref, vT_ref, small_ref):
    hb = _rms(x_ref[...], g_ref[...]).astype(BF16)

    def w(group):
        return wt_ref[group * GROUP_W:(group + 1) * GROUP_W, :]

    pool_ref[...] = _nt_dot(hb, w(G_POOL))
    k_ref[...] = _nt_dot(hb, w(G_K)).astype(BF16)
    small_ref[...] = _nt_dot(hb, wt_ref[N_GROUPS * GROUP_W:, :])
    qT_ref[...] = (_nt_dot(w(G_Q), hb) * Q_PRESCALE).astype(BF16)
    qiT_ref[...] = _nt_dot(w(G_QI), hb).astype(BF16)
    out_t = _nt_dot(w(G_V), hb).astype(BF16)
    for j in range(TM_IN // KC):
        vT_ref[j] = out_t[:, j * KC:(j + 1) * KC]


def _inproj(x2d, g, wt, bsz, seq):
    m = x2d.shape[0]
    tiles_per_seq = seq // TM_IN
    assert wt.shape[0] == N_GROUPS * GROUP_W + SMALL_W
    return pl.pallas_call(
        _inproj_kernel,
        grid=(m // TM_IN,),
        in_specs=[
            pl.BlockSpec((TM_IN, D_MODEL), lambda i: (i, 0)),
            pl.BlockSpec((1, D_MODEL), lambda i: (0, 0)),
            pl.BlockSpec(wt.shape, lambda i: (0, 0), pipeline_mode=pl.Buffered(1)),
        ],
        out_specs=[
            pl.BlockSpec((TM_IN, GROUP_W), lambda i: (i, 0)),
            pl.BlockSpec((TM_IN, GROUP_W), lambda i: (i, 0)),
            pl.BlockSpec((None, GROUP_W, TM_IN), lambda i: (i // tiles_per_seq, 0, i % tiles_per_seq)),
            pl.BlockSpec((None, GROUP_W, TM_IN), lambda i: (i // tiles_per_seq, 0, i % tiles_per_seq)),
            pl.BlockSpec((None, TM_IN // KC, GROUP_W, KC),
                         lambda i: (i // tiles_per_seq, i % tiles_per_seq, 0, 0)),
            pl.BlockSpec((TM_IN, SMALL_W), lambda i: (i, 0)),
        ],
        out_shape=[
            jax.ShapeDtypeStruct((m, POOL_WIDTH), F32),
            jax.ShapeDtypeStruct((m, ATTN_WIDTH), BF16),
            jax.ShapeDtypeStruct((bsz, ATTN_WIDTH, seq), BF16),
            jax.ShapeDtypeStruct((bsz, IDX_HEADS * IDX_DIM, seq), BF16),
            jax.ShapeDtypeStruct((bsz, seq // KC, ATTN_WIDTH, KC), BF16),
            jax.ShapeDtypeStruct((m, SMALL_W), F32),
        ],
        compiler_params=pltpu.CompilerParams(
            dimension_semantics=("arbitrary",), vmem_limit_bytes=VMEM_LIMIT),
        name="inproj",
    )(x2d, g, wt)


def _dsa_kernel(rel_ref, qT_ref, qiT_ref, wT_ref, qiTn_ref, wTn_ref, kidx_ref, k_ref, vT_ref, *rest,
                topk, n_q, n_side):
    side_in, rest = rest[:n_side], rest[n_side:]
    o_ref, side_out = rest[0], rest[1:1 + n_side]
    sc_s, sb_s, madd_s, bias_s, mx_s, alpha_s, acc_s, s_s = rest[1 + n_side:]
    b = pl.program_id(0)
    i = pl.program_id(1)
    n_ch = i + 1

    for src, dst in zip(side_in, side_out):
        dst[...] = src[...].astype(dst.dtype)

    @pl.when((b == 0) & (i == 0))
    def _():
        r = lax.broadcasted_iota(I32, (KC, QB), 0)
        c = lax.broadcasted_iota(I32, (KC, QB), 1)
        for off in range(2):
            dist = jnp.maximum(c - r + off * QB, 0)

            def head_body(h, carry, dist=dist, off=off):
                t = jnp.full((KC, QB), rel_ref[0, h], F32)
                for bk in range(1, N_BUCKETS):
                    t = jnp.where(dist >= BUCKET_LO[bk], rel_ref[bk, h], t)
                bias_s[off, h] = (t - rel_ref[N_BUCKETS - 1, h]) * LOG2E
                return carry

            lax.fori_loop(0, ATTN_HEADS, head_body, 0)

    idx_scale = (IDX_HEADS ** -0.5) * (IDX_DIM ** -0.5)
    row = lax.broadcasted_iota(I32, (KC, QB), 0)
    col = lax.broadcasted_iota(I32, (KC, QB), 1)

    def index_chunk(m, qi_ref, w_scaled, q_block):
        kc = kidx_ref[pl.ds(pl.multiple_of(m * KC, KC), KC), :]
        acc = jnp.zeros((KC, QB), F32)
        for h in range(IDX_HEADS):
            d = jnp.dot(kc, qi_ref[h * IDX_DIM:(h + 1) * IDX_DIM, :],
                        preferred_element_type=F32)
            acc = acc + w_scaled[h:h + 1, :] * jnp.maximum(d, 0.0)
        causal = (m * KC + row) <= (q_block * QB + col)
        score = jnp.where(causal, acc, -jnp.inf)
        sc_s[m] = score
        sb_s[m] = score.astype(BF16)

    @pl.when(i == 0)
    def _():
        index_chunk(0, qiT_ref, wT_ref[...] * idx_scale, 0)

    def code_to_score(code):
        bits = jnp.where(code >= 0, code, code ^ 0x7FFFFFFF)
        return pltpu.bitcast(bits, F32)

    def code16_to_bf16(code):
        bits = jnp.where(code >= 0, code, code ^ 0x7FFF)
        return pltpu.bitcast(bits.astype(I16), BF16)

    def spread(accs, ones, rows):
        accs = list(accs)
        for j in range(KC // rows):
            accs[j % N_ACC] = accs[j % N_ACC] + ones[j * rows:(j + 1) * rows, :]
        return tuple(accs)

    def total(accs, dtype):
        acc = functools.reduce(lambda a, b: a + b, accs)
        return jnp.sum(acc.astype(dtype), axis=0, keepdims=True)

    zero16 = (jnp.zeros((PACKED_SUBLANES, QB), I16),) * N_ACC
    zero8 = (jnp.zeros((SUBLANES, QB), F32),) * N_ACC

    def count16_ge(cand):
        def body(m, accs):
            return spread(accs, (sb_s[m] >= cand).astype(I16), PACKED_SUBLANES)

        return total(lax.fori_loop(0, n_ch, body, zero16), I32)

    def bit16_body(it, prefix):
        cand = prefix + jnp.left_shift(jnp.int32(1), 15 - it)
        return jnp.where(count16_ge(code16_to_bf16(cand)) >= topk, cand, prefix)

    p16 = lax.fori_loop(0, 16, bit16_body, jnp.full((1, QB), I16_MIN, I32))

    def fold(accs, flags):
        return spread(accs, jnp.where(flags, 1.0, 0.0), SUBLANES)

    def count_ge(cand):
        return total(lax.fori_loop(0, n_ch, lambda m, accs: fold(accs, sc_s[m] >= cand), zero8), F32)

    below = p16 - 1
    lo = jnp.left_shift(below, 16) + jnp.where(below < 0, 0xFFFF, 0)

    def bit_body(it, prefix):
        cand = prefix + jnp.left_shift(jnp.int32(1), 16 - it)
        return jnp.where(count_ge(code_to_score(cand)) >= topk, cand, prefix)

    prefix = lax.fori_loop(0, 17, bit_body, lo)
    thr = jnp.where(p16 == I16_MIN, -F32_MAX, code_to_score(prefix))

    def mask_body(m, kept):
        keep = sc_s[m] >= thr
        madd_s[m] = jnp.where(keep, 0.0, NEG_BIG)
        return fold(kept, keep)

    excess = total(lax.fori_loop(0, n_ch, mask_body, zero8), F32) - topk

    @pl.when(jnp.max(excess) > 0)
    def _():
        n_tied = total(lax.fori_loop(0, n_ch, lambda m, accs: fold(accs, sc_s[m] == thr), zero8), F32)
        need = n_tied - excess
        r = lax.broadcasted_iota(I32, (KC, KC), 0)
        c = lax.broadcasted_iota(I32, (KC, KC), 1)
        prefix_op = jnp.where(c <= r, 1.0, 0.0).astype(BF16)

        def fix_body(m, seen):
            tied = sc_s[m] == thr
            t = jnp.where(tied, 1.0, 0.0).astype(BF16)
            rank = seen + jnp.dot(prefix_op, t, preferred_element_type=F32)
            madd_s[m] = jnp.where(tied & (rank > need), NEG_BIG, madd_s[m])
            return rank[KC - 1:KC, :]

        lax.fori_loop(0, n_ch, fix_body, jnp.zeros((1, QB), F32))

    n_far = jnp.maximum(i - 1, 0)
    mx_s[...] = jnp.full(mx_s.shape, -jnp.inf, F32)
    acc_s[...] = jnp.zeros(acc_s.shape, F32)

    w_next = wTn_ref[...] * idx_scale

    def att_body(m, carry, near):
        index_chunk(m, qiTn_ref, w_next, i + 1)
        madd = madd_s[m]
        rows = pl.ds(pl.multiple_of(m * KC, KC), KC)
        for h in range(ATTN_HEADS):
            hd = slice(h * HEAD_DIM, (h + 1) * HEAD_DIM)
            s = jnp.dot(k_ref[rows, hd], qT_ref[hd, :], preferred_element_type=F32)
            if near:
                s = s + (madd + bias_s[i - m, h])
            else:
                s = s + madd
            s_s[h] = s
            mx = mx_s[h]
            m_new = jnp.maximum(mx, jnp.max(s, axis=0, keepdims=True))
            alpha_s[h] = jnp.exp2(mx - m_new)
            mx_s[h] = m_new
        ones_rows = jnp.ones((PACKED_SUBLANES, KC), BF16)
        for h in range(ATTN_HEADS):
            hd = slice(h * HEAD_DIM, (h + 1) * HEAD_DIM)
            p = jnp.exp2(s_s[h] - mx_s[h])
            v_aug = jnp.concatenate([vT_ref[m, hd, :], ones_rows], axis=0)
            pv = jnp.dot(v_aug, p.astype(BF16), preferred_element_type=F32)
            acc_s[h] = alpha_s[h] * acc_s[h] + pv
        return carry

    def far_pair(pair, carry):
        att_body(2 * pair, carry, near=False)
        att_body(2 * pair + 1, carry, near=False)
        return carry

    lax.fori_loop(0, n_far // 2, far_pair, 0)

    @pl.when(n_far % 2 == 1)
    def _():
        att_body(n_far - 1, 0, near=False)

    @pl.when(i >= 1)
    def _():
        att_body(i - 1, 0, near=True)

    att_body(i, 0, near=True)
    index_chunk(jnp.minimum(i + 1, n_q - 1), qiTn_ref, w_next, i + 1)

    for h in range(ATTN_HEADS):
        acc = acc_s[h]
        out_t = acc[:HEAD_DIM] / acc[HEAD_DIM:HEAD_DIM + 1]
        o_ref[:, h * HEAD_DIM:(h + 1) * HEAD_DIM] = out_t.T.astype(o_ref.dtype)


def _dsa(rel_bias, qT, qiT, wT, kidx, k, vT4, seq, side_weights):
    bsz = qT.shape[0]
    topk = min(TOPK_MAX, seq // 4)
    n_q = seq // QB
    n_steps = bsz * n_q
    kernel = functools.partial(_dsa_kernel, topk=topk, n_q=n_q, n_side=len(side_weights))
    nxt = lambda i: jnp.minimum(i + 1, n_q - 1)
    side_specs = []
    for w in side_weights:
        rows = w.shape[0] // n_steps
        assert rows * n_steps == w.shape[0] and rows % PACKED_SUBLANES == 0
        side_specs.append(pl.BlockSpec((rows, w.shape[1]), lambda b, i: (b * n_q + i, 0)))
    resident = dict(pipeline_mode=pl.Buffered(1))
    return pl.pallas_call(
        kernel,
        grid=(bsz, n_q),
        in_specs=[
            pl.BlockSpec(memory_space=pltpu.SMEM),
            pl.BlockSpec((None, ATTN_WIDTH, QB), lambda b, i: (b, 0, i)),
            pl.BlockSpec((None, IDX_HEADS * IDX_DIM, QB), lambda b, i: (b, 0, i)),
            pl.BlockSpec((None, IDX_HEADS, QB), lambda b, i: (b, 0, i)),
            pl.BlockSpec((None, IDX_HEADS * IDX_DIM, QB), lambda b, i: (b, 0, nxt(i))),
            pl.BlockSpec((None, IDX_HEADS, QB), lambda b, i: (b, 0, nxt(i))),
            pl.BlockSpec((None, seq, IDX_DIM), lambda b, i: (b, 0, 0)),
            pl.BlockSpec((None, seq, ATTN_WIDTH), lambda b, i: (b, 0, 0), **resident),
            pl.BlockSpec((None, seq // KC, ATTN_WIDTH, KC), lambda b, i: (b, 0, 0, 0), **resident),
        ] + side_specs,
        out_specs=[pl.BlockSpec((None, QB, ATTN_WIDTH), lambda b, i: (b, i, 0))] + side_specs,
        out_shape=[jax.ShapeDtypeStruct((bsz, seq, ATTN_WIDTH), BF16)]
        + [jax.ShapeDtypeStruct(w.shape, BF16) for w in side_weights],
        scratch_shapes=[
            pltpu.VMEM((seq // KC, KC, QB), F32),
            pltpu.VMEM((seq // KC, KC, QB), BF16),
            pltpu.VMEM((seq // KC, KC, QB), F32),
            pltpu.VMEM((2, ATTN_HEADS, KC, QB), F32),
            pltpu.VMEM((ATTN_HEADS, 1, QB), F32),
            pltpu.VMEM((ATTN_HEADS, 1, QB), F32),
            pltpu.VMEM((ATTN_HEADS, HEAD_DIM + PACKED_SUBLANES, QB), F32),
            pltpu.VMEM((ATTN_HEADS, KC, QB), F32),
        ],
        compiler_params=pltpu.CompilerParams(
            dimension_semantics=("arbitrary", "arbitrary"), vmem_limit_bytes=VMEM_LIMIT),
        name="dsa",
    )(rel_bias, qT, qiT, wT, qiT, wT, kidx, k, vT4, *side_weights)


def _mix_kernel(x_ref, u_ref, halo_ref, attn_ref, wpool_ref, pscale_ref, wout_ref, g_ref,
                x1_ref, h2_ref, ext_s, lvl0_s, lvl1_s, *, seq):
    m = pl.program_id(0)
    tiles_per_seq = seq // TM_MIX
    first = (m % tiles_per_seq) == 0
    ext_s[0:POOL_HALO, :] = jnp.where(first, 0.0, halo_ref[...])
    ext_s[POOL_HALO:, :] = u_ref[...]
    n_ext = TM_MIX + POOL_HALO
    lvl_bufs = (lvl0_s, lvl1_s)

    t_pos = (m % tiles_per_seq) * TM_MIX + lax.broadcasted_iota(I32, (TM_MIX, POOL_GC), 0)
    acc = x_ref[...] + jnp.dot(attn_ref[...], wout_ref[POOL_WIDTH:, :], preferred_element_type=F32)
    for g, w in enumerate(POOL_WINDOWS):
        cols = slice(g * POOL_GC, (g + 1) * POOL_GC)
        u = ext_s[POOL_HALO:, cols]
        levels = w.bit_length() - 1
        assert w == 1 << levels and SUBLANES * levels <= POOL_HALO
        src, src_cols = ext_s, cols
        for lv in range(levels):
            start = SUBLANES * (lv + 1)
            n_rows = n_ext - start
            t = src[pl.ds(start, n_rows), src_cols] + src[pl.ds(start - (1 << lv), n_rows), src_cols]
            if lv == levels - 1:
                wsum = t[POOL_HALO - start:, :]
            else:
                dst = lvl_bufs[lv % 2]
                dst[pl.ds(start, n_rows), :] = t
                src, src_cols = dst, slice(None)
        count = jnp.minimum(t_pos + 1, w).astype(F32)
        pooled = (wsum / count - u).astype(BF16)
        y = jnp.dot(pooled, wpool_ref[g], preferred_element_type=F32) * pscale_ref[:, cols]
        acc = acc + jnp.dot(y.astype(BF16), wout_ref[cols, :], preferred_element_type=F32)
    x1_ref[...] = acc
    h2_ref[...] = _rms(acc, g_ref[...]).astype(BF16)


def _mix(x2d, u_pool, attn2d, w_pool, pool_scale, w_out, g_ffn, seq):
    m = x2d.shape[0]
    halo_per_tile = TM_MIX // POOL_HALO
    kernel = functools.partial(_mix_kernel, seq=seq)
    return pl.pallas_call(
        kernel,
        grid=(m // TM_MIX,),
        in_specs=[
            pl.BlockSpec((TM_MIX, D_MODEL), lambda i: (i, 0)),
            pl.BlockSpec((TM_MIX, POOL_WIDTH), lambda i: (i, 0)),
            pl.BlockSpec((POOL_HALO, POOL_WIDTH), lambda i: (jnp.maximum(i * halo_per_tile - 1, 0), 0)),
            pl.BlockSpec((TM_MIX, ATTN_WIDTH), lambda i: (i, 0)),
            pl.BlockSpec((POOL_GROUPS, POOL_GC, POOL_GC), lambda i: (0, 0, 0)),
            pl.BlockSpec((1, POOL_WIDTH), lambda i: (0, 0)),
            pl.BlockSpec((D_MODEL, D_MODEL), lambda i: (0, 0)),
            pl.BlockSpec((1, D_MODEL), lambda i: (0, 0)),
        ],
        out_specs=[
            pl.BlockSpec((TM_MIX, D_MODEL), lambda i: (i, 0)),
            pl.BlockSpec((TM_MIX, D_MODEL), lambda i: (i, 0)),
        ],
        out_shape=[
            jax.ShapeDtypeStruct((m, D_MODEL), F32),
            jax.ShapeDtypeStruct((m, D_MODEL), BF16),
        ],
        scratch_shapes=[pltpu.VMEM((TM_MIX + POOL_HALO, POOL_WIDTH), F32),
                        pltpu.VMEM((TM_MIX + POOL_HALO, POOL_GC), F32),
                        pltpu.VMEM((TM_MIX + POOL_HALO, POOL_GC), F32)],
        compiler_params=pltpu.CompilerParams(
            dimension_semantics=("arbitrary",), vmem_limit_bytes=VMEM_LIMIT),
        name="mix",
    )(x2d, u_pool, u_pool, attn2d, w_pool, pool_scale, w_out, g_ffn)


def _ffn_kernel(h_ref, x1_ref, wg_ref, wv_ref, cwg_ref, cwv_ref, cbg_ref, cbv_ref, wd_ref,
                o_ref, carry_s, *u_bufs, seq):
    m = pl.program_id(0)
    f = pl.program_id(1)
    tiles_per_seq = seq // TM_FFN
    first = (m % tiles_per_seq) == 0

    @pl.when(f == 0)
    def _():
        o_ref[...] = x1_ref[...]

    h = h_ref[...]
    n_half = TF_FFN // TF_HALF

    def up(s):
        cols = slice(s * TF_HALF, (s + 1) * TF_HALF)
        for k, w_ref in enumerate((wg_ref, wv_ref)):
            u = u_bufs[2 * s + k]
            u[0:SUBLANES, :] = jnp.where(first, 0.0, carry_s[f, 2 * s + k])
            u[SUBLANES:, :] = jnp.dot(h, w_ref[:, cols], preferred_element_type=F32)
            carry_s[f, 2 * s + k] = u[TM_FFN:, :]

    def conv(u, cols, cw_ref, cb_ref):
        out = cb_ref[:, cols]
        for j in range(CONV_WIDTH):
            shift = CONV_WIDTH - 1 - j
            out = out + u[pl.ds(SUBLANES - shift, TM_FFN), :] * cw_ref[j:j + 1, cols]
        return out

    def gated(s):
        cols = slice(s * TF_HALF, (s + 1) * TF_HALF)
        gate = conv(u_bufs[2 * s], cols, cwg_ref, cbg_ref)
        val = conv(u_bufs[2 * s + 1], cols, cwv_ref, cbv_ref)
        return (gate * jax.nn.sigmoid(gate) * val).astype(BF16)

    def down(s, act):
        return jnp.dot(act, wd_ref[s * TF_HALF:(s + 1) * TF_HALF, :], preferred_element_type=F32)

    up(0)
    act0 = gated(0)
    up(1)
    contrib = down(0, act0)
    act1 = gated(1)
    contrib = contrib + down(1, act1)
    o_ref[...] += contrib


def _ffn(h2, x1, w_up, conv_w, conv_b, w_down, seq):
    m = h2.shape[0]
    n_f = D_FF // TF_FFN
    kernel = functools.partial(_ffn_kernel, seq=seq)
    return pl.pallas_call(
        kernel,
        grid=(m // TM_FFN, n_f),
        in_specs=[
            pl.BlockSpec((TM_FFN, D_MODEL), lambda i, f: (i, 0)),
            pl.BlockSpec((TM_FFN, D_MODEL), lambda i, f: (i, 0)),
            pl.BlockSpec((D_MODEL, TF_FFN), lambda i, f: (0, f)),
            pl.BlockSpec((D_MODEL, TF_FFN), lambda i, f: (0, f + n_f)),
            pl.BlockSpec((CONV_WIDTH, TF_FFN), lambda i, f: (0, f)),
            pl.BlockSpec((CONV_WIDTH, TF_FFN), lambda i, f: (0, f + n_f)),
            pl.BlockSpec((1, TF_FFN), lambda i, f: (0, f)),
            pl.BlockSpec((1, TF_FFN), lambda i, f: (0, f + n_f)),
            pl.BlockSpec((TF_FFN, D_MODEL), lambda i, f: (f, 0)),
        ],
        out_specs=pl.BlockSpec((TM_FFN, D_MODEL), lambda i, f: (i, 0)),
        out_shape=jax.ShapeDtypeStruct((m, D_MODEL), F32),
        scratch_shapes=[pltpu.VMEM((n_f, 2 * (TF_FFN // TF_HALF), SUBLANES, TF_HALF), F32)]
        + [pltpu.VMEM((TM_FFN + SUBLANES, TF_HALF), F32)] * (2 * (TF_FFN // TF_HALF)),
        compiler_params=pltpu.CompilerParams(
            dimension_semantics=("arbitrary", "arbitrary"), vmem_limit_bytes=VMEM_LIMIT),
        name="ffn",
    )(h2, x1, w_up, w_up, conv_w, conv_w, conv_b, conv_b, w_down)


def _ple_kernel(x_ref, p_ref, gp_ref, wgate32_ref, wproj_ref, gf_ref, o_ref, wgate_ref, *, last_layer):
    @pl.when(pl.program_id(0) == 0)
    def _():
        for r in range(0, D_MODEL, CAST_ROWS):
            wgate_ref[r:r + CAST_ROWS, :] = wgate32_ref[r:r + CAST_ROWS, :].astype(BF16)

    x = x_ref[...]
    hg = _rms(x, gp_ref[...]).astype(BF16)
    gate = jax.nn.sigmoid(jnp.dot(hg, wgate_ref[...], preferred_element_type=F32))
    emb = jnp.dot(p_ref[...].astype(BF16), wproj_ref[...], preferred_element_type=F32)
    x3 = x + emb * gate
    o_ref[...] = _rms(x3, gf_ref[...]) if last_layer else x3


def _ple(x2, p2d, g_ple, w_gate, w_proj, g_final, last_layer):
    m = x2.shape[0]
    return pl.pallas_call(
        functools.partial(_ple_kernel, last_layer=last_layer),
        grid=(m // TM_PLE,),
        in_specs=[
            pl.BlockSpec((TM_PLE, D_MODEL), lambda i: (i, 0)),
            pl.BlockSpec((TM_PLE, D_PLE), lambda i: (i, 0)),
            pl.BlockSpec((1, D_MODEL), lambda i: (0, 0)),
            pl.BlockSpec((D_MODEL, D_MODEL), lambda i: (0, 0), pipeline_mode=pl.Buffered(1)),
            pl.BlockSpec((D_PLE, D_MODEL), lambda i: (0, 0)),
            pl.BlockSpec((1, D_MODEL), lambda i: (0, 0)),
        ],
        out_specs=pl.BlockSpec((TM_PLE, D_MODEL), lambda i: (i, 0)),
        out_shape=jax.ShapeDtypeStruct((m, D_MODEL), F32),
        scratch_shapes=[pltpu.VMEM((D_MODEL, D_MODEL), BF16)],
        compiler_params=pltpu.CompilerParams(
            dimension_semantics=("arbitrary",), vmem_limit_bytes=VMEM_LIMIT),
        name="ple",
    )(x2, p2d, g_ple, w_gate, w_proj, g_final)


def kernel(x, p, g_mix, w_in, w_pool, pool_scale, rel_bias, w_out, g_ffn, w_up, conv_w, conv_b,
           w_down, g_ple, w_ple_gate, w_ple_proj, g_final):
    bsz, seq, _ = x.shape
    depth = w_in.shape[0]
    assert seq % QB == 0 and seq % TM_MIX == 0 and seq % TM_FFN == 0 and (bsz * seq) % TM_IN == 0
    x2d = x.reshape(bsz * seq, D_MODEL)
    for i in range(depth):
        u_pool, k2d, qT, qiT, vT4, small = _inproj(x2d, g_mix[i].reshape(1, D_MODEL),
                                                   w_in[i].T.astype(BF16), bsz, seq)
        small3 = small.reshape(bsz, seq, SMALL_W)
        kidx = small3[:, :, :IDX_DIM].astype(BF16)
        wT = jnp.swapaxes(small3[:, :, IDX_DIM:IDX_DIM + IDX_HEADS], 1, 2)
        attn, w_out_b, w_up_b, w_down_b = _dsa(rel_bias, qT, qiT, wT, kidx,
                                               k2d.reshape(bsz, seq, ATTN_WIDTH), vT4, seq,
                                               (w_out[i], w_up[i], w_down[i]))

        x1, h2 = _mix(x2d, u_pool, attn.reshape(bsz * seq, ATTN_WIDTH), w_pool[i].astype(BF16),
                      pool_scale[i].reshape(1, POOL_WIDTH), w_out_b,
                      g_ffn[i].reshape(1, D_MODEL), seq)
        x2 = _ffn(h2, x1, w_up_b, conv_w[i], conv_b[i].reshape(1, 2 * D_FF), w_down_b, seq)
        x2d = _ple(x2, p[i].reshape(bsz * seq, D_PLE), g_ple[i].reshape(1, D_MODEL),
                   w_ple_gate[i], w_ple_proj[i].astype(BF16),
                   g_final.reshape(1, D_MODEL), last_layer=(i == depth - 1))
    return x2d.reshape(bsz, seq, D_MODEL)
```

```python
import functools
import math

import numpy as np
import jax
import jax.numpy as jnp
from jax import lax
from jax.experimental import pallas as pl
from jax.experimental.pallas import tpu as pltpu

F32 = jnp.float32
BF16 = jnp.bfloat16
I32 = jnp.int32
I16 = jnp.int16

D_MODEL = 2048
D_PLE = 256
POOL_WIDTH = 1024
POOL_GROUPS = 4
POOL_GC = 256
POOL_WINDOWS = (2, 4, 8, 16)
ATTN_HEADS = 8
HEAD_DIM = 128
ATTN_WIDTH = 1024
IDX_HEADS = 16
IDX_DIM = 64
TOPK_MAX = 256
N_BUCKETS = 32
MAX_DISTANCE = 128
D_FF = 5632
CONV_WIDTH = 3
EPS = 1e-6
G_POOL, G_Q, G_K, G_V, G_QI = range(5)
N_GROUPS = 5
GROUP_W = 1024

V7X_VMEM_BYTES = 64 * 1024 * 1024
SUBLANES = 8
PACKED_SUBLANES = 16
VMEM_LIMIT = V7X_VMEM_BYTES - 8 * 1024 * 1024

TM_IN = 512
SMALL_W = IDX_DIM + IDX_HEADS
QB = 256
KC = 256
N_ACC = 4
L2_PASS_GROUPS = (5, 4, 4, 4)
TM_MIX = 512
POOL_HALO = 32
TM_FFN = 512
TF_FFN = 512
TF_HALF = 256
TM_PLE = 512
CAST_ROWS = 256

INT_MIN = -2 ** 31
I16_MIN = -2 ** 15
I16_MAX = 2 ** 15 - 1
NEG_BIG = -1e30
F32_MAX = float(np.finfo(np.float32).max)
LOG2E = math.log2(math.e)
Q_PRESCALE = (HEAD_DIM ** -0.5) * LOG2E


def _bucket_lower_bounds():
    n = np.arange(0, 4 * MAX_DISTANCE, dtype=np.int64)
    max_exact = N_BUCKETS // 2
    nf = np.maximum(n, 1).astype(np.float32)
    large = max_exact + (np.log(nf / np.float32(max_exact)) / np.float32(math.log(MAX_DISTANCE / max_exact))
                         * np.float32(N_BUCKETS - max_exact)).astype(np.int32)
    large = np.minimum(large, N_BUCKETS - 1)
    bucket = np.where(n < max_exact, n, large)
    assert np.all(np.diff(bucket) >= 0)
    lows = [int(np.argmax(bucket == b)) for b in range(N_BUCKETS)]
    assert all(bucket[lo] == b for b, lo in enumerate(lows))
    return lows


BUCKET_LO = _bucket_lower_bounds()


def _rms(x, g):
    ms = jnp.mean(x * x, axis=-1, keepdims=True)
    return (x * lax.rsqrt(ms + EPS)) * g


def _nt_dot(a, b):
    return lax.dot_general(a, b, (((1,), (1,)), ((), ())), preferred_element_type=F32)


def _inproj_kernel(x_ref, g_ref, wt_ref, pool_ref, k_ref, qT_ref, qiT_ref, vT_ref, small_ref):
    hb = _rms(x_ref[...], g_ref[...]).astype(BF16)

    def w(group):
        return wt_ref[group * GROUP_W:(group + 1) * GROUP_W, :]

    pool_ref[...] = _nt_dot(hb, w(G_POOL))
    k_ref[...] = _nt_dot(hb, w(G_K)).astype(BF16)
    small_ref[...] = _nt_dot(hb, wt_ref[N_GROUPS * GROUP_W:, :])
    qT_ref[...] = (_nt_dot(w(G_Q), hb) * Q_PRESCALE).astype(BF16)
    qiT_ref[...] = _nt_dot(w(G_QI), hb).astype(BF16)
    out_t = _nt_dot(w(G_V), hb).astype(BF16)
    for j in range(TM_IN // KC):
        vT_ref[j] = out_t[:, j * KC:(j + 1) * KC]


def _inproj(x2d, g, wt, bsz, seq):
    m = x2d.shape[0]
    tiles_per_seq = seq // TM_IN
    assert wt.shape[0] == N_GROUPS * GROUP_W + SMALL_W
    return pl.pallas_call(
        _inproj_kernel,
        grid=(m // TM_IN,),
        in_specs=[
            pl.BlockSpec((TM_IN, D_MODEL), lambda i: (i, 0)),
            pl.BlockSpec((1, D_MODEL), lambda i: (0, 0)),
            pl.BlockSpec(wt.shape, lambda i: (0, 0), pipeline_mode=pl.Buffered(1)),
        ],
        out_specs=[
            pl.BlockSpec((TM_IN, GROUP_W), lambda i: (i, 0)),
            pl.BlockSpec((TM_IN, GROUP_W), lambda i: (i, 0)),
            pl.BlockSpec((None, GROUP_W, TM_IN), lambda i: (i // tiles_per_seq, 0, i % tiles_per_seq)),
            pl.BlockSpec((None, GROUP_W, TM_IN), lambda i: (i // tiles_per_seq, 0, i % tiles_per_seq)),
            pl.BlockSpec((None, TM_IN // KC, GROUP_W, KC),
                         lambda i: (i // tiles_per_seq, i % tiles_per_seq, 0, 0)),
            pl.BlockSpec((TM_IN, SMALL_W), lambda i: (i, 0)),
        ],
        out_shape=[
            jax.ShapeDtypeStruct((m, POOL_WIDTH), F32),
            jax.ShapeDtypeStruct((m, ATTN_WIDTH), BF16),
            jax.ShapeDtypeStruct((bsz, ATTN_WIDTH, seq), BF16),
            jax.ShapeDtypeStruct((bsz, IDX_HEADS * IDX_DIM, seq), BF16),
            jax.ShapeDtypeStruct((bsz, seq // KC, ATTN_WIDTH, KC), BF16),
            jax.ShapeDtypeStruct((m, SMALL_W), F32),
        ],
        compiler_params=pltpu.CompilerParams(
            dimension_semantics=("arbitrary",), vmem_limit_bytes=VMEM_LIMIT),
        name="inproj",
    )(x2d, g, wt)


def _dsa_kernel(rel_ref, qT_ref, qiT_ref, wT_ref, qiTn_ref, wTn_ref, kidx_ref, k_ref, vT_ref, *rest,
                topk, n_q, n_side):
    side_in, rest = rest[:n_side], rest[n_side:]
    o_ref, side_out = rest[0], rest[1:1 + n_side]
    sc_s, sb_s, madd_s, bias_s, mx_s, alpha_s, acc_s, s_s = rest[1 + n_side:]
    b = pl.program_id(0)
    i = pl.program_id(1)
    n_ch = i + 1

    for src, dst in zip(side_in, side_out):
        dst[...] = src[...].astype(dst.dtype)

    @pl.when((b == 0) & (i == 0))
    def _():
        r = lax.broadcasted_iota(I32, (KC, QB), 0)
        c = lax.broadcasted_iota(I32, (KC, QB), 1)
        for off in range(2):
            dist = jnp.maximum(c - r + off * QB, 0)

            def head_body(h, carry, dist=dist, off=off):
                t = jnp.full((KC, QB), rel_ref[0, h], F32)
                for bk in range(1, N_BUCKETS):
                    t = jnp.where(dist >= BUCKET_LO[bk], rel_ref[bk, h], t)
                bias_s[off, h] = (t - rel_ref[N_BUCKETS - 1, h]) * LOG2E
                return carry

            lax.fori_loop(0, ATTN_HEADS, head_body, 0)

    idx_scale = (IDX_HEADS ** -0.5) * (IDX_DIM ** -0.5)
    row = lax.broadcasted_iota(I32, (KC, QB), 0)
    col = lax.broadcasted_iota(I32, (KC, QB), 1)

    def index_chunk(m, qi_ref, w_scaled, q_block):
        kc = kidx_ref[pl.ds(pl.multiple_of(m * KC, KC), KC), :]
        acc = jnp.zeros((KC, QB), F32)
        for h in range(IDX_HEADS):
            d = jnp.dot(kc, qi_ref[h * IDX_DIM:(h + 1) * IDX_DIM, :],
                        preferred_element_type=F32)
            acc = acc + w_scaled[h:h + 1, :] * jnp.maximum(d, 0.0)
        causal = (m * KC + row) <= (q_block * QB + col)
        score = jnp.where(causal, acc, -jnp.inf)
        sc_s[m] = score
        sb_s[m] = score.astype(BF16)

    @pl.when(i == 0)
    def _():
        index_chunk(0, qiT_ref, wT_ref[...] * idx_scale, 0)

    def code_to_score(code):
        bits = jnp.where(code >= 0, code, code ^ 0x7FFFFFFF)
        return pltpu.bitcast(bits, F32)

    def code16_to_bf16(code):
        bits = jnp.where(code >= 0, code, code ^ 0x7FFF)
        return pltpu.bitcast(bits.astype(I16), BF16)

    def spread(accs, ones, rows):
        accs = list(accs)
        for j in range(KC // rows):
            accs[j % N_ACC] = accs[j % N_ACC] + ones[j * rows:(j + 1) * rows, :]
        return tuple(accs)

    def total(accs, dtype):
        acc = functools.reduce(lambda a, b: a + b, accs)
        return jnp.sum(acc.astype(dtype), axis=0, keepdims=True)

    zero16 = (jnp.zeros((PACKED_SUBLANES, QB), I16),) * N_ACC
    zero8 = (jnp.zeros((SUBLANES, QB), F32),) * N_ACC

    def count16_ge(cand):
        def body(m, accs):
            return spread(accs, (sb_s[m] >= cand).astype(I16), PACKED_SUBLANES)

        return total(lax.fori_loop(0, n_ch, body, zero16), I32)

    def bit16_body(it, prefix):
        cand = prefix + jnp.left_shift(jnp.int32(1), 15 - it)
        return jnp.where(count16_ge(code16_to_bf16(cand)) >= topk, cand, prefix)

    p16 = lax.fori_loop(0, 16, bit16_body, jnp.full((1, QB), I16_MIN, I32))

    def fold(accs, flags):
        return spread(accs, jnp.where(flags, 1.0, 0.0), SUBLANES)

    def count_ge(cand):
        return total(lax.fori_loop(0, n_ch, lambda m, accs: fold(accs, sc_s[m] >= cand), zero8), F32)

    below = p16 - 1
    lo = jnp.left_shift(below, 16) + jnp.where(below < 0, 0xFFFF, 0)

    def bit_body(it, state):
        prefix, kept = state
        cand = prefix + jnp.left_shift(jnp.int32(1), 16 - it)
        cnt = count_ge(code_to_score(cand))
        ok = cnt >= topk
        return jnp.where(ok, cand, prefix), jnp.where(ok, cnt, kept)

    settled_at = jnp.where(p16 == I16_MIN, float(topk), float(topk + 1))
    state = (lo, settled_at)
    done_bits = 0
    for n_bits in L2_PASS_GROUPS:
        first_bit = done_bits

        def run(st, first_bit=first_bit, n_bits=n_bits):
            return lax.fori_loop(first_bit, first_bit + n_bits, bit_body, st)

        if done_bits == 0:
            state = run(state)
        else:
            all_settled = jnp.max(jnp.abs(state[1] - topk)) == 0.0
            state = lax.cond(all_settled, lambda st: st, run, state)
        done_bits += n_bits
    assert done_bits == 17
    prefix = state[0]
    thr = jnp.where(p16 == I16_MIN, -F32_MAX, code_to_score(prefix))

    def mask_body(m, kept):
        keep = sc_s[m] >= thr
        madd_s[m] = jnp.where(keep, 0.0, NEG_BIG)
        return fold(kept, keep)

    excess = total(lax.fori_loop(0, n_ch, mask_body, zero8), F32) - topk

    @pl.when(jnp.max(excess) > 0)
    def _():
        n_tied = total(lax.fori_loop(0, n_ch, lambda m, accs: fold(accs, sc_s[m] == thr), zero8), F32)
        need = n_tied - excess
        r = lax.broadcasted_iota(I32, (KC, KC), 0)
        c = lax.broadcasted_iota(I32, (KC, KC), 1)
        prefix_op = jnp.where(c <= r, 1.0, 0.0).astype(BF16)

        def fix_body(m, seen):
            tied = sc_s[m] == thr
            t = jnp.where(tied, 1.0, 0.0).astype(BF16)
            rank = seen + jnp.dot(prefix_op, t, preferred_element_type=F32)
            madd_s[m] = jnp.where(tied & (rank > need), NEG_BIG, madd_s[m])
            return rank[KC - 1:KC, :]

        lax.fori_loop(0, n_ch, fix_body, jnp.zeros((1, QB), F32))

    n_far = jnp.maximum(i - 1, 0)
    mx_s[...] = jnp.full(mx_s.shape, -jnp.inf, F32)
    acc_s[...] = jnp.zeros(acc_s.shape, F32)

    w_next = wTn_ref[...] * idx_scale

    def att_body(m, carry, near):
        index_chunk(m, qiTn_ref, w_next, i + 1)
        madd = madd_s[m]
        rows = pl.ds(pl.multiple_of(m * KC, KC), KC)
        for h in range(ATTN_HEADS):
            hd = slice(h * HEAD_DIM, (h + 1) * HEAD_DIM)
            s = jnp.dot(k_ref[rows, hd], qT_ref[hd, :], preferred_element_type=F32)
            if near:
                s = s + (madd + bias_s[i - m, h])
            else:
                s = s + madd
            s_s[h] = s
            mx = mx_s[h]
            m_new = jnp.maximum(mx, jnp.max(s, axis=0, keepdims=True))
            alpha_s[h] = jnp.exp2(mx - m_new)
            mx_s[h] = m_new
        ones_rows = jnp.ones((PACKED_SUBLANES, KC), BF16)
        for h in range(ATTN_HEADS):
            hd = slice(h * HEAD_DIM, (h + 1) * HEAD_DIM)
            p = jnp.exp2(s_s[h] - mx_s[h])
            v_aug = jnp.concatenate([vT_ref[m, hd, :], ones_rows], axis=0)
            pv = jnp.dot(v_aug, p.astype(BF16), preferred_element_type=F32)
            acc_s[h] = alpha_s[h] * acc_s[h] + pv
        return carry

    def far_pair(pair, carry):
        att_body(2 * pair, carry, near=False)
        att_body(2 * pair + 1, carry, near=False)
        return carry

    lax.fori_loop(0, n_far // 2, far_pair, 0)

    @pl.when(n_far % 2 == 1)
    def _():
        att_body(n_far - 1, 0, near=False)

    @pl.when(i >= 1)
    def _():
        att_body(i - 1, 0, near=True)

    att_body(i, 0, near=True)
    index_chunk(jnp.minimum(i + 1, n_q - 1), qiTn_ref, w_next, i + 1)

    for h in range(ATTN_HEADS):
        acc = acc_s[h]
        out_t = acc[:HEAD_DIM] / acc[HEAD_DIM:HEAD_DIM + 1]
        o_ref[:, h * HEAD_DIM:(h + 1) * HEAD_DIM] = out_t.T.astype(o_ref.dtype)


def _dsa(rel_bias, qT, qiT, wT, kidx, k, vT4, seq, side_weights):
    bsz = qT.shape[0]
    topk = min(TOPK_MAX, seq // 4)
    n_q = seq // QB
    n_steps = bsz * n_q
    kernel = functools.partial(_dsa_kernel, topk=topk, n_q=n_q, n_side=len(side_weights))
    nxt = lambda i: jnp.minimum(i + 1, n_q - 1)
    side_specs = []
    for w in side_weights:
        rows = w.shape[0] // n_steps
        assert rows * n_steps == w.shape[0] and rows % PACKED_SUBLANES == 0
        side_specs.append(pl.BlockSpec((rows, w.shape[1]), lambda b, i: (b * n_q + i, 0)))
    resident = dict(pipeline_mode=pl.Buffered(1))
    return pl.pallas_call(
        kernel,
        grid=(bsz, n_q),
        in_specs=[
            pl.BlockSpec(memory_space=pltpu.SMEM),
            pl.BlockSpec((None, ATTN_WIDTH, QB), lambda b, i: (b, 0, i)),
            pl.BlockSpec((None, IDX_HEADS * IDX_DIM, QB), lambda b, i: (b, 0, i)),
            pl.BlockSpec((None, IDX_HEADS, QB), lambda b, i: (b, 0, i)),
            pl.BlockSpec((None, IDX_HEADS * IDX_DIM, QB), lambda b, i: (b, 0, nxt(i))),
            pl.BlockSpec((None, IDX_HEADS, QB), lambda b, i: (b, 0, nxt(i))),
            pl.BlockSpec((None, seq, IDX_DIM), lambda b, i: (b, 0, 0)),
            pl.BlockSpec((None, seq, ATTN_WIDTH), lambda b, i: (b, 0, 0), **resident),
            pl.BlockSpec((None, seq // KC, ATTN_WIDTH, KC), lambda b, i: (b, 0, 0, 0), **resident),
        ] + side_specs,
        out_specs=[pl.BlockSpec((None, QB, ATTN_WIDTH), lambda b, i: (b, i, 0))] + side_specs,
        out_shape=[jax.ShapeDtypeStruct((bsz, seq, ATTN_WIDTH), BF16)]
        + [jax.ShapeDtypeStruct(w.shape, BF16) for w in side_weights],
        scratch_shapes=[
            pltpu.VMEM((seq // KC, KC, QB), F32),
            pltpu.VMEM((seq // KC, KC, QB), BF16),
            pltpu.VMEM((seq // KC, KC, QB), F32),
            pltpu.VMEM((2, ATTN_HEADS, KC, QB), F32),
            pltpu.VMEM((ATTN_HEADS, 1, QB), F32),
            pltpu.VMEM((ATTN_HEADS, 1, QB), F32),
            pltpu.VMEM((ATTN_HEADS, HEAD_DIM + PACKED_SUBLANES, QB), F32),
            pltpu.VMEM((ATTN_HEADS, KC, QB), F32),
        ],
        compiler_params=pltpu.CompilerParams(
            dimension_semantics=("arbitrary", "arbitrary"), vmem_limit_bytes=VMEM_LIMIT),
        name="dsa",
    )(rel_bias, qT, qiT, wT, qiT, wT, kidx, k, vT4, *side_weights)


def _mix_kernel(x_ref, u_ref, halo_ref, attn_ref, wpool_ref, pscale_ref, wout_ref, g_ref,
                x1_ref, h2_ref, ext_s, lvl0_s, lvl1_s, *, seq):
    m = pl.program_id(0)
    tiles_per_seq = seq // TM_MIX
    first = (m % tiles_per_seq) == 0
    ext_s[0:POOL_HALO, :] = jnp.where(first, 0.0, halo_ref[...])
    ext_s[POOL_HALO:, :] = u_ref[...]
    n_ext = TM_MIX + POOL_HALO
    lvl_bufs = (lvl0_s, lvl1_s)

    t_pos = (m % tiles_per_seq) * TM_MIX + lax.broadcasted_iota(I32, (TM_MIX, POOL_GC), 0)
    acc = x_ref[...] + jnp.dot(attn_ref[...], wout_ref[POOL_WIDTH:, :], preferred_element_type=F32)
    for g, w in enumerate(POOL_WINDOWS):
        cols = slice(g * POOL_GC, (g + 1) * POOL_GC)
        u = ext_s[POOL_HALO:, cols]
        levels = w.bit_length() - 1
        assert w == 1 << levels and SUBLANES * levels <= POOL_HALO
        src, src_cols = ext_s, cols
        for lv in range(levels):
            start = SUBLANES * (lv + 1)
            n_rows = n_ext - start
            t = src[pl.ds(start, n_rows), src_cols] + src[pl.ds(start - (1 << lv), n_rows), src_cols]
            if lv == levels - 1:
                wsum = t[POOL_HALO - start:, :]
            else:
                dst = lvl_bufs[lv % 2]
                dst[pl.ds(start, n_rows), :] = t
                src, src_cols = dst, slice(None)
        count = jnp.minimum(t_pos + 1, w).astype(F32)
        pooled = (wsum / count - u).astype(BF16)
        y = jnp.dot(pooled, wpool_ref[g], preferred_element_type=F32) * pscale_ref[:, cols]
        acc = acc + jnp.dot(y.astype(BF16), wout_ref[cols, :], preferred_element_type=F32)
    x1_ref[...] = acc
    h2_ref[...] = _rms(acc, g_ref[...]).astype(BF16)


def _mix(x2d, u_pool, attn2d, w_pool, pool_scale, w_out, g_ffn, seq):
    m = x2d.shape[0]
    halo_per_tile = TM_MIX // POOL_HALO
    kernel = functools.partial(_mix_kernel, seq=seq)
    return pl.pallas_call(
        kernel,
        grid=(m // TM_MIX,),
        in_specs=[
            pl.BlockSpec((TM_MIX, D_MODEL), lambda i: (i, 0)),
            pl.BlockSpec((TM_MIX, POOL_WIDTH), lambda i: (i, 0)),
            pl.BlockSpec((POOL_HALO, POOL_WIDTH), lambda i: (jnp.maximum(i * halo_per_tile - 1, 0), 0)),
            pl.BlockSpec((TM_MIX, ATTN_WIDTH), lambda i: (i, 0)),
            pl.BlockSpec((POOL_GROUPS, POOL_GC, POOL_GC), lambda i: (0, 0, 0)),
            pl.BlockSpec((1, POOL_WIDTH), lambda i: (0, 0)),
            pl.BlockSpec((D_MODEL, D_MODEL), lambda i: (0, 0)),
            pl.BlockSpec((1, D_MODEL), lambda i: (0, 0)),
        ],
        out_specs=[
            pl.BlockSpec((TM_MIX, D_MODEL), lambda i: (i, 0)),
            pl.BlockSpec((TM_MIX, D_MODEL), lambda i: (i, 0)),
        ],
        out_shape=[
            jax.ShapeDtypeStruct((m, D_MODEL), F32),
            jax.ShapeDtypeStruct((m, D_MODEL), BF16),
        ],
        scratch_shapes=[pltpu.VMEM((TM_MIX + POOL_HALO, POOL_WIDTH), F32),
                        pltpu.VMEM((TM_MIX + POOL_HALO, POOL_GC), F32),
                        pltpu.VMEM((TM_MIX + POOL_HALO, POOL_GC), F32)],
        compiler_params=pltpu.CompilerParams(
            dimension_semantics=("arbitrary",), vmem_limit_bytes=VMEM_LIMIT),
        name="mix",
    )(x2d, u_pool, u_pool, attn2d, w_pool, pool_scale, w_out, g_ffn)


def _ffn_kernel(h_ref, x1_ref, wg_ref, wv_ref, cwg_ref, cwv_ref, cbg_ref, cbv_ref, wd_ref,
                o_ref, carry_s, *u_bufs, seq):
    m = pl.program_id(0)
    f = pl.program_id(1)
    tiles_per_seq = seq // TM_FFN
    first = (m % tiles_per_seq) == 0

    @pl.when(f == 0)
    def _():
        o_ref[...] = x1_ref[...]

    h = h_ref[...]
    n_half = TF_FFN // TF_HALF

    def up(s):
        cols = slice(s * TF_HALF, (s + 1) * TF_HALF)
        for k, w_ref in enumerate((wg_ref, wv_ref)):
            u = u_bufs[2 * s + k]
            u[0:SUBLANES, :] = jnp.where(first, 0.0, carry_s[f, 2 * s + k])
            u[SUBLANES:, :] = jnp.dot(h, w_ref[:, cols], preferred_element_type=F32)
            carry_s[f, 2 * s + k] = u[TM_FFN:, :]

    def conv(u, cols, cw_ref, cb_ref):
        out = cb_ref[:, cols]
        for j in range(CONV_WIDTH):
            shift = CONV_WIDTH - 1 - j
            out = out + u[pl.ds(SUBLANES - shift, TM_FFN), :] * cw_ref[j:j + 1, cols]
        return out

    def gated(s):
        cols = slice(s * TF_HALF, (s + 1) * TF_HALF)
        gate = conv(u_bufs[2 * s], cols, cwg_ref, cbg_ref)
        val = conv(u_bufs[2 * s + 1], cols, cwv_ref, cbv_ref)
        return (gate * jax.nn.sigmoid(gate) * val).astype(BF16)

    def down(s, act):
        return jnp.dot(act, wd_ref[s * TF_HALF:(s + 1) * TF_HALF, :], preferred_element_type=F32)

    up(0)
    act0 = gated(0)
    up(1)
    contrib = down(0, act0)
    act1 = gated(1)
    contrib = contrib + down(1, act1)
    o_ref[...] += contrib


def _ffn(h2, x1, w_up, conv_w, conv_b, w_down, seq):
    m = h2.shape[0]
    n_f = D_FF // TF_FFN
    kernel = functools.partial(_ffn_kernel, seq=seq)
    return pl.pallas_call(
        kernel,
        grid=(m // TM_FFN, n_f),
        in_specs=[
            pl.BlockSpec((TM_FFN, D_MODEL), lambda i, f: (i, 0)),
            pl.BlockSpec((TM_FFN, D_MODEL), lambda i, f: (i, 0)),
            pl.BlockSpec((D_MODEL, TF_FFN), lambda i, f: (0, f)),
            pl.BlockSpec((D_MODEL, TF_FFN), lambda i, f: (0, f + n_f)),
            pl.BlockSpec((CONV_WIDTH, TF_FFN), lambda i, f: (0, f)),
            pl.BlockSpec((CONV_WIDTH, TF_FFN), lambda i, f: (0, f + n_f)),
            pl.BlockSpec((1, TF_FFN), lambda i, f: (0, f)),
            pl.BlockSpec((1, TF_FFN), lambda i, f: (0, f + n_f)),
            pl.BlockSpec((TF_FFN, D_MODEL), lambda i, f: (f, 0)),
        ],
        out_specs=pl.BlockSpec((TM_FFN, D_MODEL), lambda i, f: (i, 0)),
        out_shape=jax.ShapeDtypeStruct((m, D_MODEL), F32),
        scratch_shapes=[pltpu.VMEM((n_f, 2 * (TF_FFN // TF_HALF), SUBLANES, TF_HALF), F32)]
        + [pltpu.VMEM((TM_FFN + SUBLANES, TF_HALF), F32)] * (2 * (TF_FFN // TF_HALF)),
        compiler_params=pltpu.CompilerParams(
            dimension_semantics=("arbitrary", "arbitrary"), vmem_limit_bytes=VMEM_LIMIT),
        name="ffn",
    )(h2, x1, w_up, w_up, conv_w, conv_w, conv_b, conv_b, w_down)


def _ple_kernel(x_ref, p_ref, gp_ref, wgate32_ref, wproj_ref, gf_ref, o_ref, wgate_ref, *, last_layer):
    @pl.when(pl.program_id(0) == 0)
    def _():
        for r in range(0, D_MODEL, CAST_ROWS):
            wgate_ref[r:r + CAST_ROWS, :] = wgate32_ref[r:r + CAST_ROWS, :].astype(BF16)

    x = x_ref[...]
    hg = _rms(x, gp_ref[...]).astype(BF16)
    gate = jax.nn.sigmoid(jnp.dot(hg, wgate_ref[...], preferred_element_type=F32))
    emb = jnp.dot(p_ref[...].astype(BF16), wproj_ref[...], preferred_element_type=F32)
    x3 = x + emb * gate
    o_ref[...] = _rms(x3, gf_ref[...]) if last_layer else x3


def _ple(x2, p2d, g_ple, w_gate, w_proj, g_final, last_layer):
    m = x2.shape[0]
    return pl.pallas_call(
        functools.partial(_ple_kernel, last_layer=last_layer),
        grid=(m // TM_PLE,),
        in_specs=[
            pl.BlockSpec((TM_PLE, D_MODEL), lambda i: (i, 0)),
            pl.BlockSpec((TM_PLE, D_PLE), lambda i: (i, 0)),
            pl.BlockSpec((1, D_MODEL), lambda i: (0, 0)),
            pl.BlockSpec((D_MODEL, D_MODEL), lambda i: (0, 0), pipeline_mode=pl.Buffered(1)),
            pl.BlockSpec((D_PLE, D_MODEL), lambda i: (0, 0)),
            pl.BlockSpec((1, D_MODEL), lambda i: (0, 0)),
        ],
        out_specs=pl.BlockSpec((TM_PLE, D_MODEL), lambda i: (i, 0)),
        out_shape=jax.ShapeDtypeStruct((m, D_MODEL), F32),
        scratch_shapes=[pltpu.VMEM((D_MODEL, D_MODEL), BF16)],
        compiler_params=pltpu.CompilerParams(
            dimension_semantics=("arbitrary",), vmem_limit_bytes=VMEM_LIMIT),
        name="ple",
    )(x2, p2d, g_ple, w_gate, w_proj, g_final)


def kernel(x, p, g_mix, w_in, w_pool, pool_scale, rel_bias, w_out, g_ffn, w_up, conv_w, conv_b,
           w_down, g_ple, w_ple_gate, w_ple_proj, g_final):
    bsz, seq, _ = x.shape
    depth = w_in.shape[0]
    assert seq % QB == 0 and seq % TM_MIX == 0 and seq % TM_FFN == 0 and (bsz * seq) % TM_IN == 0
    x2d = x.reshape(bsz * seq, D_MODEL)
    for i in range(depth):
        u_pool, k2d, qT, qiT, vT4, small = _inproj(x2d, g_mix[i].reshape(1, D_MODEL),
                                                   w_in[i].T.astype(BF16), bsz, seq)
        small3 = small.reshape(bsz, seq, SMALL_W)
        kidx = small3[:, :, :IDX_DIM].astype(BF16)
        wT = jnp.swapaxes(small3[:, :, IDX_DIM:IDX_DIM + IDX_HEADS], 1, 2)
        attn, w_out_b, w_up_b, w_down_b = _dsa(rel_bias, qT, qiT, wT, kidx,
                                               k2d.reshape(bsz, seq, ATTN_WIDTH), vT4, seq,
                                               (w_out[i], w_up[i], w_down[i]))

        x1, h2 = _mix(x2d, u_pool, attn.reshape(bsz * seq, ATTN_WIDTH), w_pool[i].astype(BF16),
                      pool_scale[i].reshape(1, POOL_WIDTH), w_out_b,
                      g_ffn[i].reshape(1, D_MODEL), seq)
        x2 = _ffn(h2, x1, w_up_b, conv_w[i], conv_b[i].reshape(1, 2 * D_FF), w_down_b, seq)
        x2d = _ple(x2, p[i].reshape(bsz * seq, D_PLE), g_ple[i].reshape(1, D_MODEL),
                   w_ple_gate[i], w_ple_proj[i].astype(BF16),
                   g_final.reshape(1, D_MODEL), last_layer=(i == depth - 1))
    return x2d.reshape(bsz, seq, D_MODEL)
```

```python
import functools
import math

import numpy as np
import jax
import jax.numpy as jnp
from jax import lax
from jax.experimental import pallas as pl
from jax.experimental.pallas import tpu as pltpu

F32 = jnp.float32
BF16 = jnp.bfloat16
I32 = jnp.int32
I16 = jnp.int16

D_MODEL = 2048
D_PLE = 256
POOL_WIDTH = 1024
POOL_GROUPS = 4
POOL_GC = 256
POOL_WINDOWS = (2, 4, 8, 16)
ATTN_HEADS = 8
HEAD_DIM = 128
ATTN_WIDTH = 1024
IDX_HEADS = 16
IDX_DIM = 64
TOPK_MAX = 256
N_BUCKETS = 32
MAX_DISTANCE = 128
D_FF = 5632
CONV_WIDTH = 3
EPS = 1e-6
G_POOL, G_Q, G_K, G_V, G_QI = range(5)
N_GROUPS = 5
GROUP_W = 1024

V7X_VMEM_BYTES = 64 * 1024 * 1024
SUBLANES = 8
PACKED_SUBLANES = 16
VMEM_LIMIT = V7X_VMEM_BYTES - 8 * 1024 * 1024

TM_IN = 512
SMALL_W = IDX_DIM + IDX_HEADS
QB = 256
KC = 256
N_ACC = 4
L2_BITS = 17
L2_PASS_GROUPS = (4, 4, 4, 5)
TM_MIX = 512
POOL_HALO = 32
TM_FFN = 512
TF_FFN = 512
TF_HALF = 256
TM_PLE = 512
CAST_ROWS = 256

I16_MIN = -2 ** 15
NEG_BIG = -1e30
F32_MAX = float(np.finfo(np.float32).max)
LOG2E = math.log2(math.e)
Q_PRESCALE = (HEAD_DIM ** -0.5) * LOG2E


def _bucket_lower_bounds():
    n = np.arange(0, 4 * MAX_DISTANCE, dtype=np.int64)
    max_exact = N_BUCKETS // 2
    nf = np.maximum(n, 1).astype(np.float32)
    large = max_exact + (np.log(nf / np.float32(max_exact)) / np.float32(math.log(MAX_DISTANCE / max_exact))
                         * np.float32(N_BUCKETS - max_exact)).astype(np.int32)
    large = np.minimum(large, N_BUCKETS - 1)
    bucket = np.where(n < max_exact, n, large)
    assert np.all(np.diff(bucket) >= 0)
    lows = [int(np.argmax(bucket == b)) for b in range(N_BUCKETS)]
    assert all(bucket[lo] == b for b, lo in enumerate(lows))
    return lows


BUCKET_LO = _bucket_lower_bounds()


def _rms(x, g):
    ms = jnp.mean(x * x, axis=-1, keepdims=True)
    return (x * lax.rsqrt(ms + EPS)) * g


def _nt_dot(a, b):
    return lax.dot_general(a, b, (((1,), (1,)), ((), ())), preferred_element_type=F32)


def _inproj_kernel(x_ref, g_ref, wt_ref, pool_ref, k_ref, qT_ref, qiT_ref, vT_ref, small_ref):
    hb = _rms(x_ref[...], g_ref[...]).astype(BF16)

    def w(group):
        return wt_ref[group * GROUP_W:(group + 1) * GROUP_W, :]

    pool_ref[...] = _nt_dot(hb, w(G_POOL))
    k_ref[...] = _nt_dot(hb, w(G_K)).astype(BF16)
    small_ref[...] = _nt_dot(hb, wt_ref[N_GROUPS * GROUP_W:, :])
    qT_ref[...] = (_nt_dot(w(G_Q), hb) * Q_PRESCALE).astype(BF16)
    qiT_ref[...] = _nt_dot(w(G_QI), hb).astype(BF16)
    out_t = _nt_dot(w(G_V), hb).astype(BF16)
    for j in range(TM_IN // KC):
        vT_ref[j] = out_t[:, j * KC:(j + 1) * KC]


def _inproj(x2d, g, wt, bsz, seq):
    m = x2d.shape[0]
    tiles_per_seq = seq // TM_IN
    assert wt.shape[0] == N_GROUPS * GROUP_W + SMALL_W
    return pl.pallas_call(
        _inproj_kernel,
        grid=(m // TM_IN,),
        in_specs=[
            pl.BlockSpec((TM_IN, D_MODEL), lambda i: (i, 0)),
            pl.BlockSpec((1, D_MODEL), lambda i: (0, 0)),
            pl.BlockSpec(wt.shape, lambda i: (0, 0), pipeline_mode=pl.Buffered(1)),
        ],
        out_specs=[
            pl.BlockSpec((TM_IN, GROUP_W), lambda i: (i, 0)),
            pl.BlockSpec((TM_IN, GROUP_W), lambda i: (i, 0)),
            pl.BlockSpec((None, GROUP_W, TM_IN), lambda i: (i // tiles_per_seq, 0, i % tiles_per_seq)),
            pl.BlockSpec((None, GROUP_W, TM_IN), lambda i: (i // tiles_per_seq, 0, i % tiles_per_seq)),
            pl.BlockSpec((None, TM_IN // KC, GROUP_W, KC),
                         lambda i: (i // tiles_per_seq, i % tiles_per_seq, 0, 0)),
            pl.BlockSpec((TM_IN, SMALL_W), lambda i: (i, 0)),
        ],
        out_shape=[
            jax.ShapeDtypeStruct((m, POOL_WIDTH), F32),
            jax.ShapeDtypeStruct((m, ATTN_WIDTH), BF16),
            jax.ShapeDtypeStruct((bsz, ATTN_WIDTH, seq), BF16),
            jax.ShapeDtypeStruct((bsz, IDX_HEADS * IDX_DIM, seq), BF16),
            jax.ShapeDtypeStruct((bsz, seq // KC, ATTN_WIDTH, KC), BF16),
            jax.ShapeDtypeStruct((m, SMALL_W), F32),
        ],
        compiler_params=pltpu.CompilerParams(
            dimension_semantics=("arbitrary",), vmem_limit_bytes=VMEM_LIMIT),
        name="inproj",
    )(x2d, g, wt)


def _dsa_kernel(rel_ref, qT_ref, qiT_ref, wT_ref, qiTn_ref, wTn_ref, kidx_ref, k_ref, vT_ref, *rest,
                topk, n_q, n_side):
    side_in, rest = rest[:n_side], rest[n_side:]
    o_ref, side_out = rest[0], rest[1:1 + n_side]
    sc_s, sb_s, madd_s, bias_s, mx_s, alpha_s, acc_s, s_s = rest[1 + n_side:]
    b = pl.program_id(0)
    i = pl.program_id(1)
    n_ch = i + 1

    for src, dst in zip(side_in, side_out):
        dst[...] = src[...].astype(dst.dtype)

    @pl.when((b == 0) & (i == 0))
    def _():
        r = lax.broadcasted_iota(I32, (KC, QB), 0)
        c = lax.broadcasted_iota(I32, (KC, QB), 1)
        for off in range(2):
            dist = jnp.maximum(c - r + off * QB, 0)

            def head_body(h, carry, dist=dist, off=off):
                t = jnp.full((KC, QB), rel_ref[0, h], F32)
                for bk in range(1, N_BUCKETS):
                    t = jnp.where(dist >= BUCKET_LO[bk], rel_ref[bk, h], t)
                bias_s[off, h] = (t - rel_ref[N_BUCKETS - 1, h]) * LOG2E
                return carry

            lax.fori_loop(0, ATTN_HEADS, head_body, 0)

    idx_scale = (IDX_HEADS ** -0.5) * (IDX_DIM ** -0.5)
    row = lax.broadcasted_iota(I32, (KC, QB), 0)
    col = lax.broadcasted_iota(I32, (KC, QB), 1)

    def index_chunk(m, qi_ref, w_scaled, q_block):
        kc = kidx_ref[pl.ds(pl.multiple_of(m * KC, KC), KC), :]
        acc = jnp.zeros((KC, QB), F32)
        for h in range(IDX_HEADS):
            d = jnp.dot(kc, qi_ref[h * IDX_DIM:(h + 1) * IDX_DIM, :],
                        preferred_element_type=F32)
            acc = acc + w_scaled[h:h + 1, :] * jnp.maximum(d, 0.0)
        causal = (m * KC + row) <= (q_block * QB + col)
        score = jnp.where(causal, acc, -jnp.inf)
        sc_s[m] = score
        sb_s[m] = score.astype(BF16)

    @pl.when(i == 0)
    def _():
        index_chunk(0, qiT_ref, wT_ref[...] * idx_scale, 0)

    def code_to_score(code):
        bits = jnp.where(code >= 0, code, code ^ 0x7FFFFFFF)
        return pltpu.bitcast(bits, F32)

    def code16_to_bf16(code):
        bits = jnp.where(code >= 0, code, code ^ 0x7FFF)
        return pltpu.bitcast(bits.astype(I16), BF16)

    def spread(accs, ones, rows):
        accs = list(accs)
        for j in range(KC // rows):
            accs[j % N_ACC] = accs[j % N_ACC] + ones[j * rows:(j + 1) * rows, :]
        return tuple(accs)

    def total(accs, dtype):
        acc = functools.reduce(lambda a, b: a + b, accs)
        return jnp.sum(acc.astype(dtype), axis=0, keepdims=True)

    zero16 = (jnp.zeros((PACKED_SUBLANES, QB), I16),) * N_ACC
    zero8 = (jnp.zeros((SUBLANES, QB), F32),) * N_ACC

    def count16_ge(cand):
        def body(m, accs):
            return spread(accs, (sb_s[m] >= cand).astype(I16), PACKED_SUBLANES)

        return total(lax.fori_loop(0, n_ch, body, zero16), I32)

    def bit16_body(it, prefix):
        cand = prefix + jnp.left_shift(jnp.int32(1), 15 - it)
        return jnp.where(count16_ge(code16_to_bf16(cand)) >= topk, cand, prefix)

    p16 = lax.fori_loop(0, 16, bit16_body, jnp.full((1, QB), I16_MIN, I32))

    def fold(accs, flags):
        return spread(accs, jnp.where(flags, 1.0, 0.0), SUBLANES)

    def count_ge(cand):
        return total(lax.fori_loop(0, n_ch, lambda m, accs: fold(accs, sc_s[m] >= cand), zero8), F32)

    below = p16 - 1
    lo = jnp.left_shift(below, 16) + jnp.where(below < 0, 0xFFFF, 0)

    def bit_body(it, state):
        prefix, kept = state
        cand = prefix + jnp.left_shift(jnp.int32(1), L2_BITS - 1 - it)
        cnt = count_ge(code_to_score(cand))
        ok = cnt >= topk
        return jnp.where(ok, cand, prefix), jnp.where(ok, cnt, kept)

    settled_at = jnp.where(p16 == I16_MIN, float(topk), float(topk + 1))
    state = (lo, settled_at)
    done_bits = 0
    for n_bits in L2_PASS_GROUPS:
        first_bit = done_bits

        def run(st, first_bit=first_bit, n_bits=n_bits):
            return lax.fori_loop(first_bit, first_bit + n_bits, bit_body, st)

        if done_bits == 0:
            state = run(state)
        else:
            all_settled = jnp.max(jnp.abs(state[1] - topk)) == 0.0
            state = lax.cond(all_settled, lambda st: st, run, state)
        done_bits += n_bits
    assert done_bits == L2_BITS
    prefix = state[0]
    thr = jnp.where(p16 == I16_MIN, -F32_MAX, code_to_score(prefix))

    def mask_body(m, kept):
        keep = sc_s[m] >= thr
        madd_s[m] = jnp.where(keep, 0.0, NEG_BIG)
        return fold(kept, keep)

    excess = total(lax.fori_loop(0, n_ch, mask_body, zero8), F32) - topk

    @pl.when(jnp.max(excess) > 0)
    def _():
        n_tied = total(lax.fori_loop(0, n_ch, lambda m, accs: fold(accs, sc_s[m] == thr), zero8), F32)
        need = n_tied - excess
        r = lax.broadcasted_iota(I32, (KC, KC), 0)
        c = lax.broadcasted_iota(I32, (KC, KC), 1)
        prefix_op = jnp.where(c <= r, 1.0, 0.0).astype(BF16)

        def fix_body(m, seen):
            tied = sc_s[m] == thr
            t = jnp.where(tied, 1.0, 0.0).astype(BF16)
            rank = seen + jnp.dot(prefix_op, t, preferred_element_type=F32)
            madd_s[m] = jnp.where(tied & (rank > need), NEG_BIG, madd_s[m])
            return rank[KC - 1:KC, :]

        lax.fori_loop(0, n_ch, fix_body, jnp.zeros((1, QB), F32))

    n_far = jnp.maximum(i - 1, 0)
    mx_s[...] = jnp.full(mx_s.shape, -jnp.inf, F32)
    acc_s[...] = jnp.zeros(acc_s.shape, F32)

    w_next = wTn_ref[...] * idx_scale

    def att_body(m, carry, near):
        index_chunk(m, qiTn_ref, w_next, i + 1)
        madd = madd_s[m]
        rows = pl.ds(pl.multiple_of(m * KC, KC), KC)
        for h in range(ATTN_HEADS):
            hd = slice(h * HEAD_DIM, (h + 1) * HEAD_DIM)
            s = jnp.dot(k_ref[rows, hd], qT_ref[hd, :], preferred_element_type=F32)
            if near:
                s = s + (madd + bias_s[i - m, h])
            else:
                s = s + madd
            s_s[h] = s
            mx = mx_s[h]
            m_new = jnp.maximum(mx, jnp.max(s, axis=0, keepdims=True))
            alpha_s[h] = jnp.exp2(mx - m_new)
            mx_s[h] = m_new
        ones_rows = jnp.ones((PACKED_SUBLANES, KC), BF16)
        for h in range(ATTN_HEADS):
            hd = slice(h * HEAD_DIM, (h + 1) * HEAD_DIM)
            p = jnp.exp2(s_s[h] - mx_s[h])
            v_aug = jnp.concatenate([vT_ref[m, hd, :], ones_rows], axis=0)
            pv = jnp.dot(v_aug, p.astype(BF16), preferred_element_type=F32)
            acc_s[h] = alpha_s[h] * acc_s[h] + pv
        return carry

    def far_pair(pair, carry):
        att_body(2 * pair, carry, near=False)
        att_body(2 * pair + 1, carry, near=False)
        return carry

    lax.fori_loop(0, n_far // 2, far_pair, 0)

    @pl.when(n_far % 2 == 1)
    def _():
        att_body(n_far - 1, 0, near=False)

    @pl.when(i >= 1)
    def _():
        att_body(i - 1, 0, near=True)

    att_body(i, 0, near=True)
    index_chunk(jnp.minimum(i + 1, n_q - 1), qiTn_ref, w_next, i + 1)

    for h in range(ATTN_HEADS):
        acc = acc_s[h]
        out_t = acc[:HEAD_DIM] / acc[HEAD_DIM:HEAD_DIM + 1]
        o_ref[:, h * HEAD_DIM:(h + 1) * HEAD_DIM] = out_t.T.astype(o_ref.dtype)


def _dsa(rel_bias, qT, qiT, wT, kidx, k, vT4, seq, side_weights):
    bsz = qT.shape[0]
    topk = min(TOPK_MAX, seq // 4)
    n_q = seq // QB
    n_steps = bsz * n_q
    kernel = functools.partial(_dsa_kernel, topk=topk, n_q=n_q, n_side=len(side_weights))
    nxt = lambda i: jnp.minimum(i + 1, n_q - 1)
    side_specs = []
    for w in side_weights:
        rows = w.shape[0] // n_steps
        assert rows * n_steps == w.shape[0] and rows % PACKED_SUBLANES == 0
        side_specs.append(pl.BlockSpec((rows, w.shape[1]), lambda b, i: (b * n_q + i, 0)))
    resident = dict(pipeline_mode=pl.Buffered(1))
    return pl.pallas_call(
        kernel,
        grid=(bsz, n_q),
        in_specs=[
            pl.BlockSpec(memory_space=pltpu.SMEM),
            pl.BlockSpec((None, ATTN_WIDTH, QB), lambda b, i: (b, 0, i)),
            pl.BlockSpec((None, IDX_HEADS * IDX_DIM, QB), lambda b, i: (b, 0, i)),
            pl.BlockSpec((None, IDX_HEADS, QB), lambda b, i: (b, 0, i)),
            pl.BlockSpec((None, IDX_HEADS * IDX_DIM, QB), lambda b, i: (b, 0, nxt(i))),
            pl.BlockSpec((None, IDX_HEADS, QB), lambda b, i: (b, 0, nxt(i))),
            pl.BlockSpec((None, seq, IDX_DIM), lambda b, i: (b, 0, 0)),
            pl.BlockSpec((None, seq, ATTN_WIDTH), lambda b, i: (b, 0, 0), **resident),
            pl.BlockSpec((None, seq // KC, ATTN_WIDTH, KC), lambda b, i: (b, 0, 0, 0), **resident),
        ] + side_specs,
        out_specs=[pl.BlockSpec((None, QB, ATTN_WIDTH), lambda b, i: (b, i, 0))] + side_specs,
        out_shape=[jax.ShapeDtypeStruct((bsz, seq, ATTN_WIDTH), BF16)]
        + [jax.ShapeDtypeStruct(w.shape, BF16) for w in side_weights],
        scratch_shapes=[
            pltpu.VMEM((seq // KC, KC, QB), F32),
            pltpu.VMEM((seq // KC, KC, QB), BF16),
            pltpu.VMEM((seq // KC, KC, QB), F32),
            pltpu.VMEM((2, ATTN_HEADS, KC, QB), F32),
            pltpu.VMEM((ATTN_HEADS, 1, QB), F32),
            pltpu.VMEM((ATTN_HEADS, 1, QB), F32),
            pltpu.VMEM((ATTN_HEADS, HEAD_DIM + PACKED_SUBLANES, QB), F32),
            pltpu.VMEM((ATTN_HEADS, KC, QB), F32),
        ],
        compiler_params=pltpu.CompilerParams(
            dimension_semantics=("arbitrary", "arbitrary"), vmem_limit_bytes=VMEM_LIMIT),
        name="dsa",
    )(rel_bias, qT, qiT, wT, qiT, wT, kidx, k, vT4, *side_weights)


def _mix_kernel(x_ref, u_ref, halo_ref, attn_ref, wpool_ref, pscale_ref, wout_ref, g_ref,
                x1_ref, h2_ref, ext_s, lvl0_s, lvl1_s, *, seq):
    m = pl.program_id(0)
    tiles_per_seq = seq // TM_MIX
    first = (m % tiles_per_seq) == 0
    ext_s[0:POOL_HALO, :] = jnp.where(first, 0.0, halo_ref[...])
    ext_s[POOL_HALO:, :] = u_ref[...]
    n_ext = TM_MIX + POOL_HALO
    lvl_bufs = (lvl0_s, lvl1_s)

    t_pos = (m % tiles_per_seq) * TM_MIX + lax.broadcasted_iota(I32, (TM_MIX, POOL_GC), 0)
    acc = x_ref[...] + jnp.dot(attn_ref[...], wout_ref[POOL_WIDTH:, :], preferred_element_type=F32)
    for g, w in enumerate(POOL_WINDOWS):
        cols = slice(g * POOL_GC, (g + 1) * POOL_GC)
        u = ext_s[POOL_HALO:, cols]
        levels = w.bit_length() - 1
        assert w == 1 << levels and SUBLANES * levels <= POOL_HALO
        src, src_cols = ext_s, cols
        for lv in range(levels):
            start = SUBLANES * (lv + 1)
            n_rows = n_ext - start
            t = src[pl.ds(start, n_rows), src_cols] + src[pl.ds(start - (1 << lv), n_rows), src_cols]
            if lv == levels - 1:
                wsum = t[POOL_HALO - start:, :]
            else:
                dst = lvl_bufs[lv % 2]
                dst[pl.ds(start, n_rows), :] = t
                src, src_cols = dst, slice(None)
        count = jnp.minimum(t_pos + 1, w).astype(F32)
        pooled = (wsum / count - u).astype(BF16)
        y = jnp.dot(pooled, wpool_ref[g], preferred_element_type=F32) * pscale_ref[:, cols]
        acc = acc + jnp.dot(y.astype(BF16), wout_ref[cols, :], preferred_element_type=F32)
    x1_ref[...] = acc
    h2_ref[...] = _rms(acc, g_ref[...]).astype(BF16)


def _mix(x2d, u_pool, attn2d, w_pool, pool_scale, w_out, g_ffn, seq):
    m = x2d.shape[0]
    halo_per_tile = TM_MIX // POOL_HALO
    kernel = functools.partial(_mix_kernel, seq=seq)
    return pl.pallas_call(
        kernel,
        grid=(m // TM_MIX,),
        in_specs=[
            pl.BlockSpec((TM_MIX, D_MODEL), lambda i: (i, 0)),
            pl.BlockSpec((TM_MIX, POOL_WIDTH), lambda i: (i, 0)),
            pl.BlockSpec((POOL_HALO, POOL_WIDTH), lambda i: (jnp.maximum(i * halo_per_tile - 1, 0), 0)),
            pl.BlockSpec((TM_MIX, ATTN_WIDTH), lambda i: (i, 0)),
            pl.BlockSpec((POOL_GROUPS, POOL_GC, POOL_GC), lambda i: (0, 0, 0)),
            pl.BlockSpec((1, POOL_WIDTH), lambda i: (0, 0)),
            pl.BlockSpec((D_MODEL, D_MODEL), lambda i: (0, 0)),
            pl.BlockSpec((1, D_MODEL), lambda i: (0, 0)),
        ],
        out_specs=[
            pl.BlockSpec((TM_MIX, D_MODEL), lambda i: (i, 0)),
            pl.BlockSpec((TM_MIX, D_MODEL), lambda i: (i, 0)),
        ],
        out_shape=[
            jax.ShapeDtypeStruct((m, D_MODEL), F32),
            jax.ShapeDtypeStruct((m, D_MODEL), BF16),
        ],
        scratch_shapes=[pltpu.VMEM((TM_MIX + POOL_HALO, POOL_WIDTH), F32),
                        pltpu.VMEM((TM_MIX + POOL_HALO, POOL_GC), F32),
                        pltpu.VMEM((TM_MIX + POOL_HALO, POOL_GC), F32)],
        compiler_params=pltpu.CompilerParams(
            dimension_semantics=("arbitrary",), vmem_limit_bytes=VMEM_LIMIT),
        name="mix",
    )(x2d, u_pool, u_pool, attn2d, w_pool, pool_scale, w_out, g_ffn)


def _ffn_kernel(h_ref, x1_ref, wg_ref, wv_ref, cwg_ref, cwv_ref, cbg_ref, cbv_ref, wd_ref,
                o_ref, carry_s, *u_bufs, seq):
    m = pl.program_id(0)
    f = pl.program_id(1)
    tiles_per_seq = seq // TM_FFN
    first = (m % tiles_per_seq) == 0

    @pl.when(f == 0)
    def _():
        o_ref[...] = x1_ref[...]

    h = h_ref[...]
    n_half = TF_FFN // TF_HALF

    def up(s):
        cols = slice(s * TF_HALF, (s + 1) * TF_HALF)
        for k, w_ref in enumerate((wg_ref, wv_ref)):
            u = u_bufs[2 * s + k]
            u[0:SUBLANES, :] = jnp.where(first, 0.0, carry_s[f, 2 * s + k])
            u[SUBLANES:, :] = jnp.dot(h, w_ref[:, cols], preferred_element_type=F32)
            carry_s[f, 2 * s + k] = u[TM_FFN:, :]

    def conv(u, cols, cw_ref, cb_ref):
        out = cb_ref[:, cols]
        for j in range(CONV_WIDTH):
            shift = CONV_WIDTH - 1 - j
            out = out + u[pl.ds(SUBLANES - shift, TM_FFN), :] * cw_ref[j:j + 1, cols]
        return out

    def gated(s):
        cols = slice(s * TF_HALF, (s + 1) * TF_HALF)
        gate = conv(u_bufs[2 * s], cols, cwg_ref, cbg_ref)
        val = conv(u_bufs[2 * s + 1], cols, cwv_ref, cbv_ref)
        return (gate * jax.nn.sigmoid(gate) * val).astype(BF16)

    def down(s, act):
        return jnp.dot(act, wd_ref[s * TF_HALF:(s + 1) * TF_HALF, :], preferred_element_type=F32)

    up(0)
    act0 = gated(0)
    up(1)
    contrib = down(0, act0)
    act1 = gated(1)
    contrib = contrib + down(1, act1)
    o_ref[...] += contrib


def _ffn(h2, x1, w_up, conv_w, conv_b, w_down, seq):
    m = h2.shape[0]
    n_f = D_FF // TF_FFN
    kernel = functools.partial(_ffn_kernel, seq=seq)
    return pl.pallas_call(
        kernel,
        grid=(m // TM_FFN, n_f),
        in_specs=[
            pl.BlockSpec((TM_FFN, D_MODEL), lambda i, f: (i, 0)),
            pl.BlockSpec((TM_FFN, D_MODEL), lambda i, f: (i, 0)),
            pl.BlockSpec((D_MODEL, TF_FFN), lambda i, f: (0, f)),
            pl.BlockSpec((D_MODEL, TF_FFN), lambda i, f: (0, f + n_f)),
            pl.BlockSpec((CONV_WIDTH, TF_FFN), lambda i, f: (0, f)),
            pl.BlockSpec((CONV_WIDTH, TF_FFN), lambda i, f: (0, f + n_f)),
            pl.BlockSpec((1, TF_FFN), lambda i, f: (0, f)),
            pl.BlockSpec((1, TF_FFN), lambda i, f: (0, f + n_f)),
            pl.BlockSpec((TF_FFN, D_MODEL), lambda i, f: (f, 0)),
        ],
        out_specs=pl.BlockSpec((TM_FFN, D_MODEL), lambda i, f: (i, 0)),
        out_shape=jax.ShapeDtypeStruct((m, D_MODEL), F32),
        scratch_shapes=[pltpu.VMEM((n_f, 2 * (TF_FFN // TF_HALF), SUBLANES, TF_HALF), F32)]
        + [pltpu.VMEM((TM_FFN + SUBLANES, TF_HALF), F32)] * (2 * (TF_FFN // TF_HALF)),
        compiler_params=pltpu.CompilerParams(
            dimension_semantics=("arbitrary", "arbitrary"), vmem_limit_bytes=VMEM_LIMIT),
        name="ffn",
    )(h2, x1, w_up, w_up, conv_w, conv_w, conv_b, conv_b, w_down)


def _ple_kernel(x_ref, p_ref, gp_ref, wgate32_ref, wproj_ref, gf_ref, o_ref, wgate_ref, *, last_layer):
    @pl.when(pl.program_id(0) == 0)
    def _():
        for r in range(0, D_MODEL, CAST_ROWS):
            wgate_ref[r:r + CAST_ROWS, :] = wgate32_ref[r:r + CAST_ROWS, :].astype(BF16)

    x = x_ref[...]
    hg = _rms(x, gp_ref[...]).astype(BF16)
    gate = jax.nn.sigmoid(jnp.dot(hg, wgate_ref[...], preferred_element_type=F32))
    emb = jnp.dot(p_ref[...].astype(BF16), wproj_ref[...], preferred_element_type=F32)
    x3 = x + emb * gate
    o_ref[...] = _rms(x3, gf_ref[...]) if last_layer else x3


def _ple(x2, p2d, g_ple, w_gate, w_proj, g_final, last_layer):
    m = x2.shape[0]
    return pl.pallas_call(
        functools.partial(_ple_kernel, last_layer=last_layer),
        grid=(m // TM_PLE,),
        in_specs=[
            pl.BlockSpec((TM_PLE, D_MODEL), lambda i: (i, 0)),
            pl.BlockSpec((TM_PLE, D_PLE), lambda i: (i, 0)),
            pl.BlockSpec((1, D_MODEL), lambda i: (0, 0)),
            pl.BlockSpec((D_MODEL, D_MODEL), lambda i: (0, 0), pipeline_mode=pl.Buffered(1)),
            pl.BlockSpec((D_PLE, D_MODEL), lambda i: (0, 0)),
            pl.BlockSpec((1, D_MODEL), lambda i: (0, 0)),
        ],
        out_specs=pl.BlockSpec((TM_PLE, D_MODEL), lambda i: (i, 0)),
        out_shape=jax.ShapeDtypeStruct((m, D_MODEL), F32),
        scratch_shapes=[pltpu.VMEM((D_MODEL, D_MODEL), BF16)],
        compiler_params=pltpu.CompilerParams(
            dimension_semantics=("arbitrary",), vmem_limit_bytes=VMEM_LIMIT),
        name="ple",
    )(x2, p2d, g_ple, w_gate, w_proj, g_final)


def kernel(x, p, g_mix, w_in, w_pool, pool_scale, rel_bias, w_out, g_ffn, w_up, conv_w, conv_b,
           w_down, g_ple, w_ple_gate, w_ple_proj, g_final):
    bsz, seq, _ = x.shape
    depth = w_in.shape[0]
    assert seq % QB == 0 and seq % TM_MIX == 0 and seq % TM_FFN == 0 and (bsz * seq) % TM_IN == 0
    x2d = x.reshape(bsz * seq, D_MODEL)
    for i in range(depth):
        u_pool, k2d, qT, qiT, vT4, small = _inproj(x2d, g_mix[i].reshape(1, D_MODEL),
                                                   w_in[i].T.astype(BF16), bsz, seq)
        small3 = small.reshape(bsz, seq, SMALL_W)
        kidx = small3[:, :, :IDX_DIM].astype(BF16)
        wT = jnp.swapaxes(small3[:, :, IDX_DIM:IDX_DIM + IDX_HEADS], 1, 2)
        attn, w_out_b, w_up_b, w_down_b = _dsa(rel_bias, qT, qiT, wT, kidx,
                                               k2d.reshape(bsz, seq, ATTN_WIDTH), vT4, seq,
                                               (w_out[i], w_up[i], w_down[i]))

        x1, h2 = _mix(x2d, u_pool, attn.reshape(bsz * seq, ATTN_WIDTH), w_pool[i].astype(BF16),
                      pool_scale[i].reshape(1, POOL_WIDTH), w_out_b,
                      g_ffn[i].reshape(1, D_MODEL), seq)
        x2 = _ffn(h2, x1, w_up_b, conv_w[i], conv_b[i].reshape(1, 2 * D_FF), w_down_b, seq)
        x2d = _ple(x2, p[i].reshape(bsz * seq, D_PLE), g_ple[i].reshape(1, D_MODEL),
                   w_ple_gate[i], w_ple_proj[i].astype(BF16),
                   g_final.reshape(1, D_MODEL), last_layer=(i == depth - 1))
    return x2d.reshape(bsz, seq, D_MODEL)
```

```python
import functools
import math

import numpy as np
import jax
import jax.numpy as jnp
from jax import lax
from jax.experimental import pallas as pl
from jax.experimental.pallas import tpu as pltpu

F32 = jnp.float32
BF16 = jnp.bfloat16
I32 = jnp.int32
I16 = jnp.int16

D_MODEL = 2048
D_PLE = 256
POOL_WIDTH = 1024
POOL_GROUPS = 4
POOL_GC = 256
POOL_WINDOWS = (2, 4, 8, 16)
ATTN_HEADS = 8
HEAD_DIM = 128
ATTN_WIDTH = 1024
IDX_HEADS = 16
IDX_DIM = 64
TOPK_MAX = 256
N_BUCKETS = 32
MAX_DISTANCE = 128
D_FF = 5632
CONV_WIDTH = 3
EPS = 1e-6
G_POOL, G_Q, G_K, G_V, G_QI = range(5)
N_GROUPS = 5
GROUP_W = 1024

V7X_VMEM_BYTES = 64 * 1024 * 1024
SUBLANES = 8
PACKED_SUBLANES = 16
VMEM_LIMIT = V7X_VMEM_BYTES - 8 * 1024 * 1024

TM_IN = 512
SMALL_W = IDX_DIM + IDX_HEADS
QB = 256
KC = 256
N_ACC = 4
L2_BITS = 17
L2_PASS_GROUPS = (4, 4, 4, 5)
TM_MIX = 512
POOL_HALO = 32
TM_FFN = 1024
TF_FFN = 512
TF_HALF = 256
TM_PLE = 512

I16_MIN = -2 ** 15
NEG_BIG = -1e30
F32_MAX = float(np.finfo(np.float32).max)
LOG2E = math.log2(math.e)
Q_PRESCALE = (HEAD_DIM ** -0.5) * LOG2E


def _bucket_lower_bounds():
    n = np.arange(0, 4 * MAX_DISTANCE, dtype=np.int64)
    max_exact = N_BUCKETS // 2
    nf = np.maximum(n, 1).astype(np.float32)
    large = max_exact + (np.log(nf / np.float32(max_exact)) / np.float32(math.log(MAX_DISTANCE / max_exact))
                         * np.float32(N_BUCKETS - max_exact)).astype(np.int32)
    large = np.minimum(large, N_BUCKETS - 1)
    bucket = np.where(n < max_exact, n, large)
    assert np.all(np.diff(bucket) >= 0)
    lows = [int(np.argmax(bucket == b)) for b in range(N_BUCKETS)]
    assert all(bucket[lo] == b for b, lo in enumerate(lows))
    return lows


BUCKET_LO = _bucket_lower_bounds()


def _rms(x, g):
    ms = jnp.mean(x * x, axis=-1, keepdims=True)
    return (x * lax.rsqrt(ms + EPS)) * g


def _nt_dot(a, b):
    return lax.dot_general(a, b, (((1,), (1,)), ((), ())), preferred_element_type=F32)


def _inproj_kernel(x_ref, g_ref, wt_ref, pool_ref, k_ref, qT_ref, qiT_ref, vT_ref, small_ref):
    hb = _rms(x_ref[...], g_ref[...]).astype(BF16)

    def w(group):
        return wt_ref[group * GROUP_W:(group + 1) * GROUP_W, :]

    pool_ref[...] = _nt_dot(hb, w(G_POOL))
    k_ref[...] = _nt_dot(hb, w(G_K)).astype(BF16)
    small_ref[...] = _nt_dot(hb, wt_ref[N_GROUPS * GROUP_W:, :])
    qT_ref[...] = (_nt_dot(w(G_Q), hb) * Q_PRESCALE).astype(BF16)
    qiT_ref[...] = _nt_dot(w(G_QI), hb).astype(BF16)
    out_t = _nt_dot(w(G_V), hb).astype(BF16)
    for j in range(TM_IN // KC):
        vT_ref[j] = out_t[:, j * KC:(j + 1) * KC]


def _inproj(x2d, g, wt, bsz, seq):
    m = x2d.shape[0]
    tiles_per_seq = seq // TM_IN
    assert wt.shape[0] == N_GROUPS * GROUP_W + SMALL_W
    return pl.pallas_call(
        _inproj_kernel,
        grid=(m // TM_IN,),
        in_specs=[
            pl.BlockSpec((TM_IN, D_MODEL), lambda i: (i, 0)),
            pl.BlockSpec((1, D_MODEL), lambda i: (0, 0)),
            pl.BlockSpec(wt.shape, lambda i: (0, 0), pipeline_mode=pl.Buffered(1)),
        ],
        out_specs=[
            pl.BlockSpec((TM_IN, GROUP_W), lambda i: (i, 0)),
            pl.BlockSpec((TM_IN, GROUP_W), lambda i: (i, 0)),
            pl.BlockSpec((None, GROUP_W, TM_IN), lambda i: (i // tiles_per_seq, 0, i % tiles_per_seq)),
            pl.BlockSpec((None, GROUP_W, TM_IN), lambda i: (i // tiles_per_seq, 0, i % tiles_per_seq)),
            pl.BlockSpec((None, TM_IN // KC, GROUP_W, KC),
                         lambda i: (i // tiles_per_seq, i % tiles_per_seq, 0, 0)),
            pl.BlockSpec((TM_IN, SMALL_W), lambda i: (i, 0)),
        ],
        out_shape=[
            jax.ShapeDtypeStruct((m, POOL_WIDTH), F32),
            jax.ShapeDtypeStruct((m, ATTN_WIDTH), BF16),
            jax.ShapeDtypeStruct((bsz, ATTN_WIDTH, seq), BF16),
            jax.ShapeDtypeStruct((bsz, IDX_HEADS * IDX_DIM, seq), BF16),
            jax.ShapeDtypeStruct((bsz, seq // KC, ATTN_WIDTH, KC), BF16),
            jax.ShapeDtypeStruct((m, SMALL_W), F32),
        ],
        compiler_params=pltpu.CompilerParams(
            dimension_semantics=("arbitrary",), vmem_limit_bytes=VMEM_LIMIT),
        name="inproj",
    )(x2d, g, wt)


def _dsa_kernel(rel_ref, qT_ref, qiT_ref, wT_ref, qiTn_ref, wTn_ref, kidx_ref, k_ref, vT_ref, *rest,
                topk, n_q, n_side):
    side_in, rest = rest[:n_side], rest[n_side:]
    o_ref, side_out = rest[0], rest[1:1 + n_side]
    sc_s, sb_s, madd_s, bias_s, mx_s, alpha_s, acc_s, s_s = rest[1 + n_side:]
    b = pl.program_id(0)
    i = pl.program_id(1)
    n_ch = i + 1

    for src, dst in zip(side_in, side_out):
        dst[...] = src[...].astype(dst.dtype)

    @pl.when((b == 0) & (i == 0))
    def _():
        r = lax.broadcasted_iota(I32, (KC, QB), 0)
        c = lax.broadcasted_iota(I32, (KC, QB), 1)
        for off in range(2):
            dist = jnp.maximum(c - r + off * QB, 0)

            def head_body(h, carry, dist=dist, off=off):
                t = jnp.full((KC, QB), rel_ref[0, h], F32)
                for bk in range(1, N_BUCKETS):
                    t = jnp.where(dist >= BUCKET_LO[bk], rel_ref[bk, h], t)
                bias_s[off, h] = (t - rel_ref[N_BUCKETS - 1, h]) * LOG2E
                return carry

            lax.fori_loop(0, ATTN_HEADS, head_body, 0)

    idx_scale = (IDX_HEADS ** -0.5) * (IDX_DIM ** -0.5)
    row = lax.broadcasted_iota(I32, (KC, QB), 0)
    col = lax.broadcasted_iota(I32, (KC, QB), 1)

    def index_chunk(m, qi_ref, w_scaled, q_block):
        kc = kidx_ref[pl.ds(pl.multiple_of(m * KC, KC), KC), :]
        acc = jnp.zeros((KC, QB), F32)
        for h in range(IDX_HEADS):
            d = jnp.dot(kc, qi_ref[h * IDX_DIM:(h + 1) * IDX_DIM, :],
                        preferred_element_type=F32)
            acc = acc + w_scaled[h:h + 1, :] * jnp.maximum(d, 0.0)
        causal = (m * KC + row) <= (q_block * QB + col)
        score = jnp.where(causal, acc, -jnp.inf)
        sc_s[m] = score
        sb_s[m] = score.astype(BF16)

    @pl.when(i == 0)
    def _():
        index_chunk(0, qiT_ref, wT_ref[...] * idx_scale, 0)

    def code_to_score(code):
        bits = jnp.where(code >= 0, code, code ^ 0x7FFFFFFF)
        return pltpu.bitcast(bits, F32)

    def code16_to_bf16(code):
        bits = jnp.where(code >= 0, code, code ^ 0x7FFF)
        return pltpu.bitcast(bits.astype(I16), BF16)

    def spread(accs, ones, rows):
        accs = list(accs)
        for j in range(KC // rows):
            accs[j % N_ACC] = accs[j % N_ACC] + ones[j * rows:(j + 1) * rows, :]
        return tuple(accs)

    def total(accs, dtype):
        acc = functools.reduce(lambda a, b: a + b, accs)
        return jnp.sum(acc.astype(dtype), axis=0, keepdims=True)

    zero16 = (jnp.zeros((PACKED_SUBLANES, QB), I16),) * N_ACC
    zero8 = (jnp.zeros((SUBLANES, QB), F32),) * N_ACC

    def count16_ge(cand):
        def body(m, accs):
            return spread(accs, (sb_s[m] >= cand).astype(I16), PACKED_SUBLANES)

        return total(lax.fori_loop(0, n_ch, body, zero16), I32)

    def bit16_body(it, prefix):
        cand = prefix + jnp.left_shift(jnp.int32(1), 15 - it)
        return jnp.where(count16_ge(code16_to_bf16(cand)) >= topk, cand, prefix)

    p16 = lax.fori_loop(0, 16, bit16_body, jnp.full((1, QB), I16_MIN, I32))

    def fold(accs, flags):
        return spread(accs, jnp.where(flags, 1.0, 0.0), SUBLANES)

    def count_ge(cand):
        return total(lax.fori_loop(0, n_ch, lambda m, accs: fold(accs, sc_s[m] >= cand), zero8), F32)

    below = p16 - 1
    lo = jnp.left_shift(below, 16) + jnp.where(below < 0, 0xFFFF, 0)

    def bit_body(it, state):
        prefix, kept = state
        cand = prefix + jnp.left_shift(jnp.int32(1), L2_BITS - 1 - it)
        cnt = count_ge(code_to_score(cand))
        ok = cnt >= topk
        return jnp.where(ok, cand, prefix), jnp.where(ok, cnt, kept)

    settled_at = jnp.where(p16 == I16_MIN, float(topk), float(topk + 1))
    state = (lo, settled_at)
    done_bits = 0
    for n_bits in L2_PASS_GROUPS:
        first_bit = done_bits

        def run(st, first_bit=first_bit, n_bits=n_bits):
            return lax.fori_loop(first_bit, first_bit + n_bits, bit_body, st)

        if done_bits == 0:
            state = run(state)
        else:
            all_settled = jnp.max(jnp.abs(state[1] - topk)) == 0.0
            state = lax.cond(all_settled, lambda st: st, run, state)
        done_bits += n_bits
    assert done_bits == L2_BITS
    prefix = state[0]
    thr = jnp.where(p16 == I16_MIN, -F32_MAX, code_to_score(prefix))

    def mask_body(m, kept):
        keep = sc_s[m] >= thr
        madd_s[m] = jnp.where(keep, 0.0, NEG_BIG)
        return fold(kept, keep)

    excess = total(lax.fori_loop(0, n_ch, mask_body, zero8), F32) - topk

    @pl.when(jnp.max(excess) > 0)
    def _():
        n_tied = total(lax.fori_loop(0, n_ch, lambda m, accs: fold(accs, sc_s[m] == thr), zero8), F32)
        need = n_tied - excess
        r = lax.broadcasted_iota(I32, (KC, KC), 0)
        c = lax.broadcasted_iota(I32, (KC, KC), 1)
        prefix_op = jnp.where(c <= r, 1.0, 0.0).astype(BF16)

        def fix_body(m, seen):
            tied = sc_s[m] == thr
            t = jnp.where(tied, 1.0, 0.0).astype(BF16)
            rank = seen + jnp.dot(prefix_op, t, preferred_element_type=F32)
            madd_s[m] = jnp.where(tied & (rank > need), NEG_BIG, madd_s[m])
            return rank[KC - 1:KC, :]

        lax.fori_loop(0, n_ch, fix_body, jnp.zeros((1, QB), F32))

    n_far = jnp.maximum(i - 1, 0)
    mx_s[...] = jnp.full(mx_s.shape, -jnp.inf, F32)
    acc_s[...] = jnp.zeros(acc_s.shape, F32)

    w_next = wTn_ref[...] * idx_scale

    def att_body(m, carry, near):
        index_chunk(m, qiTn_ref, w_next, i + 1)
        madd = madd_s[m]
        rows = pl.ds(pl.multiple_of(m * KC, KC), KC)
        for h in range(ATTN_HEADS):
            hd = slice(h * HEAD_DIM, (h + 1) * HEAD_DIM)
            s = jnp.dot(k_ref[rows, hd], qT_ref[hd, :], preferred_element_type=F32)
            if near:
                s = s + (madd + bias_s[i - m, h])
            else:
                s = s + madd
            s_s[h] = s
            mx = mx_s[h]
            m_new = jnp.maximum(mx, jnp.max(s, axis=0, keepdims=True))
            alpha_s[h] = jnp.exp2(mx - m_new)
            mx_s[h] = m_new
        ones_rows = jnp.ones((PACKED_SUBLANES, KC), BF16)
        for h in range(ATTN_HEADS):
            hd = slice(h * HEAD_DIM, (h + 1) * HEAD_DIM)
            p = jnp.exp2(s_s[h] - mx_s[h])
            v_aug = jnp.concatenate([vT_ref[m, hd, :], ones_rows], axis=0)
            pv = jnp.dot(v_aug, p.astype(BF16), preferred_element_type=F32)
            acc_s[h] = alpha_s[h] * acc_s[h] + pv
        return carry

    def far_pair(pair, carry):
        att_body(2 * pair, carry, near=False)
        att_body(2 * pair + 1, carry, near=False)
        return carry

    lax.fori_loop(0, n_far // 2, far_pair, 0)

    @pl.when(n_far % 2 == 1)
    def _():
        att_body(n_far - 1, 0, near=False)

    @pl.when(i >= 1)
    def _():
        att_body(i - 1, 0, near=True)

    att_body(i, 0, near=True)
    index_chunk(jnp.minimum(i + 1, n_q - 1), qiTn_ref, w_next, i + 1)

    for h in range(ATTN_HEADS):
        acc = acc_s[h]
        out_t = acc[:HEAD_DIM] / acc[HEAD_DIM:HEAD_DIM + 1]
        o_ref[:, h * HEAD_DIM:(h + 1) * HEAD_DIM] = out_t.T.astype(o_ref.dtype)


def _dsa(rel_bias, qT, qiT, wT, kidx, k, vT4, seq, side_weights):
    bsz = qT.shape[0]
    topk = min(TOPK_MAX, seq // 4)
    n_q = seq // QB
    n_steps = bsz * n_q
    kernel = functools.partial(_dsa_kernel, topk=topk, n_q=n_q, n_side=len(side_weights))
    nxt = lambda i: jnp.minimum(i + 1, n_q - 1)
    side_specs = []
    for w in side_weights:
        rows = w.shape[0] // n_steps
        assert rows * n_steps == w.shape[0] and rows % PACKED_SUBLANES == 0
        side_specs.append(pl.BlockSpec((rows, w.shape[1]), lambda b, i: (b * n_q + i, 0)))
    resident = dict(pipeline_mode=pl.Buffered(1))
    return pl.pallas_call(
        kernel,
        grid=(bsz, n_q),
        in_specs=[
            pl.BlockSpec(memory_space=pltpu.SMEM),
            pl.BlockSpec((None, ATTN_WIDTH, QB), lambda b, i: (b, 0, i)),
            pl.BlockSpec((None, IDX_HEADS * IDX_DIM, QB), lambda b, i: (b, 0, i)),
            pl.BlockSpec((None, IDX_HEADS, QB), lambda b, i: (b, 0, i)),
            pl.BlockSpec((None, IDX_HEADS * IDX_DIM, QB), lambda b, i: (b, 0, nxt(i))),
            pl.BlockSpec((None, IDX_HEADS, QB), lambda b, i: (b, 0, nxt(i))),
            pl.BlockSpec((None, seq, IDX_DIM), lambda b, i: (b, 0, 0)),
            pl.BlockSpec((None, seq, ATTN_WIDTH), lambda b, i: (b, 0, 0), **resident),
            pl.BlockSpec((None, seq // KC, ATTN_WIDTH, KC), lambda b, i: (b, 0, 0, 0), **resident),
        ] + side_specs,
        out_specs=[pl.BlockSpec((None, QB, ATTN_WIDTH), lambda b, i: (b, i, 0))] + side_specs,
        out_shape=[jax.ShapeDtypeStruct((bsz, seq, ATTN_WIDTH), BF16)]
        + [jax.ShapeDtypeStruct(w.shape, BF16) for w in side_weights],
        scratch_shapes=[
            pltpu.VMEM((seq // KC, KC, QB), F32),
            pltpu.VMEM((seq // KC, KC, QB), BF16),
            pltpu.VMEM((seq // KC, KC, QB), F32),
            pltpu.VMEM((2, ATTN_HEADS, KC, QB), F32),
            pltpu.VMEM((ATTN_HEADS, 1, QB), F32),
            pltpu.VMEM((ATTN_HEADS, 1, QB), F32),
            pltpu.VMEM((ATTN_HEADS, HEAD_DIM + PACKED_SUBLANES, QB), F32),
            pltpu.VMEM((ATTN_HEADS, KC, QB), F32),
        ],
        compiler_params=pltpu.CompilerParams(
            dimension_semantics=("arbitrary", "arbitrary"), vmem_limit_bytes=VMEM_LIMIT),
        name="dsa",
    )(rel_bias, qT, qiT, wT, qiT, wT, kidx, k, vT4, *side_weights)


def _mix_kernel(x_ref, u_ref, halo_ref, attn_ref, wpool_ref, pscale_ref, wout_ref, g_ref,
                x1_ref, h2_ref, ext_s, lvl0_s, lvl1_s, *, seq):
    m = pl.program_id(0)
    tiles_per_seq = seq // TM_MIX
    first = (m % tiles_per_seq) == 0
    ext_s[0:POOL_HALO, :] = jnp.where(first, 0.0, halo_ref[...])
    ext_s[POOL_HALO:, :] = u_ref[...]
    n_ext = TM_MIX + POOL_HALO
    lvl_bufs = (lvl0_s, lvl1_s)

    t_pos = (m % tiles_per_seq) * TM_MIX + lax.broadcasted_iota(I32, (TM_MIX, POOL_GC), 0)
    acc = x_ref[...] + jnp.dot(attn_ref[...], wout_ref[POOL_WIDTH:, :], preferred_element_type=F32)
    for g, w in enumerate(POOL_WINDOWS):
        cols = slice(g * POOL_GC, (g + 1) * POOL_GC)
        u = ext_s[POOL_HALO:, cols]
        levels = w.bit_length() - 1
        assert w == 1 << levels and SUBLANES * levels <= POOL_HALO
        src, src_cols = ext_s, cols
        for lv in range(levels):
            start = SUBLANES * (lv + 1)
            n_rows = n_ext - start
            t = src[pl.ds(start, n_rows), src_cols] + src[pl.ds(start - (1 << lv), n_rows), src_cols]
            if lv == levels - 1:
                wsum = t[POOL_HALO - start:, :]
            else:
                dst = lvl_bufs[lv % 2]
                dst[pl.ds(start, n_rows), :] = t
                src, src_cols = dst, slice(None)
        count = jnp.minimum(t_pos + 1, w).astype(F32)
        pooled = (wsum / count - u).astype(BF16)
        y = jnp.dot(pooled, wpool_ref[g], preferred_element_type=F32) * pscale_ref[:, cols]
        acc = acc + jnp.dot(y.astype(BF16), wout_ref[cols, :], preferred_element_type=F32)
    x1_ref[...] = acc
    h2_ref[...] = _rms(acc, g_ref[...]).astype(BF16)


def _mix(x2d, u_pool, attn2d, w_pool, pool_scale, w_out, g_ffn, seq):
    m = x2d.shape[0]
    halo_per_tile = TM_MIX // POOL_HALO
    kernel = functools.partial(_mix_kernel, seq=seq)
    return pl.pallas_call(
        kernel,
        grid=(m // TM_MIX,),
        in_specs=[
            pl.BlockSpec((TM_MIX, D_MODEL), lambda i: (i, 0)),
            pl.BlockSpec((TM_MIX, POOL_WIDTH), lambda i: (i, 0)),
            pl.BlockSpec((POOL_HALO, POOL_WIDTH), lambda i: (jnp.maximum(i * halo_per_tile - 1, 0), 0)),
            pl.BlockSpec((TM_MIX, ATTN_WIDTH), lambda i: (i, 0)),
            pl.BlockSpec((POOL_GROUPS, POOL_GC, POOL_GC), lambda i: (0, 0, 0)),
            pl.BlockSpec((1, POOL_WIDTH), lambda i: (0, 0)),
            pl.BlockSpec((D_MODEL, D_MODEL), lambda i: (0, 0)),
            pl.BlockSpec((1, D_MODEL), lambda i: (0, 0)),
        ],
        out_specs=[
            pl.BlockSpec((TM_MIX, D_MODEL), lambda i: (i, 0)),
            pl.BlockSpec((TM_MIX, D_MODEL), lambda i: (i, 0)),
        ],
        out_shape=[
            jax.ShapeDtypeStruct((m, D_MODEL), F32),
            jax.ShapeDtypeStruct((m, D_MODEL), BF16),
        ],
        scratch_shapes=[pltpu.VMEM((TM_MIX + POOL_HALO, POOL_WIDTH), F32),
                        pltpu.VMEM((TM_MIX + POOL_HALO, POOL_GC), F32),
                        pltpu.VMEM((TM_MIX + POOL_HALO, POOL_GC), F32)],
        compiler_params=pltpu.CompilerParams(
            dimension_semantics=("arbitrary",), vmem_limit_bytes=VMEM_LIMIT),
        name="mix",
    )(x2d, u_pool, u_pool, attn2d, w_pool, pool_scale, w_out, g_ffn)


def _ffn_kernel(h_ref, wg_ref, wv_ref, cwg_ref, cwv_ref, cbg_ref, cbv_ref, wd_ref,
                o_ref, carry_s, *u_bufs, seq):
    m = pl.program_id(0)
    f = pl.program_id(1)
    tiles_per_seq = seq // TM_FFN
    first = (m % tiles_per_seq) == 0

    @pl.when(f == 0)
    def _():
        o_ref[...] = jnp.zeros(o_ref.shape, F32)

    h = h_ref[...]
    n_half = TF_FFN // TF_HALF

    def up(s):
        cols = slice(s * TF_HALF, (s + 1) * TF_HALF)
        for k, w_ref in enumerate((wg_ref, wv_ref)):
            u = u_bufs[2 * s + k]
            u[0:SUBLANES, :] = jnp.where(first, 0.0, carry_s[f, 2 * s + k])
            u[SUBLANES:, :] = jnp.dot(h, w_ref[:, cols], preferred_element_type=F32)
            carry_s[f, 2 * s + k] = u[TM_FFN:, :]

    def conv(u, cols, cw_ref, cb_ref):
        out = cb_ref[:, cols]
        for j in range(CONV_WIDTH):
            shift = CONV_WIDTH - 1 - j
            out = out + u[pl.ds(SUBLANES - shift, TM_FFN), :] * cw_ref[j:j + 1, cols]
        return out

    def gated(s):
        cols = slice(s * TF_HALF, (s + 1) * TF_HALF)
        gate = conv(u_bufs[2 * s], cols, cwg_ref, cbg_ref)
        val = conv(u_bufs[2 * s + 1], cols, cwv_ref, cbv_ref)
        return (gate * jax.nn.sigmoid(gate) * val).astype(BF16)

    def down(s, act):
        return jnp.dot(act, wd_ref[s * TF_HALF:(s + 1) * TF_HALF, :], preferred_element_type=F32)

    up(0)
    act0 = gated(0)
    up(1)
    contrib = down(0, act0)
    act1 = gated(1)
    contrib = contrib + down(1, act1)
    o_ref[...] += contrib


def _ffn(h2, w_up, conv_w, conv_b, w_down, seq):
    m = h2.shape[0]
    n_f = D_FF // TF_FFN
    kernel = functools.partial(_ffn_kernel, seq=seq)
    return pl.pallas_call(
        kernel,
        grid=(m // TM_FFN, n_f),
        in_specs=[
            pl.BlockSpec((TM_FFN, D_MODEL), lambda i, f: (i, 0)),
            pl.BlockSpec((D_MODEL, TF_FFN), lambda i, f: (0, f)),
            pl.BlockSpec((D_MODEL, TF_FFN), lambda i, f: (0, f + n_f)),
            pl.BlockSpec((CONV_WIDTH, TF_FFN), lambda i, f: (0, f)),
            pl.BlockSpec((CONV_WIDTH, TF_FFN), lambda i, f: (0, f + n_f)),
            pl.BlockSpec((1, TF_FFN), lambda i, f: (0, f)),
            pl.BlockSpec((1, TF_FFN), lambda i, f: (0, f + n_f)),
            pl.BlockSpec((TF_FFN, D_MODEL), lambda i, f: (f, 0)),
        ],
        out_specs=pl.BlockSpec((TM_FFN, D_MODEL), lambda i, f: (i, 0)),
        out_shape=jax.ShapeDtypeStruct((m, D_MODEL), F32),
        scratch_shapes=[pltpu.VMEM((n_f, 2 * (TF_FFN // TF_HALF), SUBLANES, TF_HALF), F32)]
        + [pltpu.VMEM((TM_FFN + SUBLANES, TF_HALF), F32)] * (2 * (TF_FFN // TF_HALF)),
        compiler_params=pltpu.CompilerParams(
            dimension_semantics=("arbitrary", "arbitrary"), vmem_limit_bytes=VMEM_LIMIT),
        name="ffn",
    )(h2, w_up, w_up, conv_w, conv_w, conv_b, conv_b, w_down)


def _ple_kernel(x1_ref, f_ref, p_ref, gp_ref, wgate_ref, wproj_ref, gf_ref, o_ref, *, last_layer):
    x = x1_ref[...] + f_ref[...]
    hg = _rms(x, gp_ref[...]).astype(BF16)
    gate = jax.nn.sigmoid(jnp.dot(hg, wgate_ref[...], preferred_element_type=F32))
    emb = jnp.dot(p_ref[...].astype(BF16), wproj_ref[...], preferred_element_type=F32)
    x3 = x + emb * gate
    o_ref[...] = _rms(x3, gf_ref[...]) if last_layer else x3


def _ple(x1, ffn_out, p2d, g_ple, w_gate, w_proj, g_final, last_layer):
    m = x1.shape[0]
    return pl.pallas_call(
        functools.partial(_ple_kernel, last_layer=last_layer),
        grid=(m // TM_PLE,),
        in_specs=[
            pl.BlockSpec((TM_PLE, D_MODEL), lambda i: (i, 0)),
            pl.BlockSpec((TM_PLE, D_MODEL), lambda i: (i, 0)),
            pl.BlockSpec((TM_PLE, D_PLE), lambda i: (i, 0)),
            pl.BlockSpec((1, D_MODEL), lambda i: (0, 0)),
            pl.BlockSpec((D_MODEL, D_MODEL), lambda i: (0, 0), pipeline_mode=pl.Buffered(1)),
            pl.BlockSpec((D_PLE, D_MODEL), lambda i: (0, 0)),
            pl.BlockSpec((1, D_MODEL), lambda i: (0, 0)),
        ],
        out_specs=pl.BlockSpec((TM_PLE, D_MODEL), lambda i: (i, 0)),
        out_shape=jax.ShapeDtypeStruct((m, D_MODEL), F32),
        compiler_params=pltpu.CompilerParams(
            dimension_semantics=("arbitrary",), vmem_limit_bytes=VMEM_LIMIT),
        name="ple",
    )(x1, ffn_out, p2d, g_ple, w_gate, w_proj, g_final)


def kernel(x, p, g_mix, w_in, w_pool, pool_scale, rel_bias, w_out, g_ffn, w_up, conv_w, conv_b,
           w_down, g_ple, w_ple_gate, w_ple_proj, g_final):
    bsz, seq, _ = x.shape
    depth = w_in.shape[0]
    assert seq % QB == 0 and seq % TM_MIX == 0 and seq % TM_FFN == 0 and (bsz * seq) % TM_IN == 0
    x2d = x.reshape(bsz * seq, D_MODEL)
    for i in range(depth):
        u_pool, k2d, qT, qiT, vT4, small = _inproj(x2d, g_mix[i].reshape(1, D_MODEL),
                                                   w_in[i].T.astype(BF16), bsz, seq)
        small3 = small.reshape(bsz, seq, SMALL_W)
        kidx = small3[:, :, :IDX_DIM].astype(BF16)
        wT = jnp.swapaxes(small3[:, :, IDX_DIM:IDX_DIM + IDX_HEADS], 1, 2)
        attn, w_out_b, w_up_b, w_down_b, w_gate_b = _dsa(rel_bias, qT, qiT, wT, kidx,
                                               k2d.reshape(bsz, seq, ATTN_WIDTH), vT4, seq,
                                               (w_out[i], w_up[i], w_down[i], w_ple_gate[i]))

        x1, h2 = _mix(x2d, u_pool, attn.reshape(bsz * seq, ATTN_WIDTH), w_pool[i].astype(BF16),
                      pool_scale[i].reshape(1, POOL_WIDTH), w_out_b,
                      g_ffn[i].reshape(1, D_MODEL), seq)
        ffn_out = _ffn(h2, w_up_b, conv_w[i], conv_b[i].reshape(1, 2 * D_FF), w_down_b, seq)
        x2d = _ple(x1, ffn_out, p[i].reshape(bsz * seq, D_PLE), g_ple[i].reshape(1, D_MODEL),
                   w_gate_b, w_ple_proj[i].astype(BF16),
                   g_final.reshape(1, D_MODEL), last_layer=(i == depth - 1))
    return x2d.reshape(bsz, seq, D_MODEL)
```

```python
import functools
import math

import numpy as np
import jax
import jax.numpy as jnp
from jax import lax
from jax.experimental import pallas as pl
from jax.experimental.pallas import tpu as pltpu

F32 = jnp.float32
BF16 = jnp.bfloat16
I32 = jnp.int32
I16 = jnp.int16

D_MODEL = 2048
D_PLE = 256
POOL_WIDTH = 1024
POOL_GROUPS = 4
POOL_GC = 256
POOL_WINDOWS = (2, 4, 8, 16)
ATTN_HEADS = 8
HEAD_DIM = 128
ATTN_WIDTH = 1024
IDX_HEADS = 16
IDX_DIM = 64
TOPK_MAX = 256
N_BUCKETS = 32
MAX_DISTANCE = 128
D_FF = 5632
CONV_WIDTH = 3
EPS = 1e-6
G_POOL, G_Q, G_K, G_V, G_QI = range(5)
N_GROUPS = 5
GROUP_W = 1024

V7X_VMEM_BYTES = 64 * 1024 * 1024
SUBLANES = 8
PACKED_SUBLANES = 16
VMEM_LIMIT = V7X_VMEM_BYTES - 8 * 1024 * 1024

TM_IN = 512
SMALL_W = IDX_DIM + IDX_HEADS
QB = 256
KC = 256
N_ACC = 4
L2_BITS = 17
L2_PASS_GROUPS = (4, 4, 4, 5)
TM_MIX = 512
POOL_HALO = 32
TM_FFN = 1024
TF_FFN = 512
TF_HALF = 256
TM_PLE = 512

I16_MIN = -2 ** 15
NEG_BIG = -1e30
F32_MAX = float(np.finfo(np.float32).max)
LOG2E = math.log2(math.e)
Q_PRESCALE = (HEAD_DIM ** -0.5) * LOG2E


def _bucket_lower_bounds():
    n = np.arange(0, 4 * MAX_DISTANCE, dtype=np.int64)
    max_exact = N_BUCKETS // 2
    nf = np.maximum(n, 1).astype(np.float32)
    large = max_exact + (np.log(nf / np.float32(max_exact)) / np.float32(math.log(MAX_DISTANCE / max_exact))
                         * np.float32(N_BUCKETS - max_exact)).astype(np.int32)
    large = np.minimum(large, N_BUCKETS - 1)
    bucket = np.where(n < max_exact, n, large)
    assert np.all(np.diff(bucket) >= 0)
    lows = [int(np.argmax(bucket == b)) for b in range(N_BUCKETS)]
    assert all(bucket[lo] == b for b, lo in enumerate(lows))
    return lows


BUCKET_LO = _bucket_lower_bounds()


def _rms(x, g):
    ms = jnp.mean(x * x, axis=-1, keepdims=True)
    return (x * lax.rsqrt(ms + EPS)) * g


def _nt_dot(a, b):
    return lax.dot_general(a, b, (((1,), (1,)), ((), ())), preferred_element_type=F32)


def _inproj_kernel(x_ref, g_ref, wt_ref, pool_ref, k_ref, qT_ref, qiT_ref, vT_ref, small_ref):
    hb = _rms(x_ref[...], g_ref[...]).astype(BF16)

    def w(group):
        return wt_ref[group * GROUP_W:(group + 1) * GROUP_W, :]

    pool_ref[...] = _nt_dot(hb, w(G_POOL))
    k_ref[...] = _nt_dot(hb, w(G_K)).astype(BF16)
    small_ref[...] = _nt_dot(hb, wt_ref[N_GROUPS * GROUP_W:, :])
    qT_ref[...] = (_nt_dot(w(G_Q), hb) * Q_PRESCALE).astype(BF16)
    qiT_ref[...] = _nt_dot(w(G_QI), hb).astype(BF16)
    out_t = _nt_dot(w(G_V), hb).astype(BF16)
    for j in range(TM_IN // KC):
        vT_ref[j] = out_t[:, j * KC:(j + 1) * KC]


def _inproj(x2d, g, wt, bsz, seq):
    m = x2d.shape[0]
    tiles_per_seq = seq // TM_IN
    assert wt.shape[0] == N_GROUPS * GROUP_W + SMALL_W
    return pl.pallas_call(
        _inproj_kernel,
        grid=(m // TM_IN,),
        in_specs=[
            pl.BlockSpec((TM_IN, D_MODEL), lambda i: (i, 0)),
            pl.BlockSpec((1, D_MODEL), lambda i: (0, 0)),
            pl.BlockSpec(wt.shape, lambda i: (0, 0), pipeline_mode=pl.Buffered(1)),
        ],
        out_specs=[
            pl.BlockSpec((TM_IN, GROUP_W), lambda i: (i, 0)),
            pl.BlockSpec((TM_IN, GROUP_W), lambda i: (i, 0)),
            pl.BlockSpec((None, GROUP_W, TM_IN), lambda i: (i // tiles_per_seq, 0, i % tiles_per_seq)),
            pl.BlockSpec((None, GROUP_W, TM_IN), lambda i: (i // tiles_per_seq, 0, i % tiles_per_seq)),
            pl.BlockSpec((None, TM_IN // KC, GROUP_W, KC),
                         lambda i: (i // tiles_per_seq, i % tiles_per_seq, 0, 0)),
            pl.BlockSpec((TM_IN, SMALL_W), lambda i: (i, 0)),
        ],
        out_shape=[
            jax.ShapeDtypeStruct((m, POOL_WIDTH), F32),
            jax.ShapeDtypeStruct((m, ATTN_WIDTH), BF16),
            jax.ShapeDtypeStruct((bsz, ATTN_WIDTH, seq), BF16),
            jax.ShapeDtypeStruct((bsz, IDX_HEADS * IDX_DIM, seq), BF16),
            jax.ShapeDtypeStruct((bsz, seq // KC, ATTN_WIDTH, KC), BF16),
            jax.ShapeDtypeStruct((m, SMALL_W), F32),
        ],
        compiler_params=pltpu.CompilerParams(
            dimension_semantics=("arbitrary",), vmem_limit_bytes=VMEM_LIMIT),
        name="inproj",
    )(x2d, g, wt)


def _dsa_kernel(rel_ref, qT_ref, qiT_ref, wT_ref, qiTn_ref, wTn_ref, kidx_ref, k_ref, vT_ref, *rest,
                topk, n_q, n_side):
    side_in, rest = rest[:n_side], rest[n_side:]
    o_ref, side_out = rest[0], rest[1:1 + n_side]
    sc_s, sb_s, madd_s, bias_s, mx_s, alpha_s, acc_s, s_s = rest[1 + n_side:]
    b = pl.program_id(0)
    i = pl.program_id(1)
    n_ch = i + 1

    for src, dst in zip(side_in, side_out):
        dst[...] = src[...].astype(dst.dtype)

    @pl.when((b == 0) & (i == 0))
    def _():
        r = lax.broadcasted_iota(I32, (KC, QB), 0)
        c = lax.broadcasted_iota(I32, (KC, QB), 1)
        for off in range(2):
            dist = jnp.maximum(c - r + off * QB, 0)

            def head_body(h, carry, dist=dist, off=off):
                t = jnp.full((KC, QB), rel_ref[0, h], F32)
                for bk in range(1, N_BUCKETS):
                    t = jnp.where(dist >= BUCKET_LO[bk], rel_ref[bk, h], t)
                bias_s[off, h] = (t - rel_ref[N_BUCKETS - 1, h]) * LOG2E
                return carry

            lax.fori_loop(0, ATTN_HEADS, head_body, 0)

    idx_scale = (IDX_HEADS ** -0.5) * (IDX_DIM ** -0.5)
    row = lax.broadcasted_iota(I32, (KC, QB), 0)
    col = lax.broadcasted_iota(I32, (KC, QB), 1)

    def index_chunk(m, qi_ref, w_scaled, q_block):
        kc = kidx_ref[pl.ds(pl.multiple_of(m * KC, KC), KC), :]
        acc = jnp.zeros((KC, QB), F32)
        for h in range(IDX_HEADS):
            d = jnp.dot(kc, qi_ref[h * IDX_DIM:(h + 1) * IDX_DIM, :],
                        preferred_element_type=F32)
            acc = acc + w_scaled[h:h + 1, :] * jnp.maximum(d, 0.0)
        causal = (m * KC + row) <= (q_block * QB + col)
        score = jnp.where(causal, acc, -jnp.inf)
        sc_s[m] = score
        sb_s[m] = score.astype(BF16)

    @pl.when(i == 0)
    def _():
        index_chunk(0, qiT_ref, wT_ref[...] * idx_scale, 0)

    def code_to_score(code):
        bits = jnp.where(code >= 0, code, code ^ 0x7FFFFFFF)
        return pltpu.bitcast(bits, F32)

    def code16_to_bf16(code):
        bits = jnp.where(code >= 0, code, code ^ 0x7FFF)
        return pltpu.bitcast(bits.astype(I16), BF16)

    def spread(accs, ones, rows):
        accs = list(accs)
        for j in range(KC // rows):
            accs[j % N_ACC] = accs[j % N_ACC] + ones[j * rows:(j + 1) * rows, :]
        return tuple(accs)

    def total(accs, dtype):
        acc = functools.reduce(lambda a, b: a + b, accs)
        return jnp.sum(acc.astype(dtype), axis=0, keepdims=True)

    zero16 = (jnp.zeros((PACKED_SUBLANES, QB), I16),) * N_ACC
    zero8 = (jnp.zeros((SUBLANES, QB), F32),) * N_ACC

    def count16_ge(cand):
        def body(m, accs):
            return spread(accs, (sb_s[m] >= cand).astype(I16), PACKED_SUBLANES)

        return total(lax.fori_loop(0, n_ch, body, zero16), I32)

    def bit16_body(it, prefix):
        cand = prefix + jnp.left_shift(jnp.int32(1), 15 - it)
        return jnp.where(count16_ge(code16_to_bf16(cand)) >= topk, cand, prefix)

    p16 = lax.fori_loop(0, 16, bit16_body, jnp.full((1, QB), I16_MIN, I32))

    def fold(accs, flags):
        return spread(accs, jnp.where(flags, 1.0, 0.0), SUBLANES)

    def count_ge(cand):
        return total(lax.fori_loop(0, n_ch, lambda m, accs: fold(accs, sc_s[m] >= cand), zero8), F32)

    below = p16 - 1
    lo = jnp.left_shift(below, 16) + jnp.where(below < 0, 0xFFFF, 0)

    def bit_body(it, state):
        prefix, kept = state
        cand = prefix + jnp.left_shift(jnp.int32(1), L2_BITS - 1 - it)
        cnt = count_ge(code_to_score(cand))
        ok = cnt >= topk
        return jnp.where(ok, cand, prefix), jnp.where(ok, cnt, kept)

    settled_at = jnp.where(p16 == I16_MIN, float(topk), float(topk + 1))
    state = (lo, settled_at)
    done_bits = 0
    for n_bits in L2_PASS_GROUPS:
        first_bit = done_bits

        def run(st, first_bit=first_bit, n_bits=n_bits):
            return lax.fori_loop(first_bit, first_bit + n_bits, bit_body, st)

        if done_bits == 0:
            state = run(state)
        else:
            all_settled = jnp.max(jnp.abs(state[1] - topk)) == 0.0
            state = lax.cond(all_settled, lambda st: st, run, state)
        done_bits += n_bits
    assert done_bits == L2_BITS
    prefix = state[0]
    thr = jnp.where(p16 == I16_MIN, -F32_MAX, code_to_score(prefix))

    def mask_body(m, kept):
        keep = sc_s[m] >= thr
        madd_s[m] = jnp.where(keep, 0.0, NEG_BIG)
        return fold(kept, keep)

    excess = total(lax.fori_loop(0, n_ch, mask_body, zero8), F32) - topk

    @pl.when(jnp.max(excess) > 0)
    def _():
        n_tied = total(lax.fori_loop(0, n_ch, lambda m, accs: fold(accs, sc_s[m] == thr), zero8), F32)
        need = n_tied - excess
        r = lax.broadcasted_iota(I32, (KC, KC), 0)
        c = lax.broadcasted_iota(I32, (KC, KC), 1)
        prefix_op = jnp.where(c <= r, 1.0, 0.0).astype(BF16)

        def fix_body(m, seen):
            tied = sc_s[m] == thr
            t = jnp.where(tied, 1.0, 0.0).astype(BF16)
            rank = seen + jnp.dot(prefix_op, t, preferred_element_type=F32)
            madd_s[m] = jnp.where(tied & (rank > need), NEG_BIG, madd_s[m])
            return rank[KC - 1:KC, :]

        lax.fori_loop(0, n_ch, fix_body, jnp.zeros((1, QB), F32))

    n_far = jnp.maximum(i - 1, 0)
    mx_s[...] = jnp.full(mx_s.shape, -jnp.inf, F32)
    acc_s[...] = jnp.zeros(acc_s.shape, F32)

    w_next = wTn_ref[...] * idx_scale

    def att_body(m, carry, near):
        index_chunk(m, qiTn_ref, w_next, i + 1)
        madd = madd_s[m]
        rows = pl.ds(pl.multiple_of(m * KC, KC), KC)
        for h in range(ATTN_HEADS):
            hd = slice(h * HEAD_DIM, (h + 1) * HEAD_DIM)
            s = jnp.dot(k_ref[rows, hd], qT_ref[hd, :], preferred_element_type=F32)
            if near:
                s = s + (madd + bias_s[i - m, h])
            else:
                s = s + madd
            s_s[h] = s
            mx = mx_s[h]
            m_new = jnp.maximum(mx, jnp.max(s, axis=0, keepdims=True))
            alpha_s[h] = jnp.exp2(mx - m_new)
            mx_s[h] = m_new
        ones_rows = jnp.ones((PACKED_SUBLANES, KC), BF16)
        for h in range(ATTN_HEADS):
            hd = slice(h * HEAD_DIM, (h + 1) * HEAD_DIM)
            p = jnp.exp2(s_s[h] - mx_s[h])
            v_aug = jnp.concatenate([vT_ref[m, hd, :], ones_rows], axis=0)
            pv = jnp.dot(v_aug, p.astype(BF16), preferred_element_type=F32)
            acc_s[h] = alpha_s[h] * acc_s[h] + pv
        return carry

    def far_group(size, start):
        def body(g, carry):
            for j in range(size):
                att_body(start + size * g + j, carry, near=False)
            return carry
        return body

    n_quads = n_far // 4
    lax.fori_loop(0, n_quads, far_group(4, 0), 0)
    lax.fori_loop(0, (n_far % 4) // 2, far_group(2, 4 * n_quads), 0)
    lax.fori_loop(n_far - n_far % 2, n_far, far_group(1, 0), 0)

    @pl.when(i >= 1)
    def _():
        att_body(i - 1, 0, near=True)

    att_body(i, 0, near=True)
    index_chunk(jnp.minimum(i + 1, n_q - 1), qiTn_ref, w_next, i + 1)

    for h in range(ATTN_HEADS):
        acc = acc_s[h]
        out_t = acc[:HEAD_DIM] / acc[HEAD_DIM:HEAD_DIM + 1]
        o_ref[:, h * HEAD_DIM:(h + 1) * HEAD_DIM] = out_t.T.astype(o_ref.dtype)


def _dsa(rel_bias, qT, qiT, wT, kidx, k, vT4, seq, side_weights):
    bsz = qT.shape[0]
    topk = min(TOPK_MAX, seq // 4)
    n_q = seq // QB
    n_steps = bsz * n_q
    kernel = functools.partial(_dsa_kernel, topk=topk, n_q=n_q, n_side=len(side_weights))
    nxt = lambda i: jnp.minimum(i + 1, n_q - 1)
    side_specs = []
    for w in side_weights:
        rows = w.shape[0] // n_steps
        assert rows * n_steps == w.shape[0] and rows % PACKED_SUBLANES == 0
        side_specs.append(pl.BlockSpec((rows, w.shape[1]), lambda b, i: (b * n_q + i, 0)))
    resident = dict(pipeline_mode=pl.Buffered(1))
    return pl.pallas_call(
        kernel,
        grid=(bsz, n_q),
        in_specs=[
            pl.BlockSpec(memory_space=pltpu.SMEM),
            pl.BlockSpec((None, ATTN_WIDTH, QB), lambda b, i: (b, 0, i)),
            pl.BlockSpec((None, IDX_HEADS * IDX_DIM, QB), lambda b, i: (b, 0, i)),
            pl.BlockSpec((None, IDX_HEADS, QB), lambda b, i: (b, 0, i)),
            pl.BlockSpec((None, IDX_HEADS * IDX_DIM, QB), lambda b, i: (b, 0, nxt(i))),
            pl.BlockSpec((None, IDX_HEADS, QB), lambda b, i: (b, 0, nxt(i))),
            pl.BlockSpec((None, seq, IDX_DIM), lambda b, i: (b, 0, 0)),
            pl.BlockSpec((None, seq, ATTN_WIDTH), lambda b, i: (b, 0, 0), **resident),
            pl.BlockSpec((None, seq // KC, ATTN_WIDTH, KC), lambda b, i: (b, 0, 0, 0), **resident),
        ] + side_specs,
        out_specs=[pl.BlockSpec((None, QB, ATTN_WIDTH), lambda b, i: (b, i, 0))] + side_specs,
        out_shape=[jax.ShapeDtypeStruct((bsz, seq, ATTN_WIDTH), BF16)]
        + [jax.ShapeDtypeStruct(w.shape, BF16) for w in side_weights],
        scratch_shapes=[
            pltpu.VMEM((seq // KC, KC, QB), F32),
            pltpu.VMEM((seq // KC, KC, QB), BF16),
            pltpu.VMEM((seq // KC, KC, QB), F32),
            pltpu.VMEM((2, ATTN_HEADS, KC, QB), F32),
            pltpu.VMEM((ATTN_HEADS, 1, QB), F32),
            pltpu.VMEM((ATTN_HEADS, 1, QB), F32),
            pltpu.VMEM((ATTN_HEADS, HEAD_DIM + PACKED_SUBLANES, QB), F32),
            pltpu.VMEM((ATTN_HEADS, KC, QB), F32),
        ],
        compiler_params=pltpu.CompilerParams(
            dimension_semantics=("arbitrary", "arbitrary"), vmem_limit_bytes=VMEM_LIMIT),
        name="dsa",
    )(rel_bias, qT, qiT, wT, qiT, wT, kidx, k, vT4, *side_weights)


def _mix_kernel(x_ref, u_ref, halo_ref, attn_ref, wpool_ref, pscale_ref, wout_ref, g_ref,
                x1_ref, h2_ref, ext_s, lvl0_s, lvl1_s, *, seq):
    m = pl.program_id(0)
    tiles_per_seq = seq // TM_MIX
    first = (m % tiles_per_seq) == 0
    ext_s[0:POOL_HALO, :] = jnp.where(first, 0.0, halo_ref[...])
    ext_s[POOL_HALO:, :] = u_ref[...]
    n_ext = TM_MIX + POOL_HALO
    lvl_bufs = (lvl0_s, lvl1_s)

    t_pos = (m % tiles_per_seq) * TM_MIX + lax.broadcasted_iota(I32, (TM_MIX, POOL_GC), 0)
    acc = x_ref[...] + jnp.dot(attn_ref[...], wout_ref[POOL_WIDTH:, :], preferred_element_type=F32)
    for g, w in enumerate(POOL_WINDOWS):
        cols = slice(g * POOL_GC, (g + 1) * POOL_GC)
        u = ext_s[POOL_HALO:, cols]
        levels = w.bit_length() - 1
        assert w == 1 << levels and SUBLANES * levels <= POOL_HALO
        src, src_cols = ext_s, cols
        for lv in range(levels):
            start = SUBLANES * (lv + 1)
            n_rows = n_ext - start
            t = src[pl.ds(start, n_rows), src_cols] + src[pl.ds(start - (1 << lv), n_rows), src_cols]
            if lv == levels - 1:
                wsum = t[POOL_HALO - start:, :]
            else:
                dst = lvl_bufs[lv % 2]
                dst[pl.ds(start, n_rows), :] = t
                src, src_cols = dst, slice(None)
        count = jnp.minimum(t_pos + 1, w).astype(F32)
        pooled = (wsum / count - u).astype(BF16)
        y = jnp.dot(pooled, wpool_ref[g], preferred_element_type=F32) * pscale_ref[:, cols]
        acc = acc + jnp.dot(y.astype(BF16), wout_ref[cols, :], preferred_element_type=F32)
    x1_ref[...] = acc
    h2_ref[...] = _rms(acc, g_ref[...]).astype(BF16)


def _mix(x2d, u_pool, attn2d, w_pool, pool_scale, w_out, g_ffn, seq):
    m = x2d.shape[0]
    halo_per_tile = TM_MIX // POOL_HALO
    kernel = functools.partial(_mix_kernel, seq=seq)
    return pl.pallas_call(
        kernel,
        grid=(m // TM_MIX,),
        in_specs=[
            pl.BlockSpec((TM_MIX, D_MODEL), lambda i: (i, 0)),
            pl.BlockSpec((TM_MIX, POOL_WIDTH), lambda i: (i, 0)),
            pl.BlockSpec((POOL_HALO, POOL_WIDTH), lambda i: (jnp.maximum(i * halo_per_tile - 1, 0), 0)),
            pl.BlockSpec((TM_MIX, ATTN_WIDTH), lambda i: (i, 0)),
            pl.BlockSpec((POOL_GROUPS, POOL_GC, POOL_GC), lambda i: (0, 0, 0)),
            pl.BlockSpec((1, POOL_WIDTH), lambda i: (0, 0)),
            pl.BlockSpec((D_MODEL, D_MODEL), lambda i: (0, 0)),
            pl.BlockSpec((1, D_MODEL), lambda i: (0, 0)),
        ],
        out_specs=[
            pl.BlockSpec((TM_MIX, D_MODEL), lambda i: (i, 0)),
            pl.BlockSpec((TM_MIX, D_MODEL), lambda i: (i, 0)),
        ],
        out_shape=[
            jax.ShapeDtypeStruct((m, D_MODEL), F32),
            jax.ShapeDtypeStruct((m, D_MODEL), BF16),
        ],
        scratch_shapes=[pltpu.VMEM((TM_MIX + POOL_HALO, POOL_WIDTH), F32),
                        pltpu.VMEM((TM_MIX + POOL_HALO, POOL_GC), F32),
                        pltpu.VMEM((TM_MIX + POOL_HALO, POOL_GC), F32)],
        compiler_params=pltpu.CompilerParams(
            dimension_semantics=("arbitrary",), vmem_limit_bytes=VMEM_LIMIT),
        name="mix",
    )(x2d, u_pool, u_pool, attn2d, w_pool, pool_scale, w_out, g_ffn)


def _ffn_kernel(h_ref, wg_ref, wv_ref, cwg_ref, cwv_ref, cbg_ref, cbv_ref, wd_ref,
                o_ref, carry_s, *u_bufs, seq):
    m = pl.program_id(0)
    f = pl.program_id(1)
    tiles_per_seq = seq // TM_FFN
    first = (m % tiles_per_seq) == 0

    @pl.when(f == 0)
    def _():
        o_ref[...] = jnp.zeros(o_ref.shape, F32)

    h = h_ref[...]
    n_half = TF_FFN // TF_HALF

    def up(s):
        cols = slice(s * TF_HALF, (s + 1) * TF_HALF)
        for k, w_ref in enumerate((wg_ref, wv_ref)):
            u = u_bufs[2 * s + k]
            u[0:SUBLANES, :] = jnp.where(first, 0.0, carry_s[f, 2 * s + k])
            u[SUBLANES:, :] = jnp.dot(h, w_ref[:, cols], preferred_element_type=F32)
            carry_s[f, 2 * s + k] = u[TM_FFN:, :]

    def conv(u, cols, cw_ref, cb_ref):
        out = cb_ref[:, cols]
        for j in range(CONV_WIDTH):
            shift = CONV_WIDTH - 1 - j
            out = out + u[pl.ds(SUBLANES - shift, TM_FFN), :] * cw_ref[j:j + 1, cols]
        return out

    def gated(s):
        cols = slice(s * TF_HALF, (s + 1) * TF_HALF)
        gate = conv(u_bufs[2 * s], cols, cwg_ref, cbg_ref)
        val = conv(u_bufs[2 * s + 1], cols, cwv_ref, cbv_ref)
        return (gate * jax.nn.sigmoid(gate) * val).astype(BF16)

    def down(s, act):
        return jnp.dot(act, wd_ref[s * TF_HALF:(s + 1) * TF_HALF, :], preferred_element_type=F32)

    contrib = None
    for s in range(n_half):
        up(s)
        part = down(s, gated(s))
        contrib = part if contrib is None else contrib + part
    o_ref[...] += contrib


def _ffn(h2, w_up, conv_w, conv_b, w_down, seq):
    m = h2.shape[0]
    n_f = D_FF // TF_FFN
    kernel = functools.partial(_ffn_kernel, seq=seq)
    return pl.pallas_call(
        kernel,
        grid=(m // TM_FFN, n_f),
        in_specs=[
            pl.BlockSpec((TM_FFN, D_MODEL), lambda i, f: (i, 0)),
            pl.BlockSpec((D_MODEL, TF_FFN), lambda i, f: (0, f)),
            pl.BlockSpec((D_MODEL, TF_FFN), lambda i, f: (0, f + n_f)),
            pl.BlockSpec((CONV_WIDTH, TF_FFN), lambda i, f: (0, f)),
            pl.BlockSpec((CONV_WIDTH, TF_FFN), lambda i, f: (0, f + n_f)),
            pl.BlockSpec((1, TF_FFN), lambda i, f: (0, f)),
            pl.BlockSpec((1, TF_FFN), lambda i, f: (0, f + n_f)),
            pl.BlockSpec((TF_FFN, D_MODEL), lambda i, f: (f, 0)),
        ],
        out_specs=pl.BlockSpec((TM_FFN, D_MODEL), lambda i, f: (i, 0)),
        out_shape=jax.ShapeDtypeStruct((m, D_MODEL), F32),
        scratch_shapes=[pltpu.VMEM((n_f, 2 * (TF_FFN // TF_HALF), SUBLANES, TF_HALF), F32)]
        + [pltpu.VMEM((TM_FFN + SUBLANES, TF_HALF), F32)] * (2 * (TF_FFN // TF_HALF)),
        compiler_params=pltpu.CompilerParams(
            dimension_semantics=("arbitrary", "arbitrary"), vmem_limit_bytes=VMEM_LIMIT),
        name="ffn",
    )(h2, w_up, w_up, conv_w, conv_w, conv_b, conv_b, w_down)


def _ple_kernel(x1_ref, f_ref, p_ref, gp_ref, wgate_ref, wproj_ref, gf_ref, o_ref, *, last_layer):
    x = x1_ref[...] + f_ref[...]
    hg = _rms(x, gp_ref[...]).astype(BF16)
    gate = jax.nn.sigmoid(jnp.dot(hg, wgate_ref[...], preferred_element_type=F32))
    emb = jnp.dot(p_ref[...].astype(BF16), wproj_ref[...], preferred_element_type=F32)
    x3 = x + emb * gate
    o_ref[...] = _rms(x3, gf_ref[...]) if last_layer else x3


def _ple(x1, ffn_out, p2d, g_ple, w_gate, w_proj, g_final, last_layer):
    m = x1.shape[0]
    return pl.pallas_call(
        functools.partial(_ple_kernel, last_layer=last_layer),
        grid=(m // TM_PLE,),
        in_specs=[
            pl.BlockSpec((TM_PLE, D_MODEL), lambda i: (i, 0)),
            pl.BlockSpec((TM_PLE, D_MODEL), lambda i: (i, 0)),
            pl.BlockSpec((TM_PLE, D_PLE), lambda i: (i, 0)),
            pl.BlockSpec((1, D_MODEL), lambda i: (0, 0)),
            pl.BlockSpec((D_MODEL, D_MODEL), lambda i: (0, 0), pipeline_mode=pl.Buffered(1)),
            pl.BlockSpec((D_PLE, D_MODEL), lambda i: (0, 0)),
            pl.BlockSpec((1, D_MODEL), lambda i: (0, 0)),
        ],
        out_specs=pl.BlockSpec((TM_PLE, D_MODEL), lambda i: (i, 0)),
        out_shape=jax.ShapeDtypeStruct((m, D_MODEL), F32),
        compiler_params=pltpu.CompilerParams(
            dimension_semantics=("arbitrary",), vmem_limit_bytes=VMEM_LIMIT),
        name="ple",
    )(x1, ffn_out, p2d, g_ple, w_gate, w_proj, g_final)


def kernel(x, p, g_mix, w_in, w_pool, pool_scale, rel_bias, w_out, g_ffn, w_up, conv_w, conv_b,
           w_down, g_ple, w_ple_gate, w_ple_proj, g_final):
    bsz, seq, _ = x.shape
    depth = w_in.shape[0]
    assert seq % QB == 0 and seq % TM_MIX == 0 and seq % TM_FFN == 0 and (bsz * seq) % TM_IN == 0
    x2d = x.reshape(bsz * seq, D_MODEL)
    for i in range(depth):
        u_pool, k2d, qT, qiT, vT4, small = _inproj(x2d, g_mix[i].reshape(1, D_MODEL),
                                                   w_in[i].T.astype(BF16), bsz, seq)
        small3 = small.reshape(bsz, seq, SMALL_W)
        kidx = small3[:, :, :IDX_DIM].astype(BF16)
        wT = jnp.swapaxes(small3[:, :, IDX_DIM:IDX_DIM + IDX_HEADS], 1, 2)
        attn, w_out_b, w_up_b, w_down_b, w_gate_b = _dsa(rel_bias, qT, qiT, wT, kidx,
                                               k2d.reshape(bsz, seq, ATTN_WIDTH), vT4, seq,
                                               (w_out[i], w_up[i], w_down[i], w_ple_gate[i]))

        x1, h2 = _mix(x2d, u_pool, attn.reshape(bsz * seq, ATTN_WIDTH), w_pool[i].astype(BF16),
                      pool_scale[i].reshape(1, POOL_WIDTH), w_out_b,
                      g_ffn[i].reshape(1, D_MODEL), seq)
        ffn_out = _ffn(h2, w_up_b, conv_w[i], conv_b[i].reshape(1, 2 * D_FF), w_down_b, seq)
        x2d = _ple(x1, ffn_out, p[i].reshape(bsz * seq, D_PLE), g_ple[i].reshape(1, D_MODEL),
                   w_gate_b, w_ple_proj[i].astype(BF16),
                   g_final.reshape(1, D_MODEL), last_layer=(i == depth - 1))
    return x2d.reshape(bsz, seq, D_MODEL)
```

```python
import functools
import math

import numpy as np
import jax
import jax.numpy as jnp
from jax import lax
from jax.experimental import pallas as pl
from jax.experimental.pallas import tpu as pltpu

F32 = jnp.float32
BF16 = jnp.bfloat16
I32 = jnp.int32
I16 = jnp.int16

D_MODEL = 2048
D_PLE = 256
POOL_WIDTH = 1024
POOL_GROUPS = 4
POOL_GC = 256
POOL_WINDOWS = (2, 4, 8, 16)
ATTN_HEADS = 8
HEAD_DIM = 128
ATTN_WIDTH = 1024
IDX_HEADS = 16
IDX_DIM = 64
TOPK_MAX = 256
N_BUCKETS = 32
MAX_DISTANCE = 128
D_FF = 5632
CONV_WIDTH = 3
EPS = 1e-6
G_POOL, G_Q, G_K, G_V, G_QI = range(5)
N_GROUPS = 5
GROUP_W = 1024

V7X_VMEM_BYTES = 64 * 1024 * 1024
SUBLANES = 8
PACKED_SUBLANES = 16
VMEM_LIMIT = V7X_VMEM_BYTES - 8 * 1024 * 1024

TM_IN = 512
SMALL_W = IDX_DIM + IDX_HEADS
QB = 256
KC = 256
N_ACC = 4
L2_BITS = 17
L2_PASS_GROUPS = (4, 4, 4, 5)
TM_MIX = 512
POOL_HALO = 32
TM_FFN = 1024
TF_FFN = 512
TF_HALF = 256
TM_PLE = 512

I16_MIN = -2 ** 15
NEG_BIG = -1e30
F32_MAX = float(np.finfo(np.float32).max)
LOG2E = math.log2(math.e)
Q_PRESCALE = (HEAD_DIM ** -0.5) * LOG2E


def _bucket_lower_bounds():
    n = np.arange(0, 4 * MAX_DISTANCE, dtype=np.int64)
    max_exact = N_BUCKETS // 2
    nf = np.maximum(n, 1).astype(np.float32)
    large = max_exact + (np.log(nf / np.float32(max_exact)) / np.float32(math.log(MAX_DISTANCE / max_exact))
                         * np.float32(N_BUCKETS - max_exact)).astype(np.int32)
    large = np.minimum(large, N_BUCKETS - 1)
    bucket = np.where(n < max_exact, n, large)
    assert np.all(np.diff(bucket) >= 0)
    lows = [int(np.argmax(bucket == b)) for b in range(N_BUCKETS)]
    assert all(bucket[lo] == b for b, lo in enumerate(lows))
    return lows


BUCKET_LO = _bucket_lower_bounds()


def _rms(x, g):
    ms = jnp.mean(x * x, axis=-1, keepdims=True)
    return (x * lax.rsqrt(ms + EPS)) * g


def _nt_dot(a, b):
    return lax.dot_general(a, b, (((1,), (1,)), ((), ())), preferred_element_type=F32)


def _inproj_kernel(x_ref, g_ref, wt_ref, pool_ref, k_ref, qT_ref, qiT_ref, vT_ref, small_ref):
    hb = _rms(x_ref[...], g_ref[...]).astype(BF16)

    def w(group):
        return wt_ref[group * GROUP_W:(group + 1) * GROUP_W, :]

    pool_ref[...] = _nt_dot(hb, w(G_POOL))
    k_ref[...] = _nt_dot(hb, w(G_K)).astype(BF16)
    small_ref[...] = _nt_dot(hb, wt_ref[N_GROUPS * GROUP_W:, :])
    qT_ref[...] = (_nt_dot(w(G_Q), hb) * Q_PRESCALE).astype(BF16)
    qiT_ref[...] = _nt_dot(w(G_QI), hb).astype(BF16)
    out_t = _nt_dot(w(G_V), hb).astype(BF16)
    for j in range(TM_IN // KC):
        vT_ref[j] = out_t[:, j * KC:(j + 1) * KC]


def _inproj(x2d, g, wt, bsz, seq):
    m = x2d.shape[0]
    tiles_per_seq = seq // TM_IN
    assert wt.shape[0] == N_GROUPS * GROUP_W + SMALL_W
    return pl.pallas_call(
        _inproj_kernel,
        grid=(m // TM_IN,),
        in_specs=[
            pl.BlockSpec((TM_IN, D_MODEL), lambda i: (i, 0)),
            pl.BlockSpec((1, D_MODEL), lambda i: (0, 0)),
            pl.BlockSpec(wt.shape, lambda i: (0, 0), pipeline_mode=pl.Buffered(1)),
        ],
        out_specs=[
            pl.BlockSpec((TM_IN, GROUP_W), lambda i: (i, 0)),
            pl.BlockSpec((TM_IN, GROUP_W), lambda i: (i, 0)),
            pl.BlockSpec((None, GROUP_W, TM_IN), lambda i: (i // tiles_per_seq, 0, i % tiles_per_seq)),
            pl.BlockSpec((None, GROUP_W, TM_IN), lambda i: (i // tiles_per_seq, 0, i % tiles_per_seq)),
            pl.BlockSpec((None, TM_IN // KC, GROUP_W, KC),
                         lambda i: (i // tiles_per_seq, i % tiles_per_seq, 0, 0)),
            pl.BlockSpec((TM_IN, SMALL_W), lambda i: (i, 0)),
        ],
        out_shape=[
            jax.ShapeDtypeStruct((m, POOL_WIDTH), F32),
            jax.ShapeDtypeStruct((m, ATTN_WIDTH), BF16),
            jax.ShapeDtypeStruct((bsz, ATTN_WIDTH, seq), BF16),
            jax.ShapeDtypeStruct((bsz, IDX_HEADS * IDX_DIM, seq), BF16),
            jax.ShapeDtypeStruct((bsz, seq // KC, ATTN_WIDTH, KC), BF16),
            jax.ShapeDtypeStruct((m, SMALL_W), F32),
        ],
        compiler_params=pltpu.CompilerParams(
            dimension_semantics=("arbitrary",), vmem_limit_bytes=VMEM_LIMIT),
        name="inproj",
    )(x2d, g, wt)


def _dsa_kernel(rel_ref, qT_ref, qiT_ref, wT_ref, qiTn_ref, wTn_ref, kidx_ref, k_hbm, vT_hbm, *rest,
                topk, n_q, n_side):
    side_in, rest = rest[:n_side], rest[n_side:]
    o_ref, side_out = rest[0], rest[1:1 + n_side]
    sc_s, sb_s, madd_s, bias_s, mx_s, alpha_s, acc_s, s_s, k_ref, vT_ref, kv_sem = rest[1 + n_side:]
    b = pl.program_id(0)
    i = pl.program_id(1)
    n_ch = i + 1

    def k_copy(m):
        rows = pl.ds(pl.multiple_of(m * KC, KC), KC)
        return pltpu.make_async_copy(k_hbm.at[b, rows, :], k_ref.at[rows, :], kv_sem.at[0, m])

    def v_copy(m):
        return pltpu.make_async_copy(vT_hbm.at[b, m], vT_ref.at[m], kv_sem.at[1, m])

    @pl.when(i == 0)
    def _():
        for m in range(n_q):
            k_copy(m).start()
            v_copy(m).start()

    for src, dst in zip(side_in, side_out):
        dst[...] = src[...].astype(dst.dtype)

    @pl.when((b == 0) & (i == 0))
    def _():
        r = lax.broadcasted_iota(I32, (KC, QB), 0)
        c = lax.broadcasted_iota(I32, (KC, QB), 1)
        for off in range(2):
            dist = jnp.maximum(c - r + off * QB, 0)

            def head_body(h, carry, dist=dist, off=off):
                t = jnp.full((KC, QB), rel_ref[0, h], F32)
                for bk in range(1, N_BUCKETS):
                    t = jnp.where(dist >= BUCKET_LO[bk], rel_ref[bk, h], t)
                bias_s[off, h] = (t - rel_ref[N_BUCKETS - 1, h]) * LOG2E
                return carry

            lax.fori_loop(0, ATTN_HEADS, head_body, 0)

    idx_scale = (IDX_HEADS ** -0.5) * (IDX_DIM ** -0.5)
    row = lax.broadcasted_iota(I32, (KC, QB), 0)
    col = lax.broadcasted_iota(I32, (KC, QB), 1)

    def index_chunk(m, qi_ref, w_scaled, q_block):
        kc = kidx_ref[pl.ds(pl.multiple_of(m * KC, KC), KC), :]
        acc = jnp.zeros((KC, QB), F32)
        for h in range(IDX_HEADS):
            d = jnp.dot(kc, qi_ref[h * IDX_DIM:(h + 1) * IDX_DIM, :],
                        preferred_element_type=F32)
            acc = acc + w_scaled[h:h + 1, :] * jnp.maximum(d, 0.0)
        causal = (m * KC + row) <= (q_block * QB + col)
        score = jnp.where(causal, acc, -jnp.inf)
        sc_s[m] = score
        sb_s[m] = score.astype(BF16)

    @pl.when(i == 0)
    def _():
        index_chunk(0, qiT_ref, wT_ref[...] * idx_scale, 0)

    def code_to_score(code):
        bits = jnp.where(code >= 0, code, code ^ 0x7FFFFFFF)
        return pltpu.bitcast(bits, F32)

    def code16_to_bf16(code):
        bits = jnp.where(code >= 0, code, code ^ 0x7FFF)
        return pltpu.bitcast(bits.astype(I16), BF16)

    def spread(accs, ones, rows):
        accs = list(accs)
        for j in range(KC // rows):
            accs[j % N_ACC] = accs[j % N_ACC] + ones[j * rows:(j + 1) * rows, :]
        return tuple(accs)

    def total(accs, dtype):
        acc = functools.reduce(lambda a, b: a + b, accs)
        return jnp.sum(acc.astype(dtype), axis=0, keepdims=True)

    zero16 = (jnp.zeros((PACKED_SUBLANES, QB), I16),) * N_ACC
    zero8 = (jnp.zeros((SUBLANES, QB), F32),) * N_ACC

    def count16_ge(cand):
        def body(m, accs):
            return spread(accs, (sb_s[m] >= cand).astype(I16), PACKED_SUBLANES)

        return total(lax.fori_loop(0, n_ch, body, zero16), I32)

    def bit16_body(it, prefix):
        cand = prefix + jnp.left_shift(jnp.int32(1), 15 - it)
        return jnp.where(count16_ge(code16_to_bf16(cand)) >= topk, cand, prefix)

    p16 = lax.fori_loop(0, 16, bit16_body, jnp.full((1, QB), I16_MIN, I32))

    def fold(accs, flags):
        return spread(accs, jnp.where(flags, 1.0, 0.0), SUBLANES)

    def count_ge(cand):
        return total(lax.fori_loop(0, n_ch, lambda m, accs: fold(accs, sc_s[m] >= cand), zero8), F32)

    below = p16 - 1
    lo = jnp.left_shift(below, 16) + jnp.where(below < 0, 0xFFFF, 0)

    def bit_body(it, state):
        prefix, kept = state
        cand = prefix + jnp.left_shift(jnp.int32(1), L2_BITS - 1 - it)
        cnt = count_ge(code_to_score(cand))
        ok = cnt >= topk
        return jnp.where(ok, cand, prefix), jnp.where(ok, cnt, kept)

    settled_at = jnp.where(p16 == I16_MIN, float(topk), float(topk + 1))
    state = (lo, settled_at)
    done_bits = 0
    for n_bits in L2_PASS_GROUPS:
        first_bit = done_bits

        def run(st, first_bit=first_bit, n_bits=n_bits):
            return lax.fori_loop(first_bit, first_bit + n_bits, bit_body, st)

        if done_bits == 0:
            state = run(state)
        else:
            all_settled = jnp.max(jnp.abs(state[1] - topk)) == 0.0
            state = lax.cond(all_settled, lambda st: st, run, state)
        done_bits += n_bits
    assert done_bits == L2_BITS
    prefix = state[0]
    thr = jnp.where(p16 == I16_MIN, -F32_MAX, code_to_score(prefix))

    def mask_body(m, kept):
        keep = sc_s[m] >= thr
        madd_s[m] = jnp.where(keep, 0.0, NEG_BIG)
        return fold(kept, keep)

    excess = total(lax.fori_loop(0, n_ch, mask_body, zero8), F32) - topk

    @pl.when(jnp.max(excess) > 0)
    def _():
        n_tied = total(lax.fori_loop(0, n_ch, lambda m, accs: fold(accs, sc_s[m] == thr), zero8), F32)
        need = n_tied - excess
        r = lax.broadcasted_iota(I32, (KC, KC), 0)
        c = lax.broadcasted_iota(I32, (KC, KC), 1)
        prefix_op = jnp.where(c <= r, 1.0, 0.0).astype(BF16)

        def fix_body(m, seen):
            tied = sc_s[m] == thr
            t = jnp.where(tied, 1.0, 0.0).astype(BF16)
            rank = seen + jnp.dot(prefix_op, t, preferred_element_type=F32)
            madd_s[m] = jnp.where(tied & (rank > need), NEG_BIG, madd_s[m])
            return rank[KC - 1:KC, :]

        lax.fori_loop(0, n_ch, fix_body, jnp.zeros((1, QB), F32))

    k_copy(i).wait()
    v_copy(i).wait()

    n_far = jnp.maximum(i - 1, 0)
    mx_s[...] = jnp.full(mx_s.shape, -jnp.inf, F32)
    acc_s[...] = jnp.zeros(acc_s.shape, F32)

    w_next = wTn_ref[...] * idx_scale

    def att_body(m, carry, near):
        index_chunk(m, qiTn_ref, w_next, i + 1)
        madd = madd_s[m]
        rows = pl.ds(pl.multiple_of(m * KC, KC), KC)
        for h in range(ATTN_HEADS):
            hd = slice(h * HEAD_DIM, (h + 1) * HEAD_DIM)
            s = jnp.dot(k_ref[rows, hd], qT_ref[hd, :], preferred_element_type=F32)
            if near:
                s = s + (madd + bias_s[i - m, h])
            else:
                s = s + madd
            s_s[h] = s
            mx = mx_s[h]
            m_new = jnp.maximum(mx, jnp.max(s, axis=0, keepdims=True))
            alpha_s[h] = jnp.exp2(mx - m_new)
            mx_s[h] = m_new
        ones_rows = jnp.ones((PACKED_SUBLANES, KC), BF16)
        for h in range(ATTN_HEADS):
            hd = slice(h * HEAD_DIM, (h + 1) * HEAD_DIM)
            p = jnp.exp2(s_s[h] - mx_s[h])
            v_aug = jnp.concatenate([vT_ref[m, hd, :], ones_rows], axis=0)
            pv = jnp.dot(v_aug, p.astype(BF16), preferred_element_type=F32)
            acc_s[h] = alpha_s[h] * acc_s[h] + pv
        return carry

    def far_group(size, start):
        def body(g, carry):
            for j in range(size):
                att_body(start + size * g + j, carry, near=False)
            return carry
        return body

    n_quads = n_far // 4
    lax.fori_loop(0, n_quads, far_group(4, 0), 0)
    lax.fori_loop(0, (n_far % 4) // 2, far_group(2, 4 * n_quads), 0)
    lax.fori_loop(n_far - n_far % 2, n_far, far_group(1, 0), 0)

    @pl.when(i >= 1)
    def _():
        att_body(i - 1, 0, near=True)

    att_body(i, 0, near=True)
    index_chunk(jnp.minimum(i + 1, n_q - 1), qiTn_ref, w_next, i + 1)

    for h in range(ATTN_HEADS):
        acc = acc_s[h]
        out_t = acc[:HEAD_DIM] / acc[HEAD_DIM:HEAD_DIM + 1]
        o_ref[:, h * HEAD_DIM:(h + 1) * HEAD_DIM] = out_t.T.astype(o_ref.dtype)


def _dsa(rel_bias, qT, qiT, wT, kidx, k, vT4, seq, side_weights):
    bsz = qT.shape[0]
    topk = min(TOPK_MAX, seq // 4)
    n_q = seq // QB
    n_steps = bsz * n_q
    assert QB == KC
    kernel = functools.partial(_dsa_kernel, topk=topk, n_q=n_q, n_side=len(side_weights))
    nxt = lambda i: jnp.minimum(i + 1, n_q - 1)
    side_specs = []
    for w in side_weights:
        rows = w.shape[0] // n_steps
        assert rows * n_steps == w.shape[0] and rows % PACKED_SUBLANES == 0
        side_specs.append(pl.BlockSpec((rows, w.shape[1]), lambda b, i: (b * n_q + i, 0)))
    return pl.pallas_call(
        kernel,
        grid=(bsz, n_q),
        in_specs=[
            pl.BlockSpec(memory_space=pltpu.SMEM),
            pl.BlockSpec((None, ATTN_WIDTH, QB), lambda b, i: (b, 0, i)),
            pl.BlockSpec((None, IDX_HEADS * IDX_DIM, QB), lambda b, i: (b, 0, i)),
            pl.BlockSpec((None, IDX_HEADS, QB), lambda b, i: (b, 0, i)),
            pl.BlockSpec((None, IDX_HEADS * IDX_DIM, QB), lambda b, i: (b, 0, nxt(i))),
            pl.BlockSpec((None, IDX_HEADS, QB), lambda b, i: (b, 0, nxt(i))),
            pl.BlockSpec((None, seq, IDX_DIM), lambda b, i: (b, 0, 0)),
            pl.BlockSpec(memory_space=pl.ANY),
            pl.BlockSpec(memory_space=pl.ANY),
        ] + side_specs,
        out_specs=[pl.BlockSpec((None, QB, ATTN_WIDTH), lambda b, i: (b, i, 0))] + side_specs,
        out_shape=[jax.ShapeDtypeStruct((bsz, seq, ATTN_WIDTH), BF16)]
        + [jax.ShapeDtypeStruct(w.shape, BF16) for w in side_weights],
        scratch_shapes=[
            pltpu.VMEM((seq // KC, KC, QB), F32),
            pltpu.VMEM((seq // KC, KC, QB), BF16),
            pltpu.VMEM((seq // KC, KC, QB), F32),
            pltpu.VMEM((2, ATTN_HEADS, KC, QB), F32),
            pltpu.VMEM((ATTN_HEADS, 1, QB), F32),
            pltpu.VMEM((ATTN_HEADS, 1, QB), F32),
            pltpu.VMEM((ATTN_HEADS, HEAD_DIM + PACKED_SUBLANES, QB), F32),
            pltpu.VMEM((ATTN_HEADS, KC, QB), F32),
            pltpu.VMEM((seq, ATTN_WIDTH), BF16),
            pltpu.VMEM((seq // KC, ATTN_WIDTH, KC), BF16),
            pltpu.SemaphoreType.DMA((2, seq // KC)),
        ],
        compiler_params=pltpu.CompilerParams(
            dimension_semantics=("arbitrary", "arbitrary"), vmem_limit_bytes=VMEM_LIMIT),
        name="dsa",
    )(rel_bias, qT, qiT, wT, qiT, wT, kidx, k, vT4, *side_weights)


def _mix_kernel(x_ref, u_ref, halo_ref, attn_ref, wpool_ref, pscale_ref, wout_ref, g_ref,
                x1_ref, h2_ref, ext_s, lvl0_s, lvl1_s, *, seq):
    m = pl.program_id(0)
    tiles_per_seq = seq // TM_MIX
    first = (m % tiles_per_seq) == 0
    ext_s[0:POOL_HALO, :] = jnp.where(first, 0.0, halo_ref[...])
    ext_s[POOL_HALO:, :] = u_ref[...]
    n_ext = TM_MIX + POOL_HALO
    lvl_bufs = (lvl0_s, lvl1_s)

    t_pos = (m % tiles_per_seq) * TM_MIX + lax.broadcasted_iota(I32, (TM_MIX, POOL_GC), 0)
    acc = x_ref[...] + jnp.dot(attn_ref[...], wout_ref[POOL_WIDTH:, :], preferred_element_type=F32)
    for g, w in enumerate(POOL_WINDOWS):
        cols = slice(g * POOL_GC, (g + 1) * POOL_GC)
        u = ext_s[POOL_HALO:, cols]
        levels = w.bit_length() - 1
        assert w == 1 << levels and SUBLANES * levels <= POOL_HALO
        src, src_cols = ext_s, cols
        for lv in range(levels):
            start = SUBLANES * (lv + 1)
            n_rows = n_ext - start
            t = src[pl.ds(start, n_rows), src_cols] + src[pl.ds(start - (1 << lv), n_rows), src_cols]
            if lv == levels - 1:
                wsum = t[POOL_HALO - start:, :]
            else:
                dst = lvl_bufs[lv % 2]
                dst[pl.ds(start, n_rows), :] = t
                src, src_cols = dst, slice(None)
        count = jnp.minimum(t_pos + 1, w).astype(F32)
        pooled = (wsum / count - u).astype(BF16)
        y = jnp.dot(pooled, wpool_ref[g], preferred_element_type=F32) * pscale_ref[:, cols]
        acc = acc + jnp.dot(y.astype(BF16), wout_ref[cols, :], preferred_element_type=F32)
    x1_ref[...] = acc
    h2_ref[...] = _rms(acc, g_ref[...]).astype(BF16)


def _mix(x2d, u_pool, attn2d, w_pool, pool_scale, w_out, g_ffn, seq):
    m = x2d.shape[0]
    halo_per_tile = TM_MIX // POOL_HALO
    kernel = functools.partial(_mix_kernel, seq=seq)
    return pl.pallas_call(
        kernel,
        grid=(m // TM_MIX,),
        in_specs=[
            pl.BlockSpec((TM_MIX, D_MODEL), lambda i: (i, 0)),
            pl.BlockSpec((TM_MIX, POOL_WIDTH), lambda i: (i, 0)),
            pl.BlockSpec((POOL_HALO, POOL_WIDTH), lambda i: (jnp.maximum(i * halo_per_tile - 1, 0), 0)),
            pl.BlockSpec((TM_MIX, ATTN_WIDTH), lambda i: (i, 0)),
            pl.BlockSpec((POOL_GROUPS, POOL_GC, POOL_GC), lambda i: (0, 0, 0)),
            pl.BlockSpec((1, POOL_WIDTH), lambda i: (0, 0)),
            pl.BlockSpec((D_MODEL, D_MODEL), lambda i: (0, 0)),
            pl.BlockSpec((1, D_MODEL), lambda i: (0, 0)),
        ],
        out_specs=[
            pl.BlockSpec((TM_MIX, D_MODEL), lambda i: (i, 0)),
            pl.BlockSpec((TM_MIX, D_MODEL), lambda i: (i, 0)),
        ],
        out_shape=[
            jax.ShapeDtypeStruct((m, D_MODEL), F32),
            jax.ShapeDtypeStruct((m, D_MODEL), BF16),
        ],
        scratch_shapes=[pltpu.VMEM((TM_MIX + POOL_HALO, POOL_WIDTH), F32),
                        pltpu.VMEM((TM_MIX + POOL_HALO, POOL_GC), F32),
                        pltpu.VMEM((TM_MIX + POOL_HALO, POOL_GC), F32)],
        compiler_params=pltpu.CompilerParams(
            dimension_semantics=("arbitrary",), vmem_limit_bytes=VMEM_LIMIT),
        name="mix",
    )(x2d, u_pool, u_pool, attn2d, w_pool, pool_scale, w_out, g_ffn)


def _ffn_kernel(h_ref, wg_ref, wv_ref, cwg_ref, cwv_ref, cbg_ref, cbv_ref, wd_ref,
                o_ref, carry_s, *u_bufs, seq):
    m = pl.program_id(0)
    f = pl.program_id(1)
    tiles_per_seq = seq // TM_FFN
    first = (m % tiles_per_seq) == 0

    @pl.when(f == 0)
    def _():
        o_ref[...] = jnp.zeros(o_ref.shape, F32)

    h = h_ref[...]
    n_half = TF_FFN // TF_HALF

    def up(s):
        cols = slice(s * TF_HALF, (s + 1) * TF_HALF)
        for k, w_ref in enumerate((wg_ref, wv_ref)):
            u = u_bufs[2 * s + k]
            u[0:SUBLANES, :] = jnp.where(first, 0.0, carry_s[f, 2 * s + k])
            u[SUBLANES:, :] = jnp.dot(h, w_ref[:, cols], preferred_element_type=F32)
            carry_s[f, 2 * s + k] = u[TM_FFN:, :]

    def conv(u, cols, cw_ref, cb_ref):
        out = cb_ref[:, cols]
        for j in range(CONV_WIDTH):
            shift = CONV_WIDTH - 1 - j
            out = out + u[pl.ds(SUBLANES - shift, TM_FFN), :] * cw_ref[j:j + 1, cols]
        return out

    def gated(s):
        cols = slice(s * TF_HALF, (s + 1) * TF_HALF)
        gate = conv(u_bufs[2 * s], cols, cwg_ref, cbg_ref)
        val = conv(u_bufs[2 * s + 1], cols, cwv_ref, cbv_ref)
        return (gate * jax.nn.sigmoid(gate) * val).astype(BF16)

    def down(s, act):
        return jnp.dot(act, wd_ref[s * TF_HALF:(s + 1) * TF_HALF, :], preferred_element_type=F32)

    contrib = None
    for s in range(n_half):
        up(s)
        part = down(s, gated(s))
        contrib = part if contrib is None else contrib + part
    o_ref[...] += contrib


def _ffn(h2, w_up, conv_w, conv_b, w_down, seq):
    m = h2.shape[0]
    n_f = D_FF // TF_FFN
    kernel = functools.partial(_ffn_kernel, seq=seq)
    return pl.pallas_call(
        kernel,
        grid=(m // TM_FFN, n_f),
        in_specs=[
            pl.BlockSpec((TM_FFN, D_MODEL), lambda i, f: (i, 0)),
            pl.BlockSpec((D_MODEL, TF_FFN), lambda i, f: (0, f)),
            pl.BlockSpec((D_MODEL, TF_FFN), lambda i, f: (0, f + n_f)),
            pl.BlockSpec((CONV_WIDTH, TF_FFN), lambda i, f: (0, f)),
            pl.BlockSpec((CONV_WIDTH, TF_FFN), lambda i, f: (0, f + n_f)),
            pl.BlockSpec((1, TF_FFN), lambda i, f: (0, f)),
            pl.BlockSpec((1, TF_FFN), lambda i, f: (0, f + n_f)),
            pl.BlockSpec((TF_FFN, D_MODEL), lambda i, f: (f, 0)),
        ],
        out_specs=pl.BlockSpec((TM_FFN, D_MODEL), lambda i, f: (i, 0)),
        out_shape=jax.ShapeDtypeStruct((m, D_MODEL), F32),
        scratch_shapes=[pltpu.VMEM((n_f, 2 * (TF_FFN // TF_HALF), SUBLANES, TF_HALF), F32)]
        + [pltpu.VMEM((TM_FFN + SUBLANES, TF_HALF), F32)] * (2 * (TF_FFN // TF_HALF)),
        compiler_params=pltpu.CompilerParams(
            dimension_semantics=("arbitrary", "arbitrary"), vmem_limit_bytes=VMEM_LIMIT),
        name="ffn",
    )(h2, w_up, w_up, conv_w, conv_w, conv_b, conv_b, w_down)


def _ple_kernel(x1_ref, f_ref, p_ref, gp_ref, wgate_ref, wproj_ref, gf_ref, o_ref, *, last_layer):
    x = x1_ref[...] + f_ref[...]
    hg = _rms(x, gp_ref[...]).astype(BF16)
    gate = jax.nn.sigmoid(jnp.dot(hg, wgate_ref[...], preferred_element_type=F32))
    emb = jnp.dot(p_ref[...].astype(BF16), wproj_ref[...], preferred_element_type=F32)
    x3 = x + emb * gate
    o_ref[...] = _rms(x3, gf_ref[...]) if last_layer else x3


def _ple(x1, ffn_out, p2d, g_ple, w_gate, w_proj, g_final, last_layer):
    m = x1.shape[0]
    return pl.pallas_call(
        functools.partial(_ple_kernel, last_layer=last_layer),
        grid=(m // TM_PLE,),
        in_specs=[
            pl.BlockSpec((TM_PLE, D_MODEL), lambda i: (i, 0)),
            pl.BlockSpec((TM_PLE, D_MODEL), lambda i: (i, 0)),
            pl.BlockSpec((TM_PLE, D_PLE), lambda i: (i, 0)),
            pl.BlockSpec((1, D_MODEL), lambda i: (0, 0)),
            pl.BlockSpec((D_MODEL, D_MODEL), lambda i: (0, 0), pipeline_mode=pl.Buffered(1)),
            pl.BlockSpec((D_PLE, D_MODEL), lambda i: (0, 0)),
            pl.BlockSpec((1, D_MODEL), lambda i: (0, 0)),
        ],
        out_specs=pl.BlockSpec((TM_PLE, D_MODEL), lambda i: (i, 0)),
        out_shape=jax.ShapeDtypeStruct((m, D_MODEL), F32),
        compiler_params=pltpu.CompilerParams(
            dimension_semantics=("arbitrary",), vmem_limit_bytes=VMEM_LIMIT),
        name="ple",
    )(x1, ffn_out, p2d, g_ple, w_gate, w_proj, g_final)


def kernel(x, p, g_mix, w_in, w_pool, pool_scale, rel_bias, w_out, g_ffn, w_up, conv_w, conv_b,
           w_down, g_ple, w_ple_gate, w_ple_proj, g_final):
    bsz, seq, _ = x.shape
    depth = w_in.shape[0]
    assert seq % QB == 0 and seq % TM_MIX == 0 and seq % TM_FFN == 0 and (bsz * seq) % TM_IN == 0
    x2d = x.reshape(bsz * seq, D_MODEL)
    for i in range(depth):
        u_pool, k2d, qT, qiT, vT4, small = _inproj(x2d, g_mix[i].reshape(1, D_MODEL),
                                                   w_in[i].T.astype(BF16), bsz, seq)
        small3 = small.reshape(bsz, seq, SMALL_W)
        kidx = small3[:, :, :IDX_DIM].astype(BF16)
        wT = jnp.swapaxes(small3[:, :, IDX_DIM:IDX_DIM + IDX_HEADS], 1, 2)
        attn, w_out_b, w_up_b, w_down_b, w_gate_b = _dsa(rel_bias, qT, qiT, wT, kidx,
                                               k2d.reshape(bsz, seq, ATTN_WIDTH), vT4, seq,
                                               (w_out[i], w_up[i], w_down[i], w_ple_gate[i]))

        x1, h2 = _mix(x2d, u_pool, attn.reshape(bsz * seq, ATTN_WIDTH), w_pool[i].astype(BF16),
                      pool_scale[i].reshape(1, POOL_WIDTH), w_out_b,
                      g_ffn[i].reshape(1, D_MODEL), seq)
        ffn_out = _ffn(h2, w_up_b, conv_w[i], conv_b[i].reshape(1, 2 * D_FF), w_down_b, seq)
        x2d = _ple(x1, ffn_out, p[i].reshape(bsz * seq, D_PLE), g_ple[i].reshape(1, D_MODEL),
                   w_gate_b, w_ple_proj[i].astype(BF16),
                   g_final.reshape(1, D_MODEL), last_layer=(i == depth - 1))
    return x2d.reshape(bsz, seq, D_MODEL)
```

```python
import functools
import math

import numpy as np
import jax
import jax.numpy as jnp
from jax import lax
from jax.experimental import pallas as pl
from jax.experimental.pallas import tpu as pltpu

F32 = jnp.float32
BF16 = jnp.bfloat16
I32 = jnp.int32
I16 = jnp.int16

D_MODEL = 2048
D_PLE = 256
POOL_WIDTH = 1024
POOL_GROUPS = 4
POOL_GC = 256
POOL_WINDOWS = (2, 4, 8, 16)
ATTN_HEADS = 8
HEAD_DIM = 128
ATTN_WIDTH = 1024
IDX_HEADS = 16
IDX_DIM = 64
TOPK_MAX = 256
N_BUCKETS = 32
MAX_DISTANCE = 128
D_FF = 5632
CONV_WIDTH = 3
EPS = 1e-6
G_POOL, G_Q, G_K, G_V, G_QI = range(5)
N_GROUPS = 5
GROUP_W = 1024

V7X_VMEM_BYTES = 64 * 1024 * 1024
SUBLANES = 8
PACKED_SUBLANES = 16
VMEM_LIMIT = V7X_VMEM_BYTES - 8 * 1024 * 1024

TM_IN = 512
SMALL_W = IDX_DIM + IDX_HEADS
QB = 256
KC = 256
N_ACC = 4
L2_BITS = 17
L2_PASS_GROUPS = (4, 4, 4, 5)
TM_MIX = 512
POOL_HALO = 32
TM_FFN = 1024
TF_FFN = 512
TF_HALF = 256
TM_PLE = 512

I16_MIN = -2 ** 15
NEG_BIG = -1e30
F32_MAX = float(np.finfo(np.float32).max)
LOG2E = math.log2(math.e)
Q_PRESCALE = (HEAD_DIM ** -0.5) * LOG2E


def _bucket_lower_bounds():
    n = np.arange(0, 4 * MAX_DISTANCE, dtype=np.int64)
    max_exact = N_BUCKETS // 2
    nf = np.maximum(n, 1).astype(np.float32)
    large = max_exact + (np.log(nf / np.float32(max_exact)) / np.float32(math.log(MAX_DISTANCE / max_exact))
                         * np.float32(N_BUCKETS - max_exact)).astype(np.int32)
    large = np.minimum(large, N_BUCKETS - 1)
    bucket = np.where(n < max_exact, n, large)
    assert np.all(np.diff(bucket) >= 0)
    lows = [int(np.argmax(bucket == b)) for b in range(N_BUCKETS)]
    assert all(bucket[lo] == b for b, lo in enumerate(lows))
    return lows


BUCKET_LO = _bucket_lower_bounds()


def _rms(x, g):
    ms = jnp.mean(x * x, axis=-1, keepdims=True)
    return (x * lax.rsqrt(ms + EPS)) * g


def _nt_dot(a, b):
    return lax.dot_general(a, b, (((1,), (1,)), ((), ())), preferred_element_type=F32)


def _inproj_kernel(x_ref, g_ref, wt_ref, pool_ref, k_ref, qT_ref, qiT_ref, vT_ref, small_ref):
    hb = _rms(x_ref[...], g_ref[...]).astype(BF16)

    def w(group):
        return wt_ref[group * GROUP_W:(group + 1) * GROUP_W, :]

    pool_ref[...] = _nt_dot(hb, w(G_POOL))
    k_ref[...] = _nt_dot(hb, w(G_K)).astype(BF16)
    small_ref[...] = _nt_dot(hb, wt_ref[N_GROUPS * GROUP_W:, :])
    qT_ref[...] = (_nt_dot(w(G_Q), hb) * Q_PRESCALE).astype(BF16)
    qiT_ref[...] = _nt_dot(w(G_QI), hb).astype(BF16)
    out_t = _nt_dot(w(G_V), hb).astype(BF16)
    for j in range(TM_IN // KC):
        vT_ref[j] = out_t[:, j * KC:(j + 1) * KC]


def _inproj(x2d, g, wt, bsz, seq):
    m = x2d.shape[0]
    tiles_per_seq = seq // TM_IN
    assert wt.shape[0] == N_GROUPS * GROUP_W + SMALL_W
    return pl.pallas_call(
        _inproj_kernel,
        grid=(m // TM_IN,),
        in_specs=[
            pl.BlockSpec((TM_IN, D_MODEL), lambda i: (i, 0)),
            pl.BlockSpec((1, D_MODEL), lambda i: (0, 0)),
            pl.BlockSpec(wt.shape, lambda i: (0, 0), pipeline_mode=pl.Buffered(1)),
        ],
        out_specs=[
            pl.BlockSpec((TM_IN, GROUP_W), lambda i: (i, 0)),
            pl.BlockSpec((TM_IN, GROUP_W), lambda i: (i, 0)),
            pl.BlockSpec((None, GROUP_W, TM_IN), lambda i: (i // tiles_per_seq, 0, i % tiles_per_seq)),
            pl.BlockSpec((None, GROUP_W, TM_IN), lambda i: (i // tiles_per_seq, 0, i % tiles_per_seq)),
            pl.BlockSpec((None, TM_IN // KC, GROUP_W, KC),
                         lambda i: (i // tiles_per_seq, i % tiles_per_seq, 0, 0)),
            pl.BlockSpec((TM_IN, SMALL_W), lambda i: (i, 0)),
        ],
        out_shape=[
            jax.ShapeDtypeStruct((m, POOL_WIDTH), F32),
            jax.ShapeDtypeStruct((m, ATTN_WIDTH), BF16),
            jax.ShapeDtypeStruct((bsz, ATTN_WIDTH, seq), BF16),
            jax.ShapeDtypeStruct((bsz, IDX_HEADS * IDX_DIM, seq), BF16),
            jax.ShapeDtypeStruct((bsz, seq // KC, ATTN_WIDTH, KC), BF16),
            jax.ShapeDtypeStruct((m, SMALL_W), F32),
        ],
        compiler_params=pltpu.CompilerParams(
            dimension_semantics=("arbitrary",), vmem_limit_bytes=VMEM_LIMIT),
        name="inproj",
    )(x2d, g, wt)


def _dsa_kernel(rel_ref, qT_ref, qiT_ref, wT_ref, qiTn_ref, wTn_ref, kidx_ref, k_hbm, vT_hbm, *rest,
                topk, n_q, n_side):
    side_in, rest = rest[:n_side], rest[n_side:]
    o_ref, side_out = rest[0], rest[1:1 + n_side]
    sc_s, sb_s, madd_s, bias_s, mx_s, alpha_s, acc_s, s_s, k_ref, vT_ref, kv_sem = rest[1 + n_side:]
    b = pl.program_id(0)
    i = pl.program_id(1)
    n_ch = i + 1

    def k_copy(m):
        rows = pl.ds(pl.multiple_of(m * KC, KC), KC)
        return pltpu.make_async_copy(k_hbm.at[b, rows, :], k_ref.at[rows, :], kv_sem.at[0, m])

    def v_copy(m):
        return pltpu.make_async_copy(vT_hbm.at[b, m], vT_ref.at[m], kv_sem.at[1, m])

    @pl.when(i == 0)
    def _():
        for m in range(n_q):
            k_copy(m).start()
            v_copy(m).start()

    for src, dst in zip(side_in, side_out):
        dst[...] = src[...].astype(dst.dtype)

    @pl.when((b == 0) & (i == 0))
    def _():
        r = lax.broadcasted_iota(I32, (KC, QB), 0)
        c = lax.broadcasted_iota(I32, (KC, QB), 1)
        for off in range(2):
            dist = jnp.maximum(c - r + off * QB, 0)

            def head_body(h, carry, dist=dist, off=off):
                t = jnp.full((KC, QB), rel_ref[0, h], F32)
                for bk in range(1, N_BUCKETS):
                    t = jnp.where(dist >= BUCKET_LO[bk], rel_ref[bk, h], t)
                bias_s[off, h] = (t - rel_ref[N_BUCKETS - 1, h]) * LOG2E
                return carry

            lax.fori_loop(0, ATTN_HEADS, head_body, 0)

    idx_scale = (IDX_HEADS ** -0.5) * (IDX_DIM ** -0.5)
    row = lax.broadcasted_iota(I32, (KC, QB), 0)
    col = lax.broadcasted_iota(I32, (KC, QB), 1)

    def index_chunk(m, qi_ref, w_scaled, q_block):
        kc = kidx_ref[pl.ds(pl.multiple_of(m * KC, KC), KC), :]
        acc = jnp.zeros((KC, QB), F32)
        for h in range(IDX_HEADS):
            d = jnp.dot(kc, qi_ref[h * IDX_DIM:(h + 1) * IDX_DIM, :],
                        preferred_element_type=F32)
            acc = acc + w_scaled[h:h + 1, :] * jnp.maximum(d, 0.0)
        causal = (m * KC + row) <= (q_block * QB + col)
        score = jnp.where(causal, acc, -jnp.inf)
        sc_s[m] = score
        sb_s[m] = score.astype(BF16)

    @pl.when(i == 0)
    def _():
        index_chunk(0, qiT_ref, wT_ref[...] * idx_scale, 0)

    def code_to_score(code):
        bits = jnp.where(code >= 0, code, code ^ 0x7FFFFFFF)
        return pltpu.bitcast(bits, F32)

    def code16_to_bf16(code):
        bits = jnp.where(code >= 0, code, code ^ 0x7FFF)
        return pltpu.bitcast(bits.astype(I16), BF16)

    def spread(accs, ones, rows):
        accs = list(accs)
        for j in range(KC // rows):
            accs[j % N_ACC] = accs[j % N_ACC] + ones[j * rows:(j + 1) * rows, :]
        return tuple(accs)

    def total(accs, dtype):
        acc = functools.reduce(lambda a, b: a + b, accs)
        return jnp.sum(acc.astype(dtype), axis=0, keepdims=True)

    zero16 = (jnp.zeros((PACKED_SUBLANES, QB), I16),) * N_ACC
    zero8 = (jnp.zeros((SUBLANES, QB), F32),) * N_ACC

    def count16_ge(cand):
        def body(m, accs):
            return spread(accs, (sb_s[m] >= cand).astype(I16), PACKED_SUBLANES)

        return total(lax.fori_loop(0, n_ch, body, zero16), I32)

    def bit16_body(it, prefix):
        cand = prefix + jnp.left_shift(jnp.int32(1), 15 - it)
        return jnp.where(count16_ge(code16_to_bf16(cand)) >= topk, cand, prefix)

    p16 = lax.fori_loop(0, 16, bit16_body, jnp.full((1, QB), I16_MIN, I32))

    def fold(accs, flags):
        return spread(accs, jnp.where(flags, 1.0, 0.0), SUBLANES)

    def count_ge(cand):
        return total(lax.fori_loop(0, n_ch, lambda m, accs: fold(accs, sc_s[m] >= cand), zero8), F32)

    below = p16 - 1
    lo = jnp.left_shift(below, 16) + jnp.where(below < 0, 0xFFFF, 0)

    def bit_body(it, state):
        prefix, kept = state
        cand = prefix + jnp.left_shift(jnp.int32(1), L2_BITS - 1 - it)
        cnt = count_ge(code_to_score(cand))
        ok = cnt >= topk
        return jnp.where(ok, cand, prefix), jnp.where(ok, cnt, kept)

    settled_at = jnp.where(p16 == I16_MIN, float(topk), float(topk + 1))
    state = (lo, settled_at)
    done_bits = 0
    for n_bits in L2_PASS_GROUPS:
        first_bit = done_bits

        def run(st, first_bit=first_bit, n_bits=n_bits):
            return lax.fori_loop(first_bit, first_bit + n_bits, bit_body, st)

        if done_bits == 0:
            state = run(state)
        else:
            all_settled = jnp.max(jnp.abs(state[1] - topk)) == 0.0
            state = lax.cond(all_settled, lambda st: st, run, state)
        done_bits += n_bits
    assert done_bits == L2_BITS
    prefix = state[0]
    thr = jnp.where(p16 == I16_MIN, -F32_MAX, code_to_score(prefix))

    def mask_body(m, kept):
        keep = sc_s[m] >= thr
        madd_s[m] = jnp.where(keep, 0.0, NEG_BIG)
        return fold(kept, keep)

    excess = total(lax.fori_loop(0, n_ch, mask_body, zero8), F32) - topk

    @pl.when(jnp.max(excess) > 0)
    def _():
        n_tied = total(lax.fori_loop(0, n_ch, lambda m, accs: fold(accs, sc_s[m] == thr), zero8), F32)
        need = n_tied - excess
        r = lax.broadcasted_iota(I32, (KC, KC), 0)
        c = lax.broadcasted_iota(I32, (KC, KC), 1)
        prefix_op = jnp.where(c <= r, 1.0, 0.0).astype(BF16)

        def fix_body(m, seen):
            tied = sc_s[m] == thr
            t = jnp.where(tied, 1.0, 0.0).astype(BF16)
            rank = seen + jnp.dot(prefix_op, t, preferred_element_type=F32)
            madd_s[m] = jnp.where(tied & (rank > need), NEG_BIG, madd_s[m])
            return rank[KC - 1:KC, :]

        lax.fori_loop(0, n_ch, fix_body, jnp.zeros((1, QB), F32))

    k_copy(i).wait()
    v_copy(i).wait()

    n_far = jnp.maximum(i - 1, 0)
    mx_s[...] = jnp.full(mx_s.shape, -jnp.inf, F32)
    acc_s[...] = jnp.zeros(acc_s.shape, F32)

    w_next = wTn_ref[...] * idx_scale

    def att_body(m, carry, near):
        index_chunk(m, qiTn_ref, w_next, i + 1)
        madd = madd_s[m]
        rows = pl.ds(pl.multiple_of(m * KC, KC), KC)
        for h in range(ATTN_HEADS):
            hd = slice(h * HEAD_DIM, (h + 1) * HEAD_DIM)
            s = jnp.dot(k_ref[rows, hd], qT_ref[hd, :], preferred_element_type=F32)
            if near:
                s = s + (madd + bias_s[i - m, h])
            else:
                s = s + madd
            s_s[h] = s
            mx = mx_s[h]
            m_new = jnp.maximum(mx, jnp.max(s, axis=0, keepdims=True))
            alpha_s[h] = jnp.exp2(mx - m_new)
            mx_s[h] = m_new
        ones_rows = jnp.ones((PACKED_SUBLANES, KC), BF16)
        for h in range(ATTN_HEADS):
            hd = slice(h * HEAD_DIM, (h + 1) * HEAD_DIM)
            p = jnp.exp2(s_s[h] - mx_s[h])
            v_aug = jnp.concatenate([vT_ref[m, hd, :], ones_rows], axis=0)
            pv = jnp.dot(v_aug, p.astype(BF16), preferred_element_type=F32)
            acc_s[h] = alpha_s[h] * acc_s[h] + pv
        return carry

    def far_group(size, start):
        def body(g, carry):
            for j in range(size):
                att_body(start + size * g + j, carry, near=False)
            return carry
        return body

    n_quads = n_far // 4
    lax.fori_loop(0, n_quads, far_group(4, 0), 0)
    lax.fori_loop(0, (n_far % 4) // 2, far_group(2, 4 * n_quads), 0)
    lax.fori_loop(n_far - n_far % 2, n_far, far_group(1, 0), 0)

    @pl.when(i >= 1)
    def _():
        att_body(i - 1, 0, near=True)

    att_body(i, 0, near=True)
    index_chunk(jnp.minimum(i + 1, n_q - 1), qiTn_ref, w_next, i + 1)

    for h in range(ATTN_HEADS):
        acc = acc_s[h]
        out_t = acc[:HEAD_DIM] / acc[HEAD_DIM:HEAD_DIM + 1]
        o_ref[:, h * HEAD_DIM:(h + 1) * HEAD_DIM] = out_t.T.astype(o_ref.dtype)


def _dsa(rel_bias, qT, qiT, wT, kidx, k, vT4, seq, side_weights):
    bsz = qT.shape[0]
    topk = min(TOPK_MAX, seq // 4)
    n_q = seq // QB
    n_steps = bsz * n_q
    assert QB == KC
    kernel = functools.partial(_dsa_kernel, topk=topk, n_q=n_q, n_side=len(side_weights))
    nxt = lambda i: jnp.minimum(i + 1, n_q - 1)
    side_specs = []
    for w in side_weights:
        rows = w.shape[0] // n_steps
        assert rows * n_steps == w.shape[0] and rows % PACKED_SUBLANES == 0
        side_specs.append(pl.BlockSpec((rows, w.shape[1]), lambda b, i: (b * n_q + i, 0)))
    return pl.pallas_call(
        kernel,
        grid=(bsz, n_q),
        in_specs=[
            pl.BlockSpec(memory_space=pltpu.SMEM),
            pl.BlockSpec((None, ATTN_WIDTH, QB), lambda b, i: (b, 0, i)),
            pl.BlockSpec((None, IDX_HEADS * IDX_DIM, QB), lambda b, i: (b, 0, i)),
            pl.BlockSpec((None, IDX_HEADS, QB), lambda b, i: (b, 0, i)),
            pl.BlockSpec((None, IDX_HEADS * IDX_DIM, QB), lambda b, i: (b, 0, nxt(i))),
            pl.BlockSpec((None, IDX_HEADS, QB), lambda b, i: (b, 0, nxt(i))),
            pl.BlockSpec((None, seq, IDX_DIM), lambda b, i: (b, 0, 0)),
            pl.BlockSpec(memory_space=pl.ANY),
            pl.BlockSpec(memory_space=pl.ANY),
        ] + side_specs,
        out_specs=[pl.BlockSpec((None, QB, ATTN_WIDTH), lambda b, i: (b, i, 0))] + side_specs,
        out_shape=[jax.ShapeDtypeStruct((bsz, seq, ATTN_WIDTH), BF16)]
        + [jax.ShapeDtypeStruct(w.shape, BF16) for w in side_weights],
        scratch_shapes=[
            pltpu.VMEM((seq // KC, KC, QB), F32),
            pltpu.VMEM((seq // KC, KC, QB), BF16),
            pltpu.VMEM((seq // KC, KC, QB), F32),
            pltpu.VMEM((2, ATTN_HEADS, KC, QB), F32),
            pltpu.VMEM((ATTN_HEADS, 1, QB), F32),
            pltpu.VMEM((ATTN_HEADS, 1, QB), F32),
            pltpu.VMEM((ATTN_HEADS, HEAD_DIM + PACKED_SUBLANES, QB), F32),
            pltpu.VMEM((ATTN_HEADS, KC, QB), F32),
            pltpu.VMEM((seq, ATTN_WIDTH), BF16),
            pltpu.VMEM((seq // KC, ATTN_WIDTH, KC), BF16),
            pltpu.SemaphoreType.DMA((2, seq // KC)),
        ],
        compiler_params=pltpu.CompilerParams(
            dimension_semantics=("arbitrary", "arbitrary"), vmem_limit_bytes=VMEM_LIMIT),
        name="dsa",
    )(rel_bias, qT, qiT, wT, qiT, wT, kidx, k, vT4, *side_weights)


def _mix_kernel(x_ref, u_ref, halo_ref, attn_ref, wpool_ref, pscale_ref, wout_ref, g_ref,
                x1_ref, h2_ref, ext_s, lvl0_s, lvl1_s, *, seq):
    m = pl.program_id(0)
    tiles_per_seq = seq // TM_MIX
    first = (m % tiles_per_seq) == 0
    ext_s[0:POOL_HALO, :] = jnp.where(first, 0.0, halo_ref[...])
    ext_s[POOL_HALO:, :] = u_ref[...]
    n_ext = TM_MIX + POOL_HALO
    lvl_bufs = (lvl0_s, lvl1_s)

    t_pos = (m % tiles_per_seq) * TM_MIX + lax.broadcasted_iota(I32, (TM_MIX, POOL_GC), 0)
    acc = x_ref[...] + jnp.dot(attn_ref[...], wout_ref[POOL_WIDTH:, :], preferred_element_type=F32)
    for g, w in enumerate(POOL_WINDOWS):
        cols = slice(g * POOL_GC, (g + 1) * POOL_GC)
        u = ext_s[POOL_HALO:, cols]
        levels = w.bit_length() - 1
        assert w == 1 << levels and SUBLANES * levels <= POOL_HALO
        src, src_cols = ext_s, cols
        for lv in range(levels):
            start = SUBLANES * (lv + 1)
            n_rows = n_ext - start
            t = src[pl.ds(start, n_rows), src_cols] + src[pl.ds(start - (1 << lv), n_rows), src_cols]
            if lv == levels - 1:
                wsum = t[POOL_HALO - start:, :]
            else:
                dst = lvl_bufs[lv % 2]
                dst[pl.ds(start, n_rows), :] = t
                src, src_cols = dst, slice(None)
        count = jnp.minimum(t_pos + 1, w).astype(F32)
        pooled = (wsum / count - u).astype(BF16)
        y = jnp.dot(pooled, wpool_ref[g], preferred_element_type=F32) * pscale_ref[:, cols]
        acc = acc + jnp.dot(y.astype(BF16), wout_ref[cols, :], preferred_element_type=F32)
    x1_ref[...] = acc
    h2_ref[...] = _rms(acc, g_ref[...]).astype(BF16)


def _mix(x2d, u_pool, attn2d, w_pool, pool_scale, w_out, g_ffn, seq):
    m = x2d.shape[0]
    halo_per_tile = TM_MIX // POOL_HALO
    kernel = functools.partial(_mix_kernel, seq=seq)
    return pl.pallas_call(
        kernel,
        grid=(m // TM_MIX,),
        in_specs=[
            pl.BlockSpec((TM_MIX, D_MODEL), lambda i: (i, 0)),
            pl.BlockSpec((TM_MIX, POOL_WIDTH), lambda i: (i, 0)),
            pl.BlockSpec((POOL_HALO, POOL_WIDTH), lambda i: (jnp.maximum(i * halo_per_tile - 1, 0), 0)),
            pl.BlockSpec((TM_MIX, ATTN_WIDTH), lambda i: (i, 0)),
            pl.BlockSpec((POOL_GROUPS, POOL_GC, POOL_GC), lambda i: (0, 0, 0)),
            pl.BlockSpec((1, POOL_WIDTH), lambda i: (0, 0)),
            pl.BlockSpec((D_MODEL, D_MODEL), lambda i: (0, 0)),
            pl.BlockSpec((1, D_MODEL), lambda i: (0, 0)),
        ],
        out_specs=[
            pl.BlockSpec((TM_MIX, D_MODEL), lambda i: (i, 0)),
            pl.BlockSpec((TM_MIX, D_MODEL), lambda i: (i, 0)),
        ],
        out_shape=[
            jax.ShapeDtypeStruct((m, D_MODEL), F32),
            jax.ShapeDtypeStruct((m, D_MODEL), BF16),
        ],
        scratch_shapes=[pltpu.VMEM((TM_MIX + POOL_HALO, POOL_WIDTH), F32),
                        pltpu.VMEM((TM_MIX + POOL_HALO, POOL_GC), F32),
                        pltpu.VMEM((TM_MIX + POOL_HALO, POOL_GC), F32)],
        compiler_params=pltpu.CompilerParams(
            dimension_semantics=("arbitrary",), vmem_limit_bytes=VMEM_LIMIT),
        name="mix",
    )(x2d, u_pool, u_pool, attn2d, w_pool, pool_scale, w_out, g_ffn)


def _ffn_kernel(h_ref, wg_ref, wv_ref, cwg_ref, cwv_ref, cbg_ref, cbv_ref, wd_ref,
                o_ref, carry_s, *u_bufs, seq):
    m = pl.program_id(0)
    f = pl.program_id(1)
    tiles_per_seq = seq // TM_FFN
    first = (m % tiles_per_seq) == 0

    @pl.when(f == 0)
    def _():
        o_ref[...] = jnp.zeros(o_ref.shape, F32)

    n_half = TF_FFN // TF_HALF

    def up(s):
        cols = slice(s * TF_HALF, (s + 1) * TF_HALF)
        for k, w_ref in enumerate((wg_ref, wv_ref)):
            u = u_bufs[2 * s + k]
            u[0:SUBLANES, :] = jnp.where(first, 0.0, carry_s[f, 2 * s + k])
            u[SUBLANES:, :] = jnp.dot(h_ref[...], w_ref[:, cols], preferred_element_type=F32)
            carry_s[f, 2 * s + k] = u[TM_FFN:, :]

    def conv(u, cols, cw_ref, cb_ref):
        out = cb_ref[:, cols]
        for j in range(CONV_WIDTH):
            shift = CONV_WIDTH - 1 - j
            out = out + u[pl.ds(SUBLANES - shift, TM_FFN), :] * cw_ref[j:j + 1, cols]
        return out

    def gated(s):
        cols = slice(s * TF_HALF, (s + 1) * TF_HALF)
        gate = conv(u_bufs[2 * s], cols, cwg_ref, cbg_ref)
        val = conv(u_bufs[2 * s + 1], cols, cwv_ref, cbv_ref)
        return (gate * jax.nn.sigmoid(gate) * val).astype(BF16)

    def down(s, act):
        return jnp.dot(act, wd_ref[s * TF_HALF:(s + 1) * TF_HALF, :], preferred_element_type=F32)

    contrib = None
    for s in range(n_half):
        up(s)
        part = down(s, gated(s))
        contrib = part if contrib is None else contrib + part
    o_ref[...] += contrib


def _ffn(h2, w_up, conv_w, conv_b, w_down, seq):
    m = h2.shape[0]
    n_f = D_FF // TF_FFN
    kernel = functools.partial(_ffn_kernel, seq=seq)
    return pl.pallas_call(
        kernel,
        grid=(m // TM_FFN, n_f),
        in_specs=[
            pl.BlockSpec((TM_FFN, D_MODEL), lambda i, f: (i, 0)),
            pl.BlockSpec((D_MODEL, TF_FFN), lambda i, f: (0, f)),
            pl.BlockSpec((D_MODEL, TF_FFN), lambda i, f: (0, f + n_f)),
            pl.BlockSpec((CONV_WIDTH, TF_FFN), lambda i, f: (0, f)),
            pl.BlockSpec((CONV_WIDTH, TF_FFN), lambda i, f: (0, f + n_f)),
            pl.BlockSpec((1, TF_FFN), lambda i, f: (0, f)),
            pl.BlockSpec((1, TF_FFN), lambda i, f: (0, f + n_f)),
            pl.BlockSpec((TF_FFN, D_MODEL), lambda i, f: (f, 0)),
        ],
        out_specs=pl.BlockSpec((TM_FFN, D_MODEL), lambda i, f: (i, 0)),
        out_shape=jax.ShapeDtypeStruct((m, D_MODEL), F32),
        scratch_shapes=[pltpu.VMEM((n_f, 2 * (TF_FFN // TF_HALF), SUBLANES, TF_HALF), F32)]
        + [pltpu.VMEM((TM_FFN + SUBLANES, TF_HALF), F32)] * (2 * (TF_FFN // TF_HALF)),
        compiler_params=pltpu.CompilerParams(
            dimension_semantics=("arbitrary", "arbitrary"), vmem_limit_bytes=VMEM_LIMIT),
        name="ffn",
    )(h2, w_up, w_up, conv_w, conv_w, conv_b, conv_b, w_down)


def _ple_kernel(x1_ref, f_ref, p_ref, gp_ref, wgate_ref, wproj_ref, gf_ref, o_ref, *, last_layer):
    x = x1_ref[...] + f_ref[...]
    hg = _rms(x, gp_ref[...]).astype(BF16)
    gate = jax.nn.sigmoid(jnp.dot(hg, wgate_ref[...], preferred_element_type=F32))
    emb = jnp.dot(p_ref[...].astype(BF16), wproj_ref[...], preferred_element_type=F32)
    x3 = x + emb * gate
    o_ref[...] = _rms(x3, gf_ref[...]) if last_layer else x3


def _ple(x1, ffn_out, p2d, g_ple, w_gate, w_proj, g_final, last_layer):
    m = x1.shape[0]
    return pl.pallas_call(
        functools.partial(_ple_kernel, last_layer=last_layer),
        grid=(m // TM_PLE,),
        in_specs=[
            pl.BlockSpec((TM_PLE, D_MODEL), lambda i: (i, 0)),
            pl.BlockSpec((TM_PLE, D_MODEL), lambda i: (i, 0)),
            pl.BlockSpec((TM_PLE, D_PLE), lambda i: (i, 0)),
            pl.BlockSpec((1, D_MODEL), lambda i: (0, 0)),
            pl.BlockSpec((D_MODEL, D_MODEL), lambda i: (0, 0), pipeline_mode=pl.Buffered(1)),
            pl.BlockSpec((D_PLE, D_MODEL), lambda i: (0, 0)),
            pl.BlockSpec((1, D_MODEL), lambda i: (0, 0)),
        ],
        out_specs=pl.BlockSpec((TM_PLE, D_MODEL), lambda i: (i, 0)),
        out_shape=jax.ShapeDtypeStruct((m, D_MODEL), F32),
        compiler_params=pltpu.CompilerParams(
            dimension_semantics=("arbitrary",), vmem_limit_bytes=VMEM_LIMIT),
        name="ple",
    )(x1, ffn_out, p2d, g_ple, w_gate, w_proj, g_final)


def kernel(x, p, g_mix, w_in, w_pool, pool_scale, rel_bias, w_out, g_ffn, w_up, conv_w, conv_b,
           w_down, g_ple, w_ple_gate, w_ple_proj, g_final):
    bsz, seq, _ = x.shape
    depth = w_in.shape[0]
    assert seq % QB == 0 and seq % TM_MIX == 0 and seq % TM_FFN == 0 and (bsz * seq) % TM_IN == 0
    x2d = x.reshape(bsz * seq, D_MODEL)
    for i in range(depth):
        u_pool, k2d, qT, qiT, vT4, small = _inproj(x2d, g_mix[i].reshape(1, D_MODEL),
                                                   w_in[i].T.astype(BF16), bsz, seq)
        small3 = small.reshape(bsz, seq, SMALL_W)
        kidx = small3[:, :, :IDX_DIM].astype(BF16)
        wT = jnp.swapaxes(small3[:, :, IDX_DIM:IDX_DIM + IDX_HEADS], 1, 2)
        attn, w_out_b, w_up_b, w_down_b, w_gate_b = _dsa(rel_bias, qT, qiT, wT, kidx,
                                               k2d.reshape(bsz, seq, ATTN_WIDTH), vT4, seq,
                                               (w_out[i], w_up[i], w_down[i], w_ple_gate[i]))

        x1, h2 = _mix(x2d, u_pool, attn.reshape(bsz * seq, ATTN_WIDTH), w_pool[i].astype(BF16),
                      pool_scale[i].reshape(1, POOL_WIDTH), w_out_b,
                      g_ffn[i].reshape(1, D_MODEL), seq)
        ffn_out = _ffn(h2, w_up_b, conv_w[i], conv_b[i].reshape(1, 2 * D_FF), w_down_b, seq)
        x2d = _ple(x1, ffn_out, p[i].reshape(bsz * seq, D_PLE), g_ple[i].reshape(1, D_MODEL),
                   w_gate_b, w_ple_proj[i].astype(BF16),
                   g_final.reshape(1, D_MODEL), last_layer=(i == depth - 1))
    return x2d.reshape(bsz, seq, D_MODEL)
```
